```python
import math
import jax, jax.numpy as jnp
from jax import lax
import numpy as np

D_MODEL = 4096
BATCH = 1
SEQ = 16384
DEPTH = 4

GRID_W = 64
CTX_LEN = 256
N_MIXERS = 3
N_DA_LAYERS = (DEPTH + 2) // 3
N_RW_LAYERS = (DEPTH + 1) // 3
N_GA_LAYERS = DEPTH // 3
ADA_RANK = 256
N_MOD = 6
D_FF = 11008
CONV_W = 3
Q_BLOCK = 128
ROPE_THETA = 10000.0
NORM_EPS = 1e-6
DA_HEAD_DIM = 128
DA_HEADS = D_MODEL // (2 * DA_HEAD_DIM)
GA_HEAD_DIM = 128
GA_HEADS = D_MODEL // GA_HEAD_DIM
GA_KV_HEADS = GA_HEADS // 4
GA_GROUP = GA_HEADS // GA_KV_HEADS
RW_HEAD_DIM = 64
RW_HEADS = D_MODEL // RW_HEAD_DIM
RW_DECAY_LORA = max(32, int(round(1.8 * D_MODEL ** 0.5 / 32)) * 32)
RW_ICLR_LORA = max(32, int(round(1.8 * D_MODEL ** 0.5 / 32)) * 32)
RW_GATE_LORA = max(32, int(round(0.6 * D_MODEL ** 0.8 / 32)) * 32)
RW_LN_EPS = 64e-5

kernel_name = 'hybrid_diffattn_rwkv7_gqa_convglu_dit'


def rms_norm(z, g):
    zf = z.astype(jnp.float32)
    y = zf * lax.rsqrt(jnp.mean(zf * zf, axis=-1, keepdims=True) + NORM_EPS)
    return (y * g.astype(jnp.float32)).astype(z.dtype)


def modulation(cond, down, up, bias):
    m = (jax.nn.silu(cond) @ down) @ up + bias
    return jnp.split(m, N_MOD, axis=-1)


def modulate(h, shift, scale):
    return h * (1.0 + scale) + shift


def axial_rope_tables(rows, head_dim):
    row = jnp.repeat(jnp.arange(rows, dtype=jnp.float32), GRID_W)
    col = jnp.tile(jnp.arange(GRID_W, dtype=jnp.float32), rows)
    n_freq = head_dim // 4
    inv_freq = ROPE_THETA ** (-jnp.arange(n_freq, dtype=jnp.float32) / n_freq)
    ang = jnp.concatenate([row[:, None] * inv_freq, col[:, None] * inv_freq], axis=-1)
    return jnp.cos(ang), jnp.sin(ang)


def apply_rope(z, cos, sin):
    shape = (1, cos.shape[0]) + (1,) * (z.ndim - 3) + (cos.shape[1],)
    cs, sn = cos.reshape(shape), sin.reshape(shape)
    z1, z2 = jnp.split(z.astype(jnp.float32), 2, axis=-1)
    return jnp.concatenate([z1 * cs - z2 * sn, z1 * sn + z2 * cs], axis=-1).astype(z.dtype)


def sweep_query_blocks(q, attend_block):
    b, t = q.shape[:2]
    nb = t // Q_BLOCK
    qb = jnp.moveaxis(q.reshape((b, nb, Q_BLOCK) + q.shape[2:]), 1, 0)
    ob = lax.map(attend_block, qb)
    return jnp.moveaxis(ob, 0, 1).reshape((b, t) + ob.shape[3:])


def diff_lambda_init(layer_idx):
    return 0.8 - 0.6 * math.exp(-0.3 * layer_idx)


def diff_attention(hc, hl, cos, sin, wqkv, wo, q_g, k_g, lam_vecs, subln_g, lam_init, need_ctx):
    scale = DA_HEAD_DIM ** -0.5

    def project(h, rotate):
        b, t, _ = h.shape
        q, k, v = jnp.split(h @ wqkv, 3, axis=-1)
        q = rms_norm(q.reshape(b, t, DA_HEADS, 2, DA_HEAD_DIM), q_g)
        k = rms_norm(k.reshape(b, t, DA_HEADS, 2, DA_HEAD_DIM), k_g)
        if rotate:
            q, k = apply_rope(q, cos, sin), apply_rope(k, cos, sin)
        return q, k, v.reshape(b, t, DA_HEADS, 2 * DA_HEAD_DIM)

    lf = lam_vecs.astype(jnp.float32)
    lam = jnp.exp(jnp.sum(lf[0] * lf[1])) - jnp.exp(jnp.sum(lf[2] * lf[3])) + lam_init

    def attend(q, k, v):
        s = jnp.einsum('bqhmd,bkhmd->bhmqk', q, k).astype(jnp.float32) * scale
        p = jax.nn.softmax(s, axis=-1)
        a = p[:, :, 0] - lam * p[:, :, 1]
        return jnp.einsum('bhqk,bkhe->bqhe', a.astype(v.dtype), v)

    def finish(o):
        b, t = o.shape[:2]
        o = rms_norm(o, subln_g) * (1.0 - lam_init)
        return o.reshape(b, t, D_MODEL) @ wo

    qc, kc, vc = project(hc, False)
    ql, kl, vl = project(hl, True)
    k_all = jnp.concatenate([kc, kl], axis=1)
    v_all = jnp.concatenate([vc, vl], axis=1)
    out_l = finish(sweep_query_blocks(ql, lambda qb: attend(qb, k_all, v_all)))
    out_c = finish(attend(qc, kc, vc)) if need_ctx else None
    return out_c, out_l


def gqa_attention(hc, hl, cos, sin, wqkv, wo, q_g, k_g, need_ctx):
    scale = GA_HEAD_DIM ** -0.5
    kv_w = GA_KV_HEADS * GA_HEAD_DIM

    def project(h, rotate):
        b, t, _ = h.shape
        q, k, v = jnp.split(h @ wqkv, [D_MODEL, D_MODEL + kv_w], axis=-1)
        q = rms_norm(q.reshape(b, t, GA_KV_HEADS, GA_GROUP, GA_HEAD_DIM), q_g)
        k = rms_norm(k.reshape(b, t, GA_KV_HEADS, GA_HEAD_DIM), k_g)
        if rotate:
            q, k = apply_rope(q, cos, sin), apply_rope(k, cos, sin)
        return q, k, v.reshape(b, t, GA_KV_HEADS, GA_HEAD_DIM)

    def attend(q, k, v):
        s = jnp.einsum('bqhgd,bkhd->bhgqk', q, k).astype(jnp.float32) * scale
        p = jax.nn.softmax(s, axis=-1)
        return jnp.einsum('bhgqk,bkhd->bqhgd', p.astype(v.dtype), v)

    def finish(o):
        b, t = o.shape[:2]
        return o.reshape(b, t, D_MODEL) @ wo

    qc, kc, vc = project(hc, False)
    ql, kl, vl = project(hl, True)
    k_all = jnp.concatenate([kc, kl], axis=1)
    v_all = jnp.concatenate([vc, vl], axis=1)
    out_l = finish(sweep_query_blocks(ql, lambda qb: attend(qb, k_all, v_all)))
    out_c = finish(attend(qc, kc, vc)) if need_ctx else None
    return out_c, out_l


def wkv7_scan(state0, r, w, k, v, a, b, reverse):
    def step(S, inp):
        r_t, w_t, k_t, v_t, a_t, b_t = inp
        sa = jnp.einsum('bhvk,bhk->bhv', S, a_t)
        S = S * w_t[:, :, None, :] + sa[..., None] * b_t[:, :, None, :] + v_t[..., None] * k_t[:, :, None, :]
        return S, jnp.einsum('bhvk,bhk->bhv', S, r_t)

    xs = tuple(z.swapaxes(0, 1) for z in (r, w, k, v, a, b))
    S, y = lax.scan(step, state0, xs, reverse=reverse)
    return S, y.swapaxes(0, 1)


def rwkv7_time_mix(hc, hl, mix, wrkv, wo, w0, w1, w2, a0, a1, a2, g1, g2, k_k, k_a, r_k, ln_w, ln_b, need_ctx):
    f32 = jnp.float32

    def heads(z):
        return z.reshape(z.shape[:-1] + (RW_HEADS, RW_HEAD_DIM)).astype(f32)

    def prepare(h):
        hp = jnp.pad(h, ((0, 0), (1, 1), (0, 0)))
        xx = 0.5 * (hp[:, :-2] + hp[:, 2:]) - h
        xr, xw, xk, xv, xa, xg = (h + xx * mix[n] for n in range(6))
        r, k, v = jnp.einsum('nbtd,nde->nbte', jnp.stack([xr, xk, xv]), wrkv)
        wl = -jax.nn.softplus(-(w0[:, None, None, :] + jnp.einsum('nbtr,nrd->nbtd', jnp.tanh(jnp.einsum('btd,ndr->nbtr', xw, w1)), w2))) - 0.5
        decay = jnp.exp(-jnp.exp(wl.astype(f32)))
        iclr = jax.nn.sigmoid(a0[:, None, None, :] + jnp.einsum('nbtr,nrd->nbtd', jnp.einsum('btd,ndr->nbtr', xa, a1), a2))
        gate = jax.nn.sigmoid(xg @ g1) @ g2
        kk = heads(k * k_k)
        kk = kk * lax.rsqrt(jnp.maximum(jnp.sum(kk * kk, axis=-1, keepdims=True), 1e-24))
        k_dir = heads(k[None] * (1.0 + (iclr - 1.0) * k_a))
        return {'r': heads(r), 'w': heads(decay), 'k': k_dir, 'v': heads(v),
                'a': -kk, 'b': kk[None] * heads(iclr), 'g': gate}

    def run(p, n, s0, reverse):
        return wkv7_scan(s0, p['r'], p['w'][n], p['k'][n], p['v'], p['a'], p['b'][n], reverse)

    def finish(p, y, h):
        b, t = h.shape[:2]
        mu = jnp.mean(y, axis=-1, keepdims=True)
        var = jnp.mean(jnp.square(y - mu), axis=-1, keepdims=True)
        y = ((y - mu) * lax.rsqrt(var + RW_LN_EPS)).reshape(b, t, D_MODEL) * ln_w + ln_b
        bonus = jnp.sum(p['r'] * (p['k'][0] + p['k'][1]) * r_k, axis=-1, keepdims=True) * p['v']
        y = y + bonus.reshape(b, t, D_MODEL)
        return (y * p['g']).astype(h.dtype) @ wo

    pc, pl = prepare(hc), prepare(hl)
    zero = jnp.zeros((hl.shape[0], RW_HEADS, RW_HEAD_DIM, RW_HEAD_DIM), f32)
    s_fc, y_fc = run(pc, 0, zero, False)
    s_bc, y_bc = run(pc, 1, zero, True)
    _, y_fl = run(pl, 0, s_fc, False)
    _, y_bl = run(pl, 1, s_bc, True)
    out_l = finish(pl, y_fl + y_bl, hl)
    out_c = finish(pc, y_fc + y_bc, hc) if need_ctx else None
    return out_c, out_l


def conv_glu(h, w_in, conv_w, conv_b, w_out):
    gate, up = jnp.split(h @ w_in, 2, axis=-1)
    gp = jnp.pad(gate, ((0, 0), (1, 1), (0, 0)))
    gate = gp[:, :-2] * conv_w[0] + gp[:, 1:-1] * conv_w[1] + gp[:, 2:] * conv_w[2] + conv_b
    return (jax.nn.silu(gate) * up) @ w_out


def setup_inputs(seed: int = 0) -> dict:
    key = jax.random.key(seed)
    ks = iter(jax.random.split(key, 48))
    f32 = jnp.float32

    def nrm(shape, scale=1.0):
        return jax.random.normal(next(ks), shape, f32) * scale

    def gain(shape):
        return 1.0 + nrm(shape, 0.05)

    D, R, F = D_MODEL, ADA_RANK, D_FF
    kv_w = GA_KV_HEADS * GA_HEAD_DIM
    return {
        'x': nrm((BATCH, SEQ, D)),
        'c': nrm((BATCH, D)),
        'ctx': nrm((BATCH, CTX_LEN, D)),
        'c_ctx': nrm((D,)),
        'ada_down': nrm((DEPTH, D, R), D ** -0.5),
        'ada_up': nrm((DEPTH, R, N_MOD * D), 0.5 * R ** -0.5),
        'ada_b': nrm((DEPTH, N_MOD * D), 0.01),
        'norm_g': gain((DEPTH, 2, D)),
        'ffn_in': nrm((DEPTH, D, 2 * F), D ** -0.5),
        'ffn_conv': nrm((DEPTH, CONV_W, F), CONV_W ** -0.5),
        'ffn_conv_b': nrm((DEPTH, F), 0.01),
        'ffn_out': nrm((DEPTH, F, D), F ** -0.5),
        'da_wqkv': nrm((N_DA_LAYERS, D, 3 * D), D ** -0.5),
        'da_wo': nrm((N_DA_LAYERS, D, D), D ** -0.5),
        'da_q_g': gain((N_DA_LAYERS, DA_HEAD_DIM)),
        'da_k_g': gain((N_DA_LAYERS, DA_HEAD_DIM)),
        'da_lambda': nrm((N_DA_LAYERS, 4, DA_HEAD_DIM), 0.1),
        'da_subln_g': gain((N_DA_LAYERS, 2 * DA_HEAD_DIM)),
        'rw_mix': jax.random.uniform(next(ks), (N_RW_LAYERS, 6, D), f32),
        'rw_wrkv': nrm((N_RW_LAYERS, 3, D, D), D ** -0.5),
        'rw_wo': nrm((N_RW_LAYERS, D, D), D ** -0.5),
        'rw_w0': jax.random.uniform(next(ks), (N_RW_LAYERS, 2, D), f32, -6.0, -1.0),
        'rw_w1': nrm((N_RW_LAYERS, 2, D, RW_DECAY_LORA), D ** -0.5),
        'rw_w2': nrm((N_RW_LAYERS, 2, RW_DECAY_LORA, D), 0.1 * RW_DECAY_LORA ** -0.5),
        'rw_a0': nrm((N_RW_LAYERS, 2, D), 0.1),
        'rw_a1': nrm((N_RW_LAYERS, 2, D, RW_ICLR_LORA), D ** -0.5),
        'rw_a2': nrm((N_RW_LAYERS, 2, RW_ICLR_LORA, D), RW_ICLR_LORA ** -0.5),
        'rw_g1': nrm((N_RW_LAYERS, D, RW_GATE_LORA), D ** -0.5),
        'rw_g2': nrm((N_RW_LAYERS, RW_GATE_LORA, D), RW_GATE_LORA ** -0.5),
        'rw_k_k': 0.85 + nrm((N_RW_LAYERS, D), 0.05),
        'rw_k_a': gain((N_RW_LAYERS, D)),
        'rw_r_k': nrm((N_RW_LAYERS, RW_HEADS, RW_HEAD_DIM), 0.1),
        'rw_ln_w': gain((N_RW_LAYERS, D)),
        'rw_ln_b': nrm((N_RW_LAYERS, D), 0.01),
        'ga_wqkv': nrm((N_GA_LAYERS, D, D + 2 * kv_w), D ** -0.5),
        'ga_wo': nrm((N_GA_LAYERS, D, D), D ** -0.5),
        'ga_q_g': gain((N_GA_LAYERS, GA_HEAD_DIM)),
        'ga_k_g': gain((N_GA_LAYERS, GA_HEAD_DIM)),
    }


def reference(x, c, ctx, c_ctx, ada_down, ada_up, ada_b, norm_g, ffn_in, ffn_conv, ffn_conv_b, ffn_out,
              da_wqkv, da_wo, da_q_g, da_k_g, da_lambda, da_subln_g,
              rw_mix, rw_wrkv, rw_wo, rw_w0, rw_w1, rw_w2, rw_a0, rw_a1, rw_a2, rw_g1, rw_g2,
              rw_k_k, rw_k_a, rw_r_k, rw_ln_w, rw_ln_b,
              ga_wqkv, ga_wo, ga_q_g, ga_k_g):
    rows = x.shape[1] // GRID_W
    cos_da, sin_da = axial_rope_tables(rows, DA_HEAD_DIM)
    cos_ga, sin_ga = axial_rope_tables(rows, GA_HEAD_DIM)
    xl, xc = x, ctx
    for i in range(DEPTH):
        kind, j = i % N_MIXERS, i // N_MIXERS
        need_ctx = i < DEPTH - 1
        ml = [m[:, None, :] for m in modulation(c, ada_down[i], ada_up[i], ada_b[i])]
        mc = modulation(c_ctx, ada_down[i], ada_up[i], ada_b[i])
        hl = modulate(rms_norm(xl, norm_g[i, 0]), ml[0], ml[1])
        hc = modulate(rms_norm(xc, norm_g[i, 0]), mc[0], mc[1])
        if kind == 0:
            oc, ol = diff_attention(hc, hl, cos_da, sin_da, da_wqkv[j], da_wo[j], da_q_g[j], da_k_g[j],
                                    da_lambda[j], da_subln_g[j], diff_lambda_init(i), need_ctx)
        elif kind == 1:
            oc, ol = rwkv7_time_mix(hc, hl, rw_mix[j], rw_wrkv[j], rw_wo[j], rw_w0[j], rw_w1[j], rw_w2[j],
                                    rw_a0[j], rw_a1[j], rw_a2[j], rw_g1[j], rw_g2[j], rw_k_k[j], rw_k_a[j],
                                    rw_r_k[j], rw_ln_w[j], rw_ln_b[j], need_ctx)
        else:
            oc, ol = gqa_attention(hc, hl, cos_ga, sin_ga, ga_wqkv[j], ga_wo[j], ga_q_g[j], ga_k_g[j], need_ctx)
        xl = xl + ml[2] * ol
        hl2 = modulate(rms_norm(xl, norm_g[i, 1]), ml[3], ml[4])
        xl = xl + ml[5] * conv_glu(hl2, ffn_in[i], ffn_conv[i], ffn_conv_b[i], ffn_out[i])
        if need_ctx:
            xc = xc + mc[2] * oc
            hc2 = modulate(rms_norm(xc, norm_g[i, 1]), mc[3], mc[4])
            xc = xc + mc[5] * conv_glu(hc2, ffn_in[i], ffn_conv[i], ffn_conv_b[i], ffn_out[i])
    return xl
```

```python
import functools
import math

import jax
import jax.numpy as jnp
from jax import lax
from jax.experimental import pallas as pl
from jax.experimental.pallas import tpu as pltpu

F32 = jnp.float32
BF16 = jnp.bfloat16

N_MOD = 6
N_MIXERS = 3
GRID_W = 64
ROPE_THETA = 10000.0
NORM_EPS = 1e-6
HEAD_DIM = 128
GA_GROUP = 4
RW_HEAD_DIM = 64
RW_LN_EPS = 64e-5

V7X_LANES = 128
V7X_SUBLANES = 8
V7X_VMEM_BYTES = 64 * 1024 * 1024
V7X_VMEM_RESERVE = 6 * 1024 * 1024
FFN_PAD = 1024


def _vmem_limit(block_bytes, temp_bytes=0):
    return int(min(2 * block_bytes + temp_bytes + V7X_VMEM_RESERVE, V7X_VMEM_BYTES - V7X_VMEM_RESERVE))


def _params(semantics, block_bytes, temp_bytes=0):
    return pltpu.CompilerParams(dimension_semantics=semantics,
                                vmem_limit_bytes=_vmem_limit(block_bytes, temp_bytes))


def _tile(n, prefs):
    for t in prefs:
        if n % t == 0:
            return t
    return n


def _nbytes(shape, dtype):
    return math.prod(shape) * jnp.dtype(dtype).itemsize


def _row_ids(i, rows, shape):
    return i * rows + lax.broadcasted_iota(jnp.int32, shape, 0)


def _ada_kernel(cond_ref, down_ref, up_ref, b_ref, o_ref):
    s = jax.nn.silu(cond_ref[...])
    t = jnp.dot(s, down_ref[...], preferred_element_type=F32, precision=lax.Precision.HIGHEST)
    m = jnp.dot(t, up_ref[...], preferred_element_type=F32, precision=lax.Precision.HIGHEST)
    o_ref[...] = m + b_ref[...]


def ada_modulation(cond, ada_down, ada_up, ada_b):
    depth, d, r = ada_down.shape
    n = ada_up.shape[-1]
    tn = _tile(n, (4096, 2048, 1024, 512))
    blocks = _nbytes((8, d), F32) + _nbytes((d, r), F32) + _nbytes((r, tn), F32) + 2 * _nbytes((8, tn), F32)
    return pl.pallas_call(
        _ada_kernel,
        out_shape=jax.ShapeDtypeStruct((depth, 8, n), F32),
        grid=(depth, n // tn),
        in_specs=[
            pl.BlockSpec((8, d), lambda l, j: (0, 0)),
            pl.BlockSpec((None, d, r), lambda l, j: (l, 0, 0)),
            pl.BlockSpec((None, r, tn), lambda l, j: (l, 0, j)),
            pl.BlockSpec((None, 1, tn), lambda l, j: (l, 0, j)),
        ],
        out_specs=pl.BlockSpec((None, 8, tn), lambda l, j: (l, 0, j)),
        compiler_params=_params(("arbitrary", "arbitrary"), blocks),
        name="ada_modulation",
    )(cond, ada_down, ada_up, ada_b.reshape(depth, 1, n))


def _norm_mod_rows(x, g, shift, scale, is_ctx):
    y = x * lax.rsqrt(jnp.mean(x * x, axis=-1, keepdims=True) + NORM_EPS) * g
    sc = jnp.where(is_ctx, scale[0:1, :], scale[1:2, :])
    sh = jnp.where(is_ctx, shift[0:1, :], shift[1:2, :])
    return y * (1.0 + sc) + sh


def _norm_mod_kernel(x_ref, g_ref, sh_ref, sc_ref, o_ref, *, rows, ctx):
    is_ctx = _row_ids(pl.program_id(0), rows, (rows, 1)) < ctx
    o_ref[...] = _norm_mod_rows(x_ref[...], g_ref[...], sh_ref[...], sc_ref[...], is_ctx).astype(o_ref.dtype)


def norm_mod(x, g, shift, scale, ctx):
    t, d = x.shape
    rows = _tile(t, (256, 128, 64, 32, 16, 8))
    blocks = _nbytes((rows, d), F32) + _nbytes((rows, d), BF16) + 5 * _nbytes((8, d), F32)
    return pl.pallas_call(
        functools.partial(_norm_mod_kernel, rows=rows, ctx=ctx),
        out_shape=jax.ShapeDtypeStruct((t, d), BF16),
        grid=(t // rows,),
        in_specs=[
            pl.BlockSpec((rows, d), lambda i: (i, 0)),
            pl.BlockSpec((1, d), lambda i: (0, 0)),
            pl.BlockSpec((2, d), lambda i: (0, 0)),
            pl.BlockSpec((2, d), lambda i: (0, 0)),
        ],
        out_specs=pl.BlockSpec((rows, d), lambda i: (i, 0)),
        compiler_params=_params(("arbitrary",), blocks, 4 * _nbytes((rows, d), F32)),
        name="norm_mod",
    )(x, g.reshape(1, d), shift, scale)


def _mm_accumulate(a_ref, b_ref, acc_ref, nk, finish):
    if nk == 1:
        finish(jnp.dot(a_ref[...], b_ref[...], preferred_element_type=F32))
        return
    k = pl.program_id(2)

    @pl.when(k == 0)
    def _():
        acc_ref[...] = jnp.zeros_like(acc_ref)

    acc_ref[...] += jnp.dot(a_ref[...], b_ref[...], preferred_element_type=F32)

    @pl.when(k == nk - 1)
    def _():
        finish(acc_ref[...])


def _mm_plain_kernel(a_ref, b_ref, o_ref, *scratch, nk, act):
    def finish(acc):
        if act == "tanh":
            acc = jnp.tanh(acc)
        elif act == "sigmoid":
            acc = jax.nn.sigmoid(acc)
        o_ref[...] = acc.astype(o_ref.dtype)

    _mm_accumulate(a_ref, b_ref, scratch[0] if scratch else None, nk, finish)


def _mm_residual_kernel(a_ref, b_ref, x_ref, g_ref, o_ref, *scratch, nk, tm, ctx):
    def finish(acc):
        is_ctx = _row_ids(pl.program_id(1), tm, (tm, 1)) < ctx
        gate = jnp.where(is_ctx, g_ref[0:1, :], g_ref[1:2, :])
        o_ref[...] = x_ref[...] + gate * acc

    _mm_accumulate(a_ref, b_ref, scratch[0] if scratch else None, nk, finish)


def _mm_headnorm_rope_kernel(a_ref, b_ref, g_ref, cc_ref, ss_ref, o_ref, *, tn):
    acc = jnp.dot(a_ref[...], b_ref[...], preferred_element_type=F32)
    cc = cc_ref[...]
    ss = ss_ref[...]
    for h in range(tn // HEAD_DIM):
        sl = slice(h * HEAD_DIM, (h + 1) * HEAD_DIM)
        z = acc[:, sl]
        y = z * lax.rsqrt(jnp.mean(z * z, axis=-1, keepdims=True) + NORM_EPS) * g_ref[:, sl]
        o_ref[:, sl] = (y * cc + pltpu.roll(y, HEAD_DIM // 2, 1) * ss).astype(o_ref.dtype)


def _mm_tiles(m, k, n):
    tm = _tile(m, (640, 512, 384, 256, 128, 64, 32, 16, 8))
    tn = _tile(n, (1024, 512, 256, 128))
    tk = k if k <= 4096 else _tile(k, (2816, 2048, 1024, 512))
    return tm, tn, tk


def _mm_call(kernel, a, b, extra_inputs, extra_specs, out_dtype, extra_bytes, name):
    m, k = a.shape
    n = b.shape[1]
    tm, tn, tk = _mm_tiles(m, k, n)
    nk = k // tk
    blocks = (_nbytes((tm, tk), a.dtype) + _nbytes((tk, tn), b.dtype) + _nbytes((tm, tn), out_dtype)
              + extra_bytes(tm, tn))
    scratch = [] if nk == 1 else [pltpu.VMEM((tm, tn), F32)]
    scratch_bytes = 0 if nk == 1 else _nbytes((tm, tn), F32)
    grid = (n // tn, m // tm) + (() if nk == 1 else (nk,))
    if nk == 1:
        a_map, b_map, o_map = (lambda j, i: (i, 0)), (lambda j, i: (0, j)), (lambda j, i: (i, j))
    else:
        a_map, b_map, o_map = (lambda j, i, kk: (i, kk)), (lambda j, i, kk: (kk, j)), (lambda j, i, kk: (i, j))
    return pl.pallas_call(
        functools.partial(kernel, nk=nk),
        out_shape=jax.ShapeDtypeStruct((m, n), out_dtype),
        grid=grid,
        in_specs=[pl.BlockSpec((tm, tk), a_map), pl.BlockSpec((tk, tn), b_map)] + extra_specs(tm, tn, nk),
        out_specs=pl.BlockSpec((tm, tn), o_map),
        scratch_shapes=scratch,
        compiler_params=pltpu.CompilerParams(
            dimension_semantics=("arbitrary",) * len(grid),
            vmem_limit_bytes=_vmem_limit(blocks, scratch_bytes + 2 * _nbytes((tm, tn), F32))),
        name=name,
    )(a, b, *extra_inputs)


def matmul(a, b, out_dtype, act=None, name="matmul"):
    return _mm_call(functools.partial(_mm_plain_kernel, act=act), a, b, (), lambda tm, tn, nk: [],
                    out_dtype, lambda tm, tn: 0, name)


def matmul_residual(a, b, x, gate, ctx, name="matmul_residual"):
    tm = _mm_tiles(a.shape[0], a.shape[1], b.shape[1])[0]

    def specs(tm_, tn, nk):
        if nk == 1:
            return [pl.BlockSpec((tm_, tn), lambda j, i: (i, j)), pl.BlockSpec((2, tn), lambda j, i: (0, j))]
        return [pl.BlockSpec((tm_, tn), lambda j, i, kk: (i, j)), pl.BlockSpec((2, tn), lambda j, i, kk: (0, j))]

    return _mm_call(functools.partial(_mm_residual_kernel, tm=tm, ctx=ctx), a, b, (x, gate), specs, F32,
                    lambda tm_, tn: _nbytes((tm_, tn), F32) + _nbytes((8, tn), F32), name)


def matmul_headnorm_rope(a, b, gain, cc, ss, name="matmul_headnorm_rope"):
    m, k = a.shape
    n = b.shape[1]
    tm, tn, tk = _mm_tiles(m, k, n)
    assert tk == k
    blocks = (_nbytes((tm, k), a.dtype) + _nbytes((k, tn), b.dtype) + _nbytes((tm, tn), BF16)
              + _nbytes((8, tn), F32) + 2 * _nbytes((tm, HEAD_DIM), F32) + _nbytes((tm, tn), F32))
    return pl.pallas_call(
        functools.partial(_mm_headnorm_rope_kernel, tn=tn),
        out_shape=jax.ShapeDtypeStruct((m, n), BF16),
        grid=(n // tn, m // tm),
        in_specs=[
            pl.BlockSpec((tm, k), lambda j, i: (i, 0)),
            pl.BlockSpec((k, tn), lambda j, i: (0, j)),
            pl.BlockSpec((1, tn), lambda j, i: (0, j)),
            pl.BlockSpec((tm, HEAD_DIM), lambda j, i: (i, 0)),
            pl.BlockSpec((tm, HEAD_DIM), lambda j, i: (i, 0)),
        ],
        out_specs=pl.BlockSpec((tm, tn), lambda j, i: (i, j)),
        compiler_params=_params(("arbitrary", "arbitrary"), blocks, 2 * _nbytes((tm, tn), F32)),
        name=name,
    )(a, b, gain.reshape(1, n), cc, ss)


def _flash_step(q_ref, kt_ref, v_ref, m_ref, l_ref, acc_ref, *, n_sub, per_sub_k, tk, ctx, masked):
    ki = pl.program_id(2)
    v = v_ref[...]
    for s in range(n_sub):
        q = q_ref[:, s * HEAD_DIM:(s + 1) * HEAD_DIM]
        kt = kt_ref[s * HEAD_DIM:(s + 1) * HEAD_DIM, :] if per_sub_k else kt_ref[...]
        sc = jnp.dot(q, kt, preferred_element_type=F32)
        if masked:
            col = ki * tk + lax.broadcasted_iota(jnp.int32, sc.shape, 1)
            sc = jnp.where(col < ctx, sc, -jnp.inf)
        m_prev = m_ref[s]
        m_new = jnp.maximum(m_prev, jnp.max(sc, axis=-1, keepdims=True))
        alpha = jnp.exp(m_prev - m_new)
        p = jnp.exp(sc - m_new)
        l_ref[s] = alpha * l_ref[s] + jnp.sum(p, axis=-1, keepdims=True)
        acc_ref[s] = alpha * acc_ref[s] + jnp.dot(p.astype(v.dtype), v, preferred_element_type=F32)
        m_ref[s] = m_new


def _flash_kernel(q_ref, kt_ref, v_ref, *rest, n_sub, per_sub_k, tq, tk, ctx, nkv, lam_init):
    if lam_init is None:
        o_ref, m_ref, l_ref, acc_ref = rest
    else:
        lam_ref, subln_ref, o_ref, m_ref, l_ref, acc_ref = rest
    qi = pl.program_id(1)
    ki = pl.program_id(2)
    q_is_ctx = (qi + 1) * tq <= ctx
    step = functools.partial(_flash_step, q_ref, kt_ref, v_ref, m_ref, l_ref, acc_ref,
                             n_sub=n_sub, per_sub_k=per_sub_k, tk=tk, ctx=ctx)

    @pl.when(ki == 0)
    def _():
        m_ref[...] = jnp.full(m_ref.shape, -jnp.inf, F32)
        l_ref[...] = jnp.zeros_like(l_ref)
        acc_ref[...] = jnp.zeros_like(acc_ref)

    @pl.when(jnp.logical_not(q_is_ctx))
    def _():
        step(masked=False)

    @pl.when(jnp.logical_and(q_is_ctx, ki * tk < ctx))
    def _():
        step(masked=True)

    @pl.when(ki == nkv - 1)
    def _():
        if lam_init is None:
            for s in range(n_sub):
                o_ref[:, s * HEAD_DIM:(s + 1) * HEAD_DIM] = (acc_ref[s] / l_ref[s]).astype(o_ref.dtype)
        else:
            lv = lam_ref[...]
            lam = (jnp.exp(jnp.sum(lv[0:1, :] * lv[1:2, :], axis=-1, keepdims=True))
                   - jnp.exp(jnp.sum(lv[2:3, :] * lv[3:4, :], axis=-1, keepdims=True)) + lam_init)
            o = acc_ref[0] / l_ref[0] - lam * (acc_ref[1] / l_ref[1])
            o = o * lax.rsqrt(jnp.mean(o * o, axis=-1, keepdims=True) + NORM_EPS) * subln_ref[...]
            o_ref[...] = (o * (1.0 - lam_init)).astype(o_ref.dtype)


def flash_attention(q, kt, v, ctx, *, n_sub, per_sub_k, lam=None, subln=None, lam_init=None, name="flash"):
    t = q.shape[0]
    qw = n_sub * HEAD_DIM
    g = q.shape[1] // qw
    kw = kt.shape[0] // g
    dv = v.shape[1] // g
    ow = dv if lam_init is not None else qw
    tq = _tile(ctx, (256, 128, 64, 32, 16, 8))
    tk = _tile(t, (1280, 1024, 768, 512, 256, 128))
    assert t % tq == 0 and ctx % tq == 0
    nkv = t // tk
    n_ctx_q = ctx // tq
    last_ctx_kv = (ctx - 1) // tk

    def kv_index(qi, ki):
        return jnp.where(qi < n_ctx_q, jnp.minimum(ki, last_ctx_kv), ki)

    in_specs = [
        pl.BlockSpec((tq, qw), lambda h, qi, ki: (qi, h)),
        pl.BlockSpec((kw, tk), lambda h, qi, ki: (h, kv_index(qi, ki))),
        pl.BlockSpec((tk, dv), lambda h, qi, ki: (kv_index(qi, ki), h)),
    ]
    inputs = [q, kt, v]
    if lam_init is not None:
        in_specs += [pl.BlockSpec(lam.shape, lambda h, qi, ki: (0, 0)),
                     pl.BlockSpec((1, dv), lambda h, qi, ki: (0, 0))]
        inputs += [lam, subln.reshape(1, dv)]
    acc_w = dv
    blocks = (_nbytes((tq, qw), BF16) + _nbytes((kw, tk), BF16) + _nbytes((tk, dv), BF16) + _nbytes((tq, ow), BF16)
              + 2 * _nbytes((tq, tk), F32))
    scratch_bytes = n_sub * (2 * _nbytes((tq, V7X_LANES), F32) + _nbytes((tq, acc_w), F32))
    return pl.pallas_call(
        functools.partial(_flash_kernel, n_sub=n_sub, per_sub_k=per_sub_k, tq=tq, tk=tk, ctx=ctx, nkv=nkv,
                          lam_init=lam_init),
        out_shape=jax.ShapeDtypeStruct((t, g * ow), BF16),
        grid=(g, t // tq, nkv),
        in_specs=in_specs,
        out_specs=pl.BlockSpec((tq, ow), lambda h, qi, ki: (qi, h)),
        scratch_shapes=[pltpu.VMEM((n_sub, tq, 1), F32), pltpu.VMEM((n_sub, tq, 1), F32),
                        pltpu.VMEM((n_sub, tq, acc_w), F32)],
        compiler_params=pltpu.CompilerParams(
            dimension_semantics=("arbitrary", "arbitrary", "arbitrary"),
            vmem_limit_bytes=_vmem_limit(blocks, scratch_bytes + 4 * _nbytes((tq, tk), F32))),
        name=name,
    )(*inputs)


def _seq_neighbors(buf_ref, cur, prev8, next8, rows, row0, ctx, total):
    buf_ref[0:V7X_SUBLANES, :] = prev8
    buf_ref[V7X_SUBLANES:V7X_SUBLANES + rows, :] = cur
    buf_ref[V7X_SUBLANES + rows:, :] = next8
    t = row0 + lax.broadcasted_iota(jnp.int32, (rows, 1), 0)
    has_prev = jnp.logical_and(t != 0, t != ctx)
    has_next = jnp.logical_and(t != ctx - 1, t != total - 1)
    before = jnp.where(has_prev, buf_ref[V7X_SUBLANES - 1:V7X_SUBLANES - 1 + rows, :], 0.0)
    after = jnp.where(has_next, buf_ref[V7X_SUBLANES + 1:V7X_SUBLANES + 1 + rows, :], 0.0)
    return before, after


def _glu_kernel(g_ref, u_ref, gp_ref, gn_ref, cw_ref, cb_ref, o_ref, buf_ref, *, rows, ctx, total):
    g = g_ref[...]
    before, after = _seq_neighbors(buf_ref, g, gp_ref[...], gn_ref[...], rows, pl.program_id(0) * rows, ctx, total)
    conv = before * cw_ref[0:1, :] + g * cw_ref[1:2, :] + after * cw_ref[2:3, :] + cb_ref[...]
    o_ref[...] = (jax.nn.silu(conv) * u_ref[...]).astype(o_ref.dtype)


def conv_glu_act(gu, conv_w, conv_b, ctx):
    t, f2 = gu.shape
    fp = f2 // 2
    rows = _tile(t, (256, 128, 64, 32, 16, 8))
    tf = _tile(fp, (1024, 512, 256, 128))
    nf = fp // tf
    rb = rows // V7X_SUBLANES
    last8 = t // V7X_SUBLANES - 1
    blocks = (2 * _nbytes((rows, tf), F32) + 2 * _nbytes((8, tf), F32) + 2 * _nbytes((8, tf), F32)
              + _nbytes((rows, tf), BF16))
    return pl.pallas_call(
        functools.partial(_glu_kernel, rows=rows, ctx=ctx, total=t),
        out_shape=jax.ShapeDtypeStruct((t, fp), BF16),
        grid=(t // rows, nf),
        in_specs=[
            pl.BlockSpec((rows, tf), lambda i, j: (i, j)),
            pl.BlockSpec((rows, tf), lambda i, j: (i, j + nf)),
            pl.BlockSpec((V7X_SUBLANES, tf), lambda i, j: (jnp.maximum(i * rb - 1, 0), j)),
            pl.BlockSpec((V7X_SUBLANES, tf), lambda i, j: (jnp.minimum((i + 1) * rb, last8), j)),
            pl.BlockSpec((3, tf), lambda i, j: (0, j)),
            pl.BlockSpec((1, tf), lambda i, j: (0, j)),
        ],
        out_specs=pl.BlockSpec((rows, tf), lambda i, j: (i, j)),
        scratch_shapes=[pltpu.VMEM((rows + 2 * V7X_SUBLANES, tf), F32)],
        compiler_params=pltpu.CompilerParams(
            dimension_semantics=("arbitrary", "arbitrary"),
            vmem_limit_bytes=_vmem_limit(blocks, 6 * _nbytes((rows + 16, tf), F32))),
        name="conv_glu_act",
    )(gu, gu, gu, gu, conv_w, conv_b.reshape(1, fp))


def _rw_mix_kernel(x_ref, xp_ref, xn_ref, g_ref, sh_ref, sc_ref, mix_ref, *rest, rows, ctx, total):
    outs, buf_ref = rest[:6], rest[6]
    row0 = pl.program_id(0) * rows
    g, sh, sc = g_ref[...], sh_ref[...], sc_ref[...]

    def nm(x, first_row):
        t = first_row + lax.broadcasted_iota(jnp.int32, (x.shape[0], 1), 0)
        return _norm_mod_rows(x, g, sh, sc, t < ctx)

    h = nm(x_ref[...], row0)
    hp = nm(xp_ref[...], row0 - V7X_SUBLANES)
    hn = nm(xn_ref[...], row0 + rows)
    before, after = _seq_neighbors(buf_ref, h, hp, hn, rows, row0, ctx, total)
    xx = 0.5 * (before + after) - h
    for n in range(6):
        outs[n][...] = (h + xx * mix_ref[n:n + 1, :]).astype(outs[n].dtype)


def rw_token_mix(x, g, shift, scale, mix, ctx):
    t, d = x.shape
    rows = _tile(t, (128, 64, 32, 16, 8))
    rb = rows // V7X_SUBLANES
    last8 = t // V7X_SUBLANES - 1
    blocks = _nbytes((rows + 16, d), F32) + 6 * _nbytes((rows, d), BF16) + 4 * _nbytes((8, d), F32)
    return pl.pallas_call(
        functools.partial(_rw_mix_kernel, rows=rows, ctx=ctx, total=t),
        out_shape=[jax.ShapeDtypeStruct((t, d), BF16)] * 6,
        grid=(t // rows,),
        in_specs=[
            pl.BlockSpec((rows, d), lambda i: (i, 0)),
            pl.BlockSpec((V7X_SUBLANES, d), lambda i: (jnp.maximum(i * rb - 1, 0), 0)),
            pl.BlockSpec((V7X_SUBLANES, d), lambda i: (jnp.minimum((i + 1) * rb, last8), 0)),
            pl.BlockSpec((1, d), lambda i: (0, 0)),
            pl.BlockSpec((2, d), lambda i: (0, 0)),
            pl.BlockSpec((2, d), lambda i: (0, 0)),
            pl.BlockSpec((6, d), lambda i: (0, 0)),
        ],
        out_specs=[pl.BlockSpec((rows, d), lambda i: (i, 0))] * 6,
        scratch_shapes=[pltpu.VMEM((rows + 2 * V7X_SUBLANES, d), F32)],
        compiler_params=pltpu.CompilerParams(
            dimension_semantics=("arbitrary",),
            vmem_limit_bytes=_vmem_limit(blocks, 6 * _nbytes((rows + 16, d), F32))),
        name="rw_token_mix",
    )(x, x, x, g.reshape(1, d), shift, scale, mix)


def _head_sum(x, ones_blk):
    cols = []
    for c in range(x.shape[1] // V7X_LANES):
        cols.append(jnp.dot(x[:, c * V7X_LANES:(c + 1) * V7X_LANES], ones_blk, preferred_element_type=F32,
                            precision=lax.Precision.HIGHEST))
    return jnp.concatenate(cols, axis=1)


def _rw_prep_kernel(r_ref, k_ref, v_ref, wl0_ref, wl1_ref, al0_ref, al1_ref, w0_ref, a0_ref, kk_ref, ka_ref,
                    rk_ref, ones_ref, w_o, a_o, b_o, kd_o, wr_o, br_o, kr_o, bonus_o):
    r, k, v = r_ref[...], k_ref[...], v_ref[...]
    ones_blk = ones_ref[...]
    kk = k * kk_ref[...]
    kk = kk * lax.rsqrt(jnp.maximum(_head_sum(kk * kk, ones_blk), 1e-24))
    a_o[...] = -kk
    kd_sum = jnp.zeros_like(k)
    for n, (wl_ref, al_ref) in enumerate(((wl0_ref, al0_ref), (wl1_ref, al1_ref))):
        wl = -jax.nn.softplus(-(w0_ref[n:n + 1, :] + wl_ref[...])) - 0.5
        w = jnp.exp(-jnp.exp(wl))
        iclr = jax.nn.sigmoid(a0_ref[n:n + 1, :] + al_ref[...])
        kd = k * (1.0 + (iclr - 1.0) * ka_ref[...])
        b = kk * iclr
        w_o[n] = w
        b_o[n] = b
        kd_o[n] = kd
        wr_o[n] = w * r
        br_o[n] = _head_sum(b * r, ones_blk)
        kr_o[n] = _head_sum(kd * r, ones_blk)
        kd_sum = kd_sum + kd
    bonus_o[...] = _head_sum(r * kd_sum * rk_ref[...], ones_blk) * v


def rw_prepare(r, k, v, wl, al, w0, a0, k_k, k_a, r_k):
    t, d = r.shape
    rows = _tile(t, (32, 16, 8))
    lane = jnp.arange(V7X_LANES) // RW_HEAD_DIM
    ones_blk = (lane[:, None] == lane[None, :]).astype(F32)
    row = pl.BlockSpec((rows, d), lambda i: (i, 0))
    dir_row = pl.BlockSpec((2, rows, d), lambda i: (0, i, 0))
    vec = pl.BlockSpec((1, d), lambda i: (0, 0))
    vec2 = pl.BlockSpec((2, d), lambda i: (0, 0))
    blocks = 22 * _nbytes((rows, d), F32) + 8 * _nbytes((8, d), F32)
    sds = jax.ShapeDtypeStruct((t, d), F32)
    sds2 = jax.ShapeDtypeStruct((2, t, d), F32)
    return pl.pallas_call(
        _rw_prep_kernel,
        out_shape=[sds2, sds, sds2, sds2, sds2, sds2, sds2, sds],
        grid=(t // rows,),
        in_specs=[row, row, row,
                  pl.BlockSpec((None, rows, d), lambda i: (0, i, 0)), pl.BlockSpec((None, rows, d), lambda i: (1, i, 0)),
                  pl.BlockSpec((None, rows, d), lambda i: (0, i, 0)), pl.BlockSpec((None, rows, d), lambda i: (1, i, 0)),
                  vec2, vec2, vec, vec, vec,
                  pl.BlockSpec((V7X_LANES, V7X_LANES), lambda i: (0, 0))],
        out_specs=[dir_row, row, dir_row, dir_row, dir_row, dir_row, dir_row, row],
        compiler_params=_params(("arbitrary",), blocks, 12 * _nbytes((rows, d), F32)),
        name="rw_prepare",
    )(r, k, v, wl, wl, al, al, w0, a0, k_k.reshape(1, d), k_a.reshape(1, d), r_k.reshape(1, d), ones_blk)


def _wkv_kernel(w_ref, a_ref, b_ref, kd_ref, wr_ref, v_ref, br_ref, kr_ref, y_ref, s_ref, *, steps, chans):
    @pl.when(pl.program_id(0) == 0)
    def _():
        s_ref[...] = jnp.zeros_like(s_ref)

    def step(i, carry):
        sa = jnp.zeros(s_ref.shape[1:], F32)
        u = jnp.zeros(s_ref.shape[1:], F32)
        for c in range(chans):
            sc = s_ref[c]
            sa = sa + sc * a_ref[i, c:c + 1, :]
            u = u + sc * wr_ref[i, c:c + 1, :]
        vv = v_ref[i]
        for c in range(chans):
            s_ref[c] = s_ref[c] * w_ref[i, c:c + 1, :] + sa * b_ref[i, c:c + 1, :] + vv * kd_ref[i, c:c + 1, :]
        y_ref[i] = u + sa * br_ref[i] + vv * kr_ref[i]
        return carry

    lax.fori_loop(0, steps, step, 0)


def wkv_scan(w, a, b, kd, wr, v, br, kr):
    t, chans, lanes = w.shape
    steps = _tile(t, (16, 8, 4, 2))
    big = pl.BlockSpec((steps, chans, lanes), lambda i: (i, 0, 0))
    small = pl.BlockSpec((steps, 1, lanes), lambda i: (i, 0, 0))
    blocks = 7 * _nbytes((steps, chans, lanes), F32) + 2 * _nbytes((steps, 8, lanes), F32)
    state_bytes = _nbytes((chans, chans, lanes), F32)
    return pl.pallas_call(
        functools.partial(_wkv_kernel, steps=steps, chans=chans),
        out_shape=jax.ShapeDtypeStruct((t, chans, lanes), F32),
        grid=(t // steps,),
        in_specs=[big, big, big, big, big, big, small, small],
        out_specs=big,
        scratch_shapes=[pltpu.VMEM((chans, chans, lanes), F32)],
        compiler_params=pltpu.CompilerParams(
            dimension_semantics=("arbitrary",), vmem_limit_bytes=_vmem_limit(blocks, state_bytes)),
        name="wkv_scan",
    )(w, a, b, kd, wr, v, br, kr)


def _rw_finish_kernel(yf_ref, yb_ref, bonus_ref, gate_ref, lnw_ref, lnb_ref, ones_ref, o_ref):
    y = yf_ref[...] + yb_ref[...]
    ones_blk = ones_ref[...]
    mu = _head_sum(y, ones_blk) * (1.0 / RW_HEAD_DIM)
    yc = y - mu
    var = _head_sum(yc * yc, ones_blk) * (1.0 / RW_HEAD_DIM)
    z = yc * lax.rsqrt(var + RW_LN_EPS) * lnw_ref[...] + lnb_ref[...] + bonus_ref[...]
    o_ref[...] = (z * gate_ref[...]).astype(o_ref.dtype)


def rw_finish(yf, yb, bonus, gate, ln_w, ln_b):
    t, d = yf.shape
    rows = _tile(t, (128, 64, 32, 16, 8))
    lane = jnp.arange(V7X_LANES) // RW_HEAD_DIM
    ones_blk = (lane[:, None] == lane[None, :]).astype(F32)
    row = pl.BlockSpec((rows, d), lambda i: (i, 0))
    vec = pl.BlockSpec((1, d), lambda i: (0, 0))
    blocks = 4 * _nbytes((rows, d), F32) + _nbytes((rows, d), BF16) + 3 * _nbytes((8, d), F32)
    return pl.pallas_call(
        _rw_finish_kernel,
        out_shape=jax.ShapeDtypeStruct((t, d), BF16),
        grid=(t // rows,),
        in_specs=[row, row, row, row, vec, vec, pl.BlockSpec((V7X_LANES, V7X_LANES), lambda i: (0, 0))],
        out_specs=row,
        compiler_params=_params(("arbitrary",), blocks, 6 * _nbytes((rows, d), F32)),
        name="rw_finish",
    )(yf, yb, bonus, gate, ln_w.reshape(1, d), ln_b.reshape(1, d), ones_blk)


def _rope_tables(ctx, seq):
    pos = jnp.arange(seq)
    row = (pos // GRID_W).astype(F32)
    col = (pos % GRID_W).astype(F32)
    n_freq = HEAD_DIM // 4
    inv_freq = ROPE_THETA ** (-jnp.arange(n_freq, dtype=F32) / n_freq)
    ang = jnp.concatenate([row[:, None] * inv_freq, col[:, None] * inv_freq], axis=-1)
    cos = jnp.concatenate([jnp.ones((ctx, HEAD_DIM // 2), F32), jnp.cos(ang)], axis=0)
    sin = jnp.concatenate([jnp.zeros((ctx, HEAD_DIM // 2), F32), jnp.sin(ang)], axis=0)
    return jnp.concatenate([cos, cos], axis=-1), jnp.concatenate([-sin, sin], axis=-1)


def _seg_reverse(x, ctx):
    return jnp.concatenate([x[:ctx][::-1], x[ctx:][::-1]], axis=0)


def _to_scan(xf, xb, ctx):
    t, d = xf.shape
    h = d // RW_HEAD_DIM
    f = xf.reshape(t, h, RW_HEAD_DIM).transpose(0, 2, 1)
    b = _seg_reverse(xb, ctx).reshape(t, h, RW_HEAD_DIM).transpose(0, 2, 1)
    return jnp.concatenate([f, b], axis=-1)


def _head_scalar_to_scan(xf, xb, ctx):
    f = xf[:, ::RW_HEAD_DIM]
    b = _seg_reverse(xb, ctx)[:, ::RW_HEAD_DIM]
    return jnp.concatenate([f, b], axis=-1)[:, None, :]


def _from_scan(y, ctx):
    t, c, l = y.shape
    h = l // 2
    f = y[:, :, :h].transpose(0, 2, 1).reshape(t, h * c)
    b = _seg_reverse(y[:, :, h:], ctx).transpose(0, 2, 1).reshape(t, h * c)
    return f, b


def _diff_lambda_init(layer_idx):
    return 0.8 - 0.6 * math.exp(-0.3 * layer_idx)


def _diff_attention(h, cc, ss, wqkv, wo, q_g, k_g, lam_vecs, subln_g, lam_init, x, gate, ctx):
    d = h.shape[1]
    n_heads = d // HEAD_DIM
    scale = HEAD_DIM ** -0.5
    gain = jnp.concatenate([jnp.tile(q_g * scale, n_heads), jnp.tile(k_g, n_heads)])
    qk = matmul_headnorm_rope(h, wqkv[:, :2 * d].astype(BF16), gain, cc, ss, name="da_qk_proj")
    v = matmul(h, wqkv[:, 2 * d:].astype(BF16), BF16, name="da_v_proj")
    o = flash_attention(qk[:, :d], qk[:, d:].T, v, ctx, n_sub=2, per_sub_k=True, lam=lam_vecs, subln=subln_g,
                        lam_init=lam_init, name="da_flash")
    return matmul_residual(o, wo.astype(BF16), x, gate, ctx, name="da_out_proj")


def _gqa_attention(h, cc, ss, wqkv, wo, q_g, k_g, x, gate, ctx):
    d = h.shape[1]
    kv_w = d // GA_GROUP
    scale = HEAD_DIM ** -0.5
    gain = jnp.concatenate([jnp.tile(q_g * scale, d // HEAD_DIM), jnp.tile(k_g, kv_w // HEAD_DIM)])
    qk = matmul_headnorm_rope(h, wqkv[:, :d + kv_w].astype(BF16), gain, cc, ss, name="ga_qk_proj")
    v = matmul(h, wqkv[:, d + kv_w:].astype(BF16), BF16, name="ga_v_proj")
    o = flash_attention(qk[:, :d], qk[:, d:].T, v, ctx, n_sub=GA_GROUP, per_sub_k=False, name="ga_flash")
    return matmul_residual(o, wo.astype(BF16), x, gate, ctx, name="ga_out_proj")


def _pad_cols(w, n):
    return jnp.pad(w, ((0, 0), (0, n - w.shape[1])))


def _rwkv7(x, g, shift, scale, mix, wrkv, wo, w0, w1, w2, a0, a1, a2, g1, g2, k_k, k_a, r_k, ln_w, ln_b, gate,
           ctx):
    t, d = x.shape
    xr, xw, xk, xv, xa, xg = rw_token_mix(x, g, shift, scale, mix, ctx)
    r = matmul(xr, wrkv[0].astype(BF16), F32, name="rw_r_proj")
    k = matmul(xk, wrkv[1].astype(BF16), F32, name="rw_k_proj")
    v = matmul(xv, wrkv[2].astype(BF16), F32, name="rw_v_proj")
    lw = w1.shape[-1]
    la = a1.shape[-1]
    hw = matmul(xw, jnp.concatenate([w1[0], w1[1]], axis=1).astype(BF16), BF16, act="tanh", name="rw_w_lora_in")
    ha = matmul(xa, jnp.concatenate([a1[0], a1[1]], axis=1).astype(BF16), BF16, name="rw_a_lora_in")
    lg = -(-g1.shape[1] // V7X_LANES) * V7X_LANES
    hg = matmul(xg, _pad_cols(g1, lg).astype(BF16), BF16, act="sigmoid", name="rw_g_lora_in")
    wl = jnp.stack([matmul(hw[:, n * lw:(n + 1) * lw], w2[n].astype(BF16), F32, name="rw_w_lora_out")
                    for n in range(2)])
    al = jnp.stack([matmul(ha[:, n * la:(n + 1) * la], a2[n].astype(BF16), F32, name="rw_a_lora_out")
                    for n in range(2)])
    g2p = jnp.pad(g2, ((0, lg - g2.shape[0]), (0, 0))).astype(BF16)
    out_gate = matmul(hg, g2p, F32, name="rw_g_lora_out")
    w, a, b, kd, wr, br, kr, bonus = rw_prepare(r, k, v, wl, al, w0, a0, k_k, k_a, r_k.reshape(d))
    y = wkv_scan(_to_scan(w[0], w[1], ctx), _to_scan(a, a, ctx), _to_scan(b[0], b[1], ctx),
                 _to_scan(kd[0], kd[1], ctx), _to_scan(wr[0], wr[1], ctx), _to_scan(v, v, ctx),
                 _head_scalar_to_scan(br[0], br[1], ctx), _head_scalar_to_scan(kr[0], kr[1], ctx))
    yf, yb = _from_scan(y, ctx)
    z = rw_finish(yf, yb, bonus, out_gate, ln_w, ln_b)
    return matmul_residual(z, wo.astype(BF16), x, gate, ctx, name="rw_out_proj")


def _conv_glu(h, w_in, conv_w, conv_b, w_out, x, gate, ctx):
    f = conv_w.shape[1]
    fp = -(-f // FFN_PAD) * FFN_PAD
    w_in_p = jnp.concatenate([_pad_cols(w_in[:, :f], fp), _pad_cols(w_in[:, f:], fp)], axis=1).astype(BF16)
    gu = matmul(h, w_in_p, F32, name="ffn_in_proj")
    act = conv_glu_act(gu, _pad_cols(conv_w, fp), jnp.pad(conv_b, (0, fp - f)), ctx)
    w_out_p = jnp.pad(w_out, ((0, fp - f), (0, 0))).astype(BF16)
    return matmul_residual(act, w_out_p, x, gate, ctx, name="ffn_out_proj")


def kernel(x, c, ctx, c_ctx, ada_down, ada_up, ada_b, norm_g, ffn_in, ffn_conv, ffn_conv_b, ffn_out, da_wqkv, da_wo, da_q_g, da_k_g, da_lambda, da_subln_g, rw_mix, rw_wrkv, rw_wo, rw_w0, rw_w1, rw_w2, rw_a0, rw_a1, rw_a2, rw_g1, rw_g2, rw_k_k, rw_k_a, rw_r_k, rw_ln_w, rw_ln_b, ga_wqkv, ga_wo, ga_q_g, ga_k_g):
    assert x.shape[0] == 1
    seq, d = x.shape[1], x.shape[2]
    n_ctx = ctx.shape[1]
    depth = ada_down.shape[0]
    stream = jnp.concatenate([ctx[0], x[0]], axis=0)
    cond = jnp.zeros((V7X_SUBLANES, d), F32).at[0].set(c_ctx).at[1].set(c[0])
    mod = ada_modulation(cond, ada_down, ada_up, ada_b)[:, :2].reshape(depth, 2, N_MOD, d)
    cc, ss = _rope_tables(n_ctx, seq)
    for i in range(depth):
        kind, j = i % N_MIXERS, i // N_MIXERS
        m = [mod[i, :, n, :] for n in range(N_MOD)]
        if kind == 1:
            stream = _rwkv7(stream, norm_g[i, 0], m[0], m[1], rw_mix[j], rw_wrkv[j], rw_wo[j], rw_w0[j], rw_w1[j],
                            rw_w2[j], rw_a0[j], rw_a1[j], rw_a2[j], rw_g1[j], rw_g2[j], rw_k_k[j], rw_k_a[j],
                            rw_r_k[j], rw_ln_w[j], rw_ln_b[j], m[2], n_ctx)
        else:
            h = norm_mod(stream, norm_g[i, 0], m[0], m[1], n_ctx)
            if kind == 0:
                stream = _diff_attention(h, cc, ss, da_wqkv[j], da_wo[j], da_q_g[j], da_k_g[j], da_lambda[j],
                                         da_subln_g[j], _diff_lambda_init(i), stream, m[2], n_ctx)
            else:
                stream = _gqa_attention(h, cc, ss, ga_wqkv[j], ga_wo[j], ga_q_g[j], ga_k_g[j], stream, m[2], n_ctx)
        h2 = norm_mod(stream, norm_g[i, 1], m[3], m[4], n_ctx)
        stream = _conv_glu(h2, ffn_in[i], ffn_conv[i], ffn_conv_b[i], ffn_out[i], stream, m[5], n_ctx)
    return stream[n_ctx:][None]
```

```python
import functools
import math

import jax
import jax.numpy as jnp
from jax import lax
from jax.experimental import pallas as pl
from jax.experimental.pallas import tpu as pltpu

F32 = jnp.float32
BF16 = jnp.bfloat16

N_MOD = 6
N_MIXERS = 3
GRID_W = 64
ROPE_THETA = 10000.0
NORM_EPS = 1e-6
HEAD_DIM = 128
GA_GROUP = 4
RW_HEAD_DIM = 64
RW_LN_EPS = 64e-5

V7X_LANES = 128
V7X_SUBLANES = 8
V7X_VMEM_BYTES = 64 * 1024 * 1024
V7X_VMEM_RESERVE = 6 * 1024 * 1024
FFN_PAD = 1024


def _vmem_limit(block_bytes, temp_bytes=0):
    return int(min(2 * block_bytes + temp_bytes + V7X_VMEM_RESERVE, V7X_VMEM_BYTES - V7X_VMEM_RESERVE))


def _params(semantics, block_bytes, temp_bytes=0):
    return pltpu.CompilerParams(dimension_semantics=semantics,
                                vmem_limit_bytes=_vmem_limit(block_bytes, temp_bytes))


def _tile(n, prefs):
    for t in prefs:
        if n % t == 0:
            return t
    return n


def _nbytes(shape, dtype):
    return math.prod(shape) * jnp.dtype(dtype).itemsize


def _row_ids(i, rows, shape):
    return i * rows + lax.broadcasted_iota(jnp.int32, shape, 0)


def _ada_kernel(cond_ref, down_ref, up_ref, b_ref, o_ref):
    s = jax.nn.silu(cond_ref[...])
    t = jnp.dot(s, down_ref[...], preferred_element_type=F32, precision=lax.Precision.HIGHEST)
    m = jnp.dot(t, up_ref[...], preferred_element_type=F32, precision=lax.Precision.HIGHEST)
    o_ref[...] = m + b_ref[...]


def ada_modulation(cond, ada_down, ada_up, ada_b):
    depth, d, r = ada_down.shape
    n = ada_up.shape[-1]
    tn = _tile(n, (4096, 2048, 1024, 512))
    blocks = _nbytes((8, d), F32) + _nbytes((d, r), F32) + _nbytes((r, tn), F32) + 2 * _nbytes((8, tn), F32)
    return pl.pallas_call(
        _ada_kernel,
        out_shape=jax.ShapeDtypeStruct((depth, 8, n), F32),
        grid=(depth, n // tn),
        in_specs=[
            pl.BlockSpec((8, d), lambda l, j: (0, 0)),
            pl.BlockSpec((None, d, r), lambda l, j: (l, 0, 0)),
            pl.BlockSpec((None, r, tn), lambda l, j: (l, 0, j)),
            pl.BlockSpec((None, 1, tn), lambda l, j: (l, 0, j)),
        ],
        out_specs=pl.BlockSpec((None, 8, tn), lambda l, j: (l, 0, j)),
        compiler_params=_params(("arbitrary", "arbitrary"), blocks),
        name="ada_modulation",
    )(cond, ada_down, ada_up, ada_b.reshape(depth, 1, n))


def _norm_mod_rows(x, g, shift, scale, is_ctx):
    y = x * lax.rsqrt(jnp.mean(x * x, axis=-1, keepdims=True) + NORM_EPS) * g
    sc = jnp.where(is_ctx, scale[0:1, :], scale[1:2, :])
    sh = jnp.where(is_ctx, shift[0:1, :], shift[1:2, :])
    return y * (1.0 + sc) + sh


def _norm_mod_kernel(x_ref, g_ref, sh_ref, sc_ref, o_ref, *, rows, ctx):
    is_ctx = _row_ids(pl.program_id(0), rows, (rows, 1)) < ctx
    o_ref[...] = _norm_mod_rows(x_ref[...], g_ref[...], sh_ref[...], sc_ref[...], is_ctx).astype(o_ref.dtype)


def norm_mod(x, g, shift, scale, ctx):
    t, d = x.shape
    rows = _tile(t, (256, 128, 64, 32, 16, 8))
    blocks = _nbytes((rows, d), F32) + _nbytes((rows, d), BF16) + 5 * _nbytes((8, d), F32)
    return pl.pallas_call(
        functools.partial(_norm_mod_kernel, rows=rows, ctx=ctx),
        out_shape=jax.ShapeDtypeStruct((t, d), BF16),
        grid=(t // rows,),
        in_specs=[
            pl.BlockSpec((rows, d), lambda i: (i, 0)),
            pl.BlockSpec((1, d), lambda i: (0, 0)),
            pl.BlockSpec((2, d), lambda i: (0, 0)),
            pl.BlockSpec((2, d), lambda i: (0, 0)),
        ],
        out_specs=pl.BlockSpec((rows, d), lambda i: (i, 0)),
        compiler_params=_params(("arbitrary",), blocks, 4 * _nbytes((rows, d), F32)),
        name="norm_mod",
    )(x, g.reshape(1, d), shift, scale)


def _mm_accumulate(a_ref, b_ref, acc_ref, nk, finish):
    if nk == 1:
        finish(jnp.dot(a_ref[...], b_ref[...], preferred_element_type=F32))
        return
    k = pl.program_id(2)

    @pl.when(k == 0)
    def _():
        acc_ref[...] = jnp.zeros_like(acc_ref)

    acc_ref[...] += jnp.dot(a_ref[...], b_ref[...], preferred_element_type=F32)

    @pl.when(k == nk - 1)
    def _():
        finish(acc_ref[...])


def _mm_plain_kernel(a_ref, b_ref, o_ref, *scratch, nk, act):
    def finish(acc):
        if act == "tanh":
            acc = jnp.tanh(acc)
        elif act == "sigmoid":
            acc = jax.nn.sigmoid(acc)
        o_ref[...] = acc.astype(o_ref.dtype)

    _mm_accumulate(a_ref, b_ref, scratch[0] if scratch else None, nk, finish)


def _mm_residual_kernel(a_ref, b_ref, x_ref, g_ref, o_ref, *scratch, nk, tm, ctx):
    def finish(acc):
        is_ctx = _row_ids(pl.program_id(1), tm, (tm, 1)) < ctx
        gate = jnp.where(is_ctx, g_ref[0:1, :], g_ref[1:2, :])
        o_ref[...] = x_ref[...] + gate * acc

    _mm_accumulate(a_ref, b_ref, scratch[0] if scratch else None, nk, finish)


def _mm_headnorm_rope_kernel(a_ref, b_ref, g_ref, cc_ref, ss_ref, o_ref, *, tn):
    acc = jnp.dot(a_ref[...], b_ref[...], preferred_element_type=F32)
    cc = cc_ref[...]
    ss = ss_ref[...]
    for h in range(tn // HEAD_DIM):
        sl = slice(h * HEAD_DIM, (h + 1) * HEAD_DIM)
        z = acc[:, sl]
        y = z * lax.rsqrt(jnp.mean(z * z, axis=-1, keepdims=True) + NORM_EPS) * g_ref[:, sl]
        o_ref[:, sl] = (y * cc + pltpu.roll(y, HEAD_DIM // 2, 1) * ss).astype(o_ref.dtype)


def _mm_tiles(m, k, n):
    tm = _tile(m, (640, 512, 384, 256, 128, 64, 32, 16, 8))
    tn = _tile(n, (1024, 512, 256, 128))
    tk = k if k <= 4096 else _tile(k, (2816, 2048, 1024, 512))
    return tm, tn, tk


def _mm_call(kernel, a, b, extra_inputs, extra_specs, out_dtype, extra_bytes, name):
    m, k = a.shape
    n = b.shape[1]
    tm, tn, tk = _mm_tiles(m, k, n)
    nk = k // tk
    blocks = (_nbytes((tm, tk), a.dtype) + _nbytes((tk, tn), b.dtype) + _nbytes((tm, tn), out_dtype)
              + extra_bytes(tm, tn))
    scratch = [] if nk == 1 else [pltpu.VMEM((tm, tn), F32)]
    scratch_bytes = 0 if nk == 1 else _nbytes((tm, tn), F32)
    grid = (n // tn, m // tm) + (() if nk == 1 else (nk,))
    if nk == 1:
        a_map, b_map, o_map = (lambda j, i: (i, 0)), (lambda j, i: (0, j)), (lambda j, i: (i, j))
    else:
        a_map, b_map, o_map = (lambda j, i, kk: (i, kk)), (lambda j, i, kk: (kk, j)), (lambda j, i, kk: (i, j))
    return pl.pallas_call(
        functools.partial(kernel, nk=nk),
        out_shape=jax.ShapeDtypeStruct((m, n), out_dtype),
        grid=grid,
        in_specs=[pl.BlockSpec((tm, tk), a_map), pl.BlockSpec((tk, tn), b_map)] + extra_specs(tm, tn, nk),
        out_specs=pl.BlockSpec((tm, tn), o_map),
        scratch_shapes=scratch,
        compiler_params=pltpu.CompilerParams(
            dimension_semantics=("arbitrary",) * len(grid),
            vmem_limit_bytes=_vmem_limit(blocks, scratch_bytes + 2 * _nbytes((tm, tn), F32))),
        name=name,
    )(a, b, *extra_inputs)


def matmul(a, b, out_dtype, act=None, name="matmul"):
    return _mm_call(functools.partial(_mm_plain_kernel, act=act), a, b, (), lambda tm, tn, nk: [],
                    out_dtype, lambda tm, tn: 0, name)


def matmul_residual(a, b, x, gate, ctx, name="matmul_residual"):
    tm = _mm_tiles(a.shape[0], a.shape[1], b.shape[1])[0]

    def specs(tm_, tn, nk):
        if nk == 1:
            return [pl.BlockSpec((tm_, tn), lambda j, i: (i, j)), pl.BlockSpec((2, tn), lambda j, i: (0, j))]
        return [pl.BlockSpec((tm_, tn), lambda j, i, kk: (i, j)), pl.BlockSpec((2, tn), lambda j, i, kk: (0, j))]

    return _mm_call(functools.partial(_mm_residual_kernel, tm=tm, ctx=ctx), a, b, (x, gate), specs, F32,
                    lambda tm_, tn: _nbytes((tm_, tn), F32) + _nbytes((8, tn), F32), name)


def matmul_headnorm_rope(a, b, gain, cc, ss, name="matmul_headnorm_rope"):
    m, k = a.shape
    n = b.shape[1]
    tm, tn, tk = _mm_tiles(m, k, n)
    assert tk == k
    blocks = (_nbytes((tm, k), a.dtype) + _nbytes((k, tn), b.dtype) + _nbytes((tm, tn), BF16)
              + _nbytes((8, tn), F32) + 2 * _nbytes((tm, HEAD_DIM), F32) + _nbytes((tm, tn), F32))
    return pl.pallas_call(
        functools.partial(_mm_headnorm_rope_kernel, tn=tn),
        out_shape=jax.ShapeDtypeStruct((m, n), BF16),
        grid=(n // tn, m // tm),
        in_specs=[
            pl.BlockSpec((tm, k), lambda j, i: (i, 0)),
            pl.BlockSpec((k, tn), lambda j, i: (0, j)),
            pl.BlockSpec((1, tn), lambda j, i: (0, j)),
            pl.BlockSpec((tm, HEAD_DIM), lambda j, i: (i, 0)),
            pl.BlockSpec((tm, HEAD_DIM), lambda j, i: (i, 0)),
        ],
        out_specs=pl.BlockSpec((tm, tn), lambda j, i: (i, j)),
        compiler_params=_params(("arbitrary", "arbitrary"), blocks, 2 * _nbytes((tm, tn), F32)),
        name=name,
    )(a, b, gain.reshape(1, n), cc, ss)


SCORE_BOUND_DIRECT = 64.0


def _flash_finalize(o_ref, l_rows, acc_ref, lam_ref, subln_ref, *, n_sub, lam_init):
    if lam_init is None:
        for s in range(n_sub):
            o_ref[:, s * HEAD_DIM:(s + 1) * HEAD_DIM] = (acc_ref[s] / l_rows[s]).astype(o_ref.dtype)
        return
    lv = lam_ref[...]
    lam = (jnp.exp(jnp.sum(lv[0:1, :] * lv[1:2, :], axis=-1, keepdims=True))
           - jnp.exp(jnp.sum(lv[2:3, :] * lv[3:4, :], axis=-1, keepdims=True)) + lam_init)
    o = acc_ref[0] / l_rows[0] - lam * (acc_ref[1] / l_rows[1])
    o = o * lax.rsqrt(jnp.mean(o * o, axis=-1, keepdims=True) + NORM_EPS) * subln_ref[...]
    o_ref[...] = (o * (1.0 - lam_init)).astype(o_ref.dtype)


def _flash_direct_kernel(q_ref, kt_ref, v_ref, *rest, n_sub, per_sub_k, tq, tc, ctx, n_chunks, lam_init):
    if lam_init is None:
        lam_ref = subln_ref = None
        o_ref, l_ref, acc_ref = rest
    else:
        lam_ref, subln_ref, o_ref, l_ref, acc_ref = rest
    q_is_ctx = (pl.program_id(1) + 1) * tq <= ctx
    l_ref[...] = jnp.zeros_like(l_ref)
    acc_ref[...] = jnp.zeros_like(acc_ref)

    def accumulate(chunk, width, v):
        for s in range(n_sub):
            rows = slice(s * HEAD_DIM, (s + 1) * HEAD_DIM) if per_sub_k else slice(None)
            kt = kt_ref[chunk, rows, 0:width]
            p = jnp.exp2(jnp.dot(q_ref[:, s * HEAD_DIM:(s + 1) * HEAD_DIM], kt, preferred_element_type=F32))
            part = p[:, 0:V7X_LANES]
            for j in range(1, width // V7X_LANES):
                part = part + p[:, j * V7X_LANES:(j + 1) * V7X_LANES]
            l_ref[s] += part
            acc_ref[s] += jnp.dot(p.astype(v.dtype), v, preferred_element_type=F32)

    @pl.when(jnp.logical_not(q_is_ctx))
    def _():
        def body(c, carry):
            accumulate(c, tc, v_ref[pl.ds(pl.multiple_of(c * tc, tc), tc), :])
            return carry

        lax.fori_loop(0, n_chunks, body, 0)

    @pl.when(q_is_ctx)
    def _():
        accumulate(0, ctx, v_ref[0:ctx, :])

    l_rows = [jnp.sum(l_ref[s], axis=-1, keepdims=True) for s in range(n_sub)]
    _flash_finalize(o_ref, l_rows, acc_ref, lam_ref, subln_ref, n_sub=n_sub, lam_init=lam_init)


def _flash_online_step(q_ref, kt_ref, v_ref, m_ref, l_ref, acc_ref, *, n_sub, per_sub_k, tk, ctx, masked):
    ki = pl.program_id(2)
    v = v_ref[...]
    for s in range(n_sub):
        q = q_ref[:, s * HEAD_DIM:(s + 1) * HEAD_DIM]
        kt = kt_ref[s * HEAD_DIM:(s + 1) * HEAD_DIM, :] if per_sub_k else kt_ref[...]
        sc = jnp.dot(q, kt, preferred_element_type=F32)
        if masked:
            col = ki * tk + lax.broadcasted_iota(jnp.int32, sc.shape, 1)
            sc = jnp.where(col < ctx, sc, -jnp.inf)
        m_prev = m_ref[s]
        m_new = jnp.maximum(m_prev, jnp.max(sc, axis=-1, keepdims=True))
        alpha = jnp.exp2(m_prev - m_new)
        p = jnp.exp2(sc - m_new)
        l_ref[s] = alpha * l_ref[s] + jnp.sum(p, axis=-1, keepdims=True)
        acc_ref[s] = alpha * acc_ref[s] + jnp.dot(p.astype(v.dtype), v, preferred_element_type=F32)
        m_ref[s] = m_new


def _flash_online_kernel(q_ref, kt_ref, v_ref, *rest, n_sub, per_sub_k, tq, tk, ctx, nkv, lam_init):
    if lam_init is None:
        lam_ref = subln_ref = None
        o_ref, m_ref, l_ref, acc_ref = rest
    else:
        lam_ref, subln_ref, o_ref, m_ref, l_ref, acc_ref = rest
    qi = pl.program_id(1)
    ki = pl.program_id(2)
    q_is_ctx = (qi + 1) * tq <= ctx
    step = functools.partial(_flash_online_step, q_ref, kt_ref, v_ref, m_ref, l_ref, acc_ref,
                             n_sub=n_sub, per_sub_k=per_sub_k, tk=tk, ctx=ctx)

    @pl.when(ki == 0)
    def _():
        m_ref[...] = jnp.full(m_ref.shape, -jnp.inf, F32)
        l_ref[...] = jnp.zeros_like(l_ref)
        acc_ref[...] = jnp.zeros_like(acc_ref)

    @pl.when(jnp.logical_not(q_is_ctx))
    def _():
        step(masked=False)

    @pl.when(jnp.logical_and(q_is_ctx, ki * tk < ctx))
    def _():
        step(masked=True)

    @pl.when(ki == nkv - 1)
    def _():
        _flash_finalize(o_ref, [l_ref[s] for s in range(n_sub)], acc_ref, lam_ref, subln_ref, n_sub=n_sub,
                        lam_init=lam_init)


def flash_attention(q, kt, v, ctx, score_bound, *, groups, n_sub, per_sub_k, lam=None, subln=None, lam_init=None,
                    name="flash"):
    t = q.shape[0]
    g = groups
    qw = n_sub * HEAD_DIM
    kw = kt.shape[0] // g
    dv = v.shape[1] // g
    ow = dv if lam_init is not None else qw
    tq = _tile(ctx, (256, 128, 64, 32, 16, 8))
    tk = _tile(t, (1280, 1024, 768, 512, 256, 128))
    assert t % tq == 0 and ctx % tq == 0 and ctx <= tk and ctx % V7X_LANES == 0
    nkv = t // tk
    n_ctx_q = ctx // tq
    extra_specs3 = extra_specs2 = []
    extra_inputs = []
    if lam_init is not None:
        extra_specs3 = [pl.BlockSpec(lam.shape, lambda h, qi, ki: (0, 0)),
                        pl.BlockSpec((1, dv), lambda h, qi, ki: (0, 0))]
        extra_specs2 = [pl.BlockSpec(lam.shape, lambda h, qi: (0, 0)), pl.BlockSpec((1, dv), lambda h, qi: (0, 0))]
        extra_inputs = [lam, subln.reshape(1, dv)]
    out_shape = jax.ShapeDtypeStruct((t, g * ow), BF16)
    tile_bytes = _nbytes((tq, qw), BF16) + _nbytes((tq, ow), BF16)
    temp_bytes = 4 * _nbytes((tq, tk), F32) + n_sub * _nbytes((tq, dv), F32)

    def direct(q, kt, v):
        kt_chunks = kt.reshape(g, kw, nkv, tk).transpose(0, 2, 1, 3)
        blocks = tile_bytes + _nbytes((kw, t), BF16) + _nbytes((t, dv), BF16)
        return pl.pallas_call(
            functools.partial(_flash_direct_kernel, n_sub=n_sub, per_sub_k=per_sub_k, tq=tq, tc=tk, ctx=ctx,
                              n_chunks=nkv, lam_init=lam_init),
            out_shape=out_shape,
            grid=(g, t // tq),
            in_specs=[pl.BlockSpec((tq, qw), lambda h, qi: (qi, h)),
                      pl.BlockSpec((None, nkv, kw, tk), lambda h, qi: (h, 0, 0, 0)),
                      pl.BlockSpec((t, dv), lambda h, qi: (0, h))] + extra_specs2,
            out_specs=pl.BlockSpec((tq, ow), lambda h, qi: (qi, h)),
            scratch_shapes=[pltpu.VMEM((n_sub, tq, V7X_LANES), F32), pltpu.VMEM((n_sub, tq, dv), F32)],
            compiler_params=pltpu.CompilerParams(
                dimension_semantics=("arbitrary", "arbitrary"),
                vmem_limit_bytes=_vmem_limit(blocks, temp_bytes + n_sub * _nbytes((tq, V7X_LANES), F32))),
            name=name + "_direct",
        )(q, kt_chunks, v, *extra_inputs)

    def online(q, kt, v):
        last_ctx_kv = (ctx - 1) // tk

        def kv_index(qi, ki):
            return jnp.where(qi < n_ctx_q, jnp.minimum(ki, last_ctx_kv), ki)

        blocks = tile_bytes + _nbytes((kw, tk), BF16) + _nbytes((tk, dv), BF16)
        return pl.pallas_call(
            functools.partial(_flash_online_kernel, n_sub=n_sub, per_sub_k=per_sub_k, tq=tq, tk=tk, ctx=ctx, nkv=nkv,
                              lam_init=lam_init),
            out_shape=out_shape,
            grid=(g, t // tq, nkv),
            in_specs=[pl.BlockSpec((tq, qw), lambda h, qi, ki: (qi, h)),
                      pl.BlockSpec((kw, tk), lambda h, qi, ki: (h, kv_index(qi, ki))),
                      pl.BlockSpec((tk, dv), lambda h, qi, ki: (kv_index(qi, ki), h))] + extra_specs3,
            out_specs=pl.BlockSpec((tq, ow), lambda h, qi, ki: (qi, h)),
            scratch_shapes=[pltpu.VMEM((n_sub, tq, 1), F32), pltpu.VMEM((n_sub, tq, 1), F32),
                            pltpu.VMEM((n_sub, tq, dv), F32)],
            compiler_params=pltpu.CompilerParams(
                dimension_semantics=("arbitrary", "arbitrary", "arbitrary"),
                vmem_limit_bytes=_vmem_limit(blocks, temp_bytes + 2 * n_sub * _nbytes((tq, V7X_LANES), F32))),
            name=name + "_online",
        )(q, kt, v, *extra_inputs)

    return lax.cond(score_bound <= SCORE_BOUND_DIRECT, direct, online, q, kt, v)


def _seq_neighbors(buf_ref, cur, prev8, next8, rows, row0, ctx, total):
    buf_ref[0:V7X_SUBLANES, :] = prev8
    buf_ref[V7X_SUBLANES:V7X_SUBLANES + rows, :] = cur
    buf_ref[V7X_SUBLANES + rows:, :] = next8
    t = row0 + lax.broadcasted_iota(jnp.int32, (rows, 1), 0)
    has_prev = jnp.logical_and(t != 0, t != ctx)
    has_next = jnp.logical_and(t != ctx - 1, t != total - 1)
    before = jnp.where(has_prev, buf_ref[V7X_SUBLANES - 1:V7X_SUBLANES - 1 + rows, :], 0.0)
    after = jnp.where(has_next, buf_ref[V7X_SUBLANES + 1:V7X_SUBLANES + 1 + rows, :], 0.0)
    return before, after


def _glu_kernel(g_ref, u_ref, gp_ref, gn_ref, cw_ref, cb_ref, o_ref, buf_ref, *, rows, ctx, total):
    g = g_ref[...]
    before, after = _seq_neighbors(buf_ref, g, gp_ref[...], gn_ref[...], rows, pl.program_id(0) * rows, ctx, total)
    conv = before * cw_ref[0:1, :] + g * cw_ref[1:2, :] + after * cw_ref[2:3, :] + cb_ref[...]
    o_ref[...] = (jax.nn.silu(conv) * u_ref[...]).astype(o_ref.dtype)


def conv_glu_act(gu, conv_w, conv_b, ctx):
    t, f2 = gu.shape
    fp = f2 // 2
    rows = _tile(t, (256, 128, 64, 32, 16, 8))
    tf = _tile(fp, (1024, 512, 256, 128))
    nf = fp // tf
    rb = rows // V7X_SUBLANES
    last8 = t // V7X_SUBLANES - 1
    blocks = (2 * _nbytes((rows, tf), F32) + 2 * _nbytes((8, tf), F32) + 2 * _nbytes((8, tf), F32)
              + _nbytes((rows, tf), BF16))
    return pl.pallas_call(
        functools.partial(_glu_kernel, rows=rows, ctx=ctx, total=t),
        out_shape=jax.ShapeDtypeStruct((t, fp), BF16),
        grid=(t // rows, nf),
        in_specs=[
            pl.BlockSpec((rows, tf), lambda i, j: (i, j)),
            pl.BlockSpec((rows, tf), lambda i, j: (i, j + nf)),
            pl.BlockSpec((V7X_SUBLANES, tf), lambda i, j: (jnp.maximum(i * rb - 1, 0), j)),
            pl.BlockSpec((V7X_SUBLANES, tf), lambda i, j: (jnp.minimum((i + 1) * rb, last8), j)),
            pl.BlockSpec((3, tf), lambda i, j: (0, j)),
            pl.BlockSpec((1, tf), lambda i, j: (0, j)),
        ],
        out_specs=pl.BlockSpec((rows, tf), lambda i, j: (i, j)),
        scratch_shapes=[pltpu.VMEM((rows + 2 * V7X_SUBLANES, tf), F32)],
        compiler_params=pltpu.CompilerParams(
            dimension_semantics=("arbitrary", "arbitrary"),
            vmem_limit_bytes=_vmem_limit(blocks, 6 * _nbytes((rows + 16, tf), F32))),
        name="conv_glu_act",
    )(gu, gu, gu, gu, conv_w, conv_b.reshape(1, fp))


def _rw_mix_kernel(x_ref, xp_ref, xn_ref, g_ref, sh_ref, sc_ref, mix_ref, *rest, rows, ctx, total):
    outs, buf_ref = rest[:6], rest[6]
    row0 = pl.program_id(0) * rows
    g, sh, sc = g_ref[...], sh_ref[...], sc_ref[...]

    def nm(x, first_row):
        t = first_row + lax.broadcasted_iota(jnp.int32, (x.shape[0], 1), 0)
        return _norm_mod_rows(x, g, sh, sc, t < ctx)

    h = nm(x_ref[...], row0)
    hp = nm(xp_ref[...], row0 - V7X_SUBLANES)
    hn = nm(xn_ref[...], row0 + rows)
    before, after = _seq_neighbors(buf_ref, h, hp, hn, rows, row0, ctx, total)
    xx = 0.5 * (before + after) - h
    for n in range(6):
        outs[n][...] = (h + xx * mix_ref[n:n + 1, :]).astype(outs[n].dtype)


def rw_token_mix(x, g, shift, scale, mix, ctx):
    t, d = x.shape
    rows = _tile(t, (128, 64, 32, 16, 8))
    rb = rows // V7X_SUBLANES
    last8 = t // V7X_SUBLANES - 1
    blocks = _nbytes((rows + 16, d), F32) + 6 * _nbytes((rows, d), BF16) + 4 * _nbytes((8, d), F32)
    return pl.pallas_call(
        functools.partial(_rw_mix_kernel, rows=rows, ctx=ctx, total=t),
        out_shape=[jax.ShapeDtypeStruct((t, d), BF16)] * 6,
        grid=(t // rows,),
        in_specs=[
            pl.BlockSpec((rows, d), lambda i: (i, 0)),
            pl.BlockSpec((V7X_SUBLANES, d), lambda i: (jnp.maximum(i * rb - 1, 0), 0)),
            pl.BlockSpec((V7X_SUBLANES, d), lambda i: (jnp.minimum((i + 1) * rb, last8), 0)),
            pl.BlockSpec((1, d), lambda i: (0, 0)),
            pl.BlockSpec((2, d), lambda i: (0, 0)),
            pl.BlockSpec((2, d), lambda i: (0, 0)),
            pl.BlockSpec((6, d), lambda i: (0, 0)),
        ],
        out_specs=[pl.BlockSpec((rows, d), lambda i: (i, 0))] * 6,
        scratch_shapes=[pltpu.VMEM((rows + 2 * V7X_SUBLANES, d), F32)],
        compiler_params=pltpu.CompilerParams(
            dimension_semantics=("arbitrary",),
            vmem_limit_bytes=_vmem_limit(blocks, 6 * _nbytes((rows + 16, d), F32))),
        name="rw_token_mix",
    )(x, x, x, g.reshape(1, d), shift, scale, mix)


SCAN_STEPS = 16


def _mirror_block(i, n_blocks, n_ctx_blocks):
    return jnp.where(i < n_ctx_blocks, n_ctx_blocks - 1 - i, n_blocks + n_ctx_blocks - 1 - i)


def _scan_specs(t, chans, lanes, ctx):
    assert ctx % SCAN_STEPS == 0 and t % SCAN_STEPS == 0
    nb, nbc = t // SCAN_STEPS, ctx // SCAN_STEPS
    here = pl.BlockSpec((SCAN_STEPS, chans, lanes), lambda i: (i, 0, 0))
    mirror = pl.BlockSpec((SCAN_STEPS, chans, lanes), lambda i: (_mirror_block(i, nb, nbc), 0, 0))
    return nb, here, mirror


def _rw_prep_kernel(ra, rb, ka, kb, va, vb, wla, wlb, ala, alb, w0_ref, a0_ref, kk_ref, ka_ref, rk_ref,
                    w_o, a_o, b_o, kd_o, wr_o, v_o, bonus_o, br_o, kr_o):
    lanes = ra.shape[-1]
    is_fwd = lax.broadcasted_iota(jnp.int32, (1, lanes), 1) < lanes // 2

    def body(s, carry):
        sb = SCAN_STEPS - 1 - s

        def pick(xa, xb):
            return jnp.where(is_fwd, xa[s], xb[sb])

        r, k, v = pick(ra, rb), pick(ka, kb), pick(va, vb)
        kk = k * kk_ref[...]
        kk = kk * lax.rsqrt(jnp.maximum(jnp.sum(kk * kk, axis=0, keepdims=True), 1e-24))
        w = jnp.exp(-jnp.exp(-jax.nn.softplus(-(w0_ref[...] + pick(wla, wlb))) - 0.5))
        iclr = jax.nn.sigmoid(a0_ref[...] + pick(ala, alb))
        kd = k * (1.0 + (iclr - 1.0) * ka_ref[...])
        b = kk * iclr
        w_o[s] = w
        a_o[s] = -kk
        b_o[s] = b
        kd_o[s] = kd
        wr_o[s] = w * r
        v_o[s] = v
        br_o[s] = jnp.sum(b * r, axis=0, keepdims=True)
        kr_o[s] = jnp.sum(kd * r, axis=0, keepdims=True)
        bonus_o[s] = jnp.sum(r * kd * rk_ref[...], axis=0, keepdims=True) * v
        return carry

    lax.fori_loop(0, SCAN_STEPS, body, 0)


def rw_prepare(r, k, v, wl, al, w0, a0, k_k, k_a, r_k, ctx):
    t, chans, lanes = r.shape
    nb, here, mirror = _scan_specs(t, chans, lanes, ctx)
    par = pl.BlockSpec((chans, lanes), lambda i: (0, 0))
    small = pl.BlockSpec((SCAN_STEPS, 1, lanes), lambda i: (i, 0, 0))
    big = jax.ShapeDtypeStruct((t, chans, lanes), F32)
    row = jax.ShapeDtypeStruct((t, 1, lanes), F32)
    blocks = 17 * _nbytes((SCAN_STEPS, chans, lanes), F32) + 5 * _nbytes((chans, lanes), F32)
    return pl.pallas_call(
        _rw_prep_kernel,
        out_shape=[big] * 7 + [row] * 2,
        grid=(nb,),
        in_specs=[here, mirror] * 5 + [par] * 5,
        out_specs=[here] * 7 + [small] * 2,
        compiler_params=_params(("arbitrary",), blocks, 16 * _nbytes((chans, lanes), F32)),
        name="rw_prepare",
    )(r, r, k, k, v, v, wl, wl, al, al, w0, a0, k_k, k_a, r_k)


def _wkv_kernel(w_ref, a_ref, b_ref, kd_ref, wr_ref, v_ref, br_ref, kr_ref, y_ref, s_ref, *, steps, chans):
    @pl.when(pl.program_id(0) == 0)
    def _():
        s_ref[...] = jnp.zeros_like(s_ref)

    def step(i, carry):
        sa = jnp.zeros(s_ref.shape[1:], F32)
        u = jnp.zeros(s_ref.shape[1:], F32)
        for c in range(chans):
            sc = s_ref[c]
            sa = sa + sc * a_ref[i, c:c + 1, :]
            u = u + sc * wr_ref[i, c:c + 1, :]
        vv = v_ref[i]
        for c in range(chans):
            s_ref[c] = s_ref[c] * w_ref[i, c:c + 1, :] + sa * b_ref[i, c:c + 1, :] + vv * kd_ref[i, c:c + 1, :]
        y_ref[i] = u + sa * br_ref[i] + vv * kr_ref[i]
        return carry

    lax.fori_loop(0, steps, step, 0)


def wkv_scan(w, a, b, kd, wr, v, br, kr):
    t, chans, lanes = w.shape
    steps = SCAN_STEPS
    big = pl.BlockSpec((steps, chans, lanes), lambda i: (i, 0, 0))
    small = pl.BlockSpec((steps, 1, lanes), lambda i: (i, 0, 0))
    blocks = 7 * _nbytes((steps, chans, lanes), F32) + 2 * _nbytes((steps, 8, lanes), F32)
    state_bytes = _nbytes((chans, chans, lanes), F32)
    return pl.pallas_call(
        functools.partial(_wkv_kernel, steps=steps, chans=chans),
        out_shape=jax.ShapeDtypeStruct((t, chans, lanes), F32),
        grid=(t // steps,),
        in_specs=[big, big, big, big, big, big, small, small],
        out_specs=big,
        scratch_shapes=[pltpu.VMEM((chans, chans, lanes), F32)],
        compiler_params=pltpu.CompilerParams(
            dimension_semantics=("arbitrary",), vmem_limit_bytes=_vmem_limit(blocks, state_bytes)),
        name="wkv_scan",
    )(w, a, b, kd, wr, v, br, kr)


def _rw_finish_kernel(ya, yb, ba, bb, gate_ref, lnw_ref, lnb_ref, o_ref):
    half = ya.shape[-1] // 2

    def body(s, carry):
        sb = SCAN_STEPS - 1 - s
        y = (ya[s] + pltpu.roll(yb[sb], half, 1))[:, :half]
        bonus = (ba[s] + pltpu.roll(bb[sb], half, 1))[:, :half]
        mu = jnp.mean(y, axis=0, keepdims=True)
        yc = y - mu
        var = jnp.mean(yc * yc, axis=0, keepdims=True)
        z = yc * lax.rsqrt(var + RW_LN_EPS) * lnw_ref[...] + lnb_ref[...] + bonus
        o_ref[s] = (z * gate_ref[s]).astype(o_ref.dtype)
        return carry

    lax.fori_loop(0, SCAN_STEPS, body, 0)


def rw_finish(y, bonus, gate, ln_w, ln_b, ctx):
    t, chans, lanes = y.shape
    heads = lanes // 2
    nb, here, mirror = _scan_specs(t, chans, lanes, ctx)
    half = pl.BlockSpec((SCAN_STEPS, chans, heads), lambda i: (i, 0, 0))
    par = pl.BlockSpec((chans, heads), lambda i: (0, 0))
    blocks = 6 * _nbytes((SCAN_STEPS, chans, lanes), F32) + 2 * _nbytes((chans, lanes), F32)
    return pl.pallas_call(
        _rw_finish_kernel,
        out_shape=jax.ShapeDtypeStruct((t, chans, heads), BF16),
        grid=(nb,),
        in_specs=[here, mirror, here, mirror, half, par, par],
        out_specs=half,
        compiler_params=_params(("arbitrary",), blocks, 8 * _nbytes((chans, lanes), F32)),
        name="rw_finish",
    )(y, y, bonus, bonus, gate, ln_w, ln_b)


def _rope_tables(ctx, seq):
    pos = jnp.arange(seq)
    row = (pos // GRID_W).astype(F32)
    col = (pos % GRID_W).astype(F32)
    n_freq = HEAD_DIM // 4
    inv_freq = ROPE_THETA ** (-jnp.arange(n_freq, dtype=F32) / n_freq)
    ang = jnp.concatenate([row[:, None] * inv_freq, col[:, None] * inv_freq], axis=-1)
    cos = jnp.concatenate([jnp.ones((ctx, HEAD_DIM // 2), F32), jnp.cos(ang)], axis=0)
    sin = jnp.concatenate([jnp.zeros((ctx, HEAD_DIM // 2), F32), jnp.sin(ang)], axis=0)
    return jnp.concatenate([cos, cos], axis=-1), jnp.concatenate([-sin, sin], axis=-1)


def _channel_major(w):
    heads = w.shape[-1] // RW_HEAD_DIM
    return jnp.swapaxes(w.reshape(w.shape[:-1] + (heads, RW_HEAD_DIM)), -1, -2)


def _both_directions(w_cm):
    return jnp.concatenate([w_cm, w_cm], axis=-1).reshape(w_cm.shape[:-2] + (-1,))


def _per_direction(w2):
    cm = _channel_major(w2)
    zero = jnp.zeros_like(cm[0])
    top = jnp.concatenate([cm[0], zero], axis=-1)
    bottom = jnp.concatenate([zero, cm[1]], axis=-1)
    return jnp.concatenate([top, bottom], axis=0).reshape(2 * w2.shape[1], -1)


def _score_gain(q_g, k_g):
    q_gain = q_g * (HEAD_DIM ** -0.5 * math.log2(math.e))
    bound = HEAD_DIM * jnp.max(jnp.abs(q_gain)) * jnp.max(jnp.abs(k_g))
    return q_gain, bound


def _diff_lambda_init(layer_idx):
    return 0.8 - 0.6 * math.exp(-0.3 * layer_idx)


def _diff_attention(h, cc, ss, wqkv, wo, q_g, k_g, lam_vecs, subln_g, lam_init, x, gate, ctx):
    d = h.shape[1]
    n_heads = d // HEAD_DIM
    q_gain, bound = _score_gain(q_g, k_g)
    gain = jnp.concatenate([jnp.tile(q_gain, n_heads), jnp.tile(k_g, n_heads)])
    qk = matmul_headnorm_rope(h, wqkv[:, :2 * d].astype(BF16), gain, cc, ss, name="da_qk_proj")
    v = matmul(h, wqkv[:, 2 * d:].astype(BF16), BF16, name="da_v_proj")
    o = flash_attention(qk, qk[:, d:].T, v, ctx, bound, groups=n_heads // 2, n_sub=2, per_sub_k=True, lam=lam_vecs,
                        subln=subln_g, lam_init=lam_init, name="da_flash")
    return matmul_residual(o, wo.astype(BF16), x, gate, ctx, name="da_out_proj")


def _gqa_attention(h, cc, ss, wqkv, wo, q_g, k_g, x, gate, ctx):
    d = h.shape[1]
    kv_w = d // GA_GROUP
    q_gain, bound = _score_gain(q_g, k_g)
    gain = jnp.concatenate([jnp.tile(q_gain, d // HEAD_DIM), jnp.tile(k_g, kv_w // HEAD_DIM)])
    qk = matmul_headnorm_rope(h, wqkv[:, :d + kv_w].astype(BF16), gain, cc, ss, name="ga_qk_proj")
    v = matmul(h, wqkv[:, d + kv_w:].astype(BF16), BF16, name="ga_v_proj")
    o = flash_attention(qk, qk[:, d:].T, v, ctx, bound, groups=kv_w // HEAD_DIM, n_sub=GA_GROUP, per_sub_k=False,
                        name="ga_flash")
    return matmul_residual(o, wo.astype(BF16), x, gate, ctx, name="ga_out_proj")


def _pad_cols(w, n):
    return jnp.pad(w, ((0, 0), (0, n - w.shape[1])))


def _rwkv7(x, g, shift, scale, mix, wrkv, wo, w0, w1, w2, a0, a1, a2, g1, g2, k_k, k_a, r_k, ln_w, ln_b, gate,
           ctx):
    t, d = x.shape
    heads = d // RW_HEAD_DIM
    scan_shape = (t, RW_HEAD_DIM, 2 * heads)
    xr, xw, xk, xv, xa, xg = rw_token_mix(x, g, shift, scale, mix, ctx)

    def shared(xin, w, name):
        return matmul(xin, _both_directions(_channel_major(w)).astype(BF16), F32, name=name).reshape(scan_shape)

    r = shared(xr, wrkv[0], "rw_r_proj")
    k = shared(xk, wrkv[1], "rw_k_proj")
    v = shared(xv, wrkv[2], "rw_v_proj")
    hw = matmul(xw, jnp.concatenate([w1[0], w1[1]], axis=1).astype(BF16), BF16, act="tanh", name="rw_w_lora_in")
    ha = matmul(xa, jnp.concatenate([a1[0], a1[1]], axis=1).astype(BF16), BF16, name="rw_a_lora_in")
    wl = matmul(hw, _per_direction(w2).astype(BF16), F32, name="rw_w_lora_out").reshape(scan_shape)
    al = matmul(ha, _per_direction(a2).astype(BF16), F32, name="rw_a_lora_out").reshape(scan_shape)
    lg = -(-g1.shape[1] // V7X_LANES) * V7X_LANES
    hg = matmul(xg, _pad_cols(g1, lg).astype(BF16), BF16, act="sigmoid", name="rw_g_lora_in")
    g2p = jnp.pad(_channel_major(g2).reshape(g2.shape), ((0, lg - g2.shape[0]), (0, 0))).astype(BF16)
    out_gate = matmul(hg, g2p, F32, name="rw_g_lora_out").reshape(t, RW_HEAD_DIM, heads)

    def dir_param(p):
        return jnp.concatenate([_channel_major(p[0]), _channel_major(p[1])], axis=-1)

    def shared_param(p):
        return jnp.concatenate([_channel_major(p)] * 2, axis=-1)

    w, a, b, kd, wr, vs, bonus, br, kr = rw_prepare(
        r, k, v, wl, al, dir_param(w0), dir_param(a0), shared_param(k_k), shared_param(k_a),
        shared_param(r_k.reshape(d)), ctx)
    y = wkv_scan(w, a, b, kd, wr, vs, br, kr)
    z = rw_finish(y, bonus, out_gate, _channel_major(ln_w), _channel_major(ln_b), ctx).reshape(t, d)
    wo_cm = jnp.swapaxes(wo.reshape(heads, RW_HEAD_DIM, d), 0, 1).reshape(d, d)
    return matmul_residual(z, wo_cm.astype(BF16), x, gate, ctx, name="rw_out_proj")


def _conv_glu(h, w_in, conv_w, conv_b, w_out, x, gate, ctx):
    f = conv_w.shape[1]
    fp = -(-f // FFN_PAD) * FFN_PAD
    w_in_p = jnp.concatenate([_pad_cols(w_in[:, :f], fp), _pad_cols(w_in[:, f:], fp)], axis=1).astype(BF16)
    gu = matmul(h, w_in_p, F32, name="ffn_in_proj")
    act = conv_glu_act(gu, _pad_cols(conv_w, fp), jnp.pad(conv_b, (0, fp - f)), ctx)
    w_out_p = jnp.pad(w_out, ((0, fp - f), (0, 0))).astype(BF16)
    return matmul_residual(act, w_out_p, x, gate, ctx, name="ffn_out_proj")


def kernel(x, c, ctx, c_ctx, ada_down, ada_up, ada_b, norm_g, ffn_in, ffn_conv, ffn_conv_b, ffn_out, da_wqkv, da_wo, da_q_g, da_k_g, da_lambda, da_subln_g, rw_mix, rw_wrkv, rw_wo, rw_w0, rw_w1, rw_w2, rw_a0, rw_a1, rw_a2, rw_g1, rw_g2, rw_k_k, rw_k_a, rw_r_k, rw_ln_w, rw_ln_b, ga_wqkv, ga_wo, ga_q_g, ga_k_g):
    assert x.shape[0] == 1
    seq, d = x.shape[1], x.shape[2]
    n_ctx = ctx.shape[1]
    depth = ada_down.shape[0]
    stream = jnp.concatenate([ctx[0], x[0]], axis=0)
    cond = jnp.zeros((V7X_SUBLANES, d), F32).at[0].set(c_ctx).at[1].set(c[0])
    mod = ada_modulation(cond, ada_down, ada_up, ada_b)[:, :2].reshape(depth, 2, N_MOD, d)
    cc, ss = _rope_tables(n_ctx, seq)
    for i in range(depth):
        kind, j = i % N_MIXERS, i // N_MIXERS
        m = [mod[i, :, n, :] for n in range(N_MOD)]
        if kind == 1:
            stream = _rwkv7(stream, norm_g[i, 0], m[0], m[1], rw_mix[j], rw_wrkv[j], rw_wo[j], rw_w0[j], rw_w1[j],
                            rw_w2[j], rw_a0[j], rw_a1[j], rw_a2[j], rw_g1[j], rw_g2[j], rw_k_k[j], rw_k_a[j],
                            rw_r_k[j], rw_ln_w[j], rw_ln_b[j], m[2], n_ctx)
        else:
            h = norm_mod(stream, norm_g[i, 0], m[0], m[1], n_ctx)
            if kind == 0:
                stream = _diff_attention(h, cc, ss, da_wqkv[j], da_wo[j], da_q_g[j], da_k_g[j], da_lambda[j],
                                         da_subln_g[j], _diff_lambda_init(i), stream, m[2], n_ctx)
            else:
                stream = _gqa_attention(h, cc, ss, ga_wqkv[j], ga_wo[j], ga_q_g[j], ga_k_g[j], stream, m[2], n_ctx)
        h2 = norm_mod(stream, norm_g[i, 1], m[3], m[4], n_ctx)
        stream = _conv_glu(h2, ffn_in[i], ffn_conv[i], ffn_conv_b[i], ffn_out[i], stream, m[5], n_ctx)
    return stream[n_ctx:][None]
```

```python
import functools
import math

import jax
import jax.numpy as jnp
from jax import lax
from jax.experimental import pallas as pl
from jax.experimental.pallas import tpu as pltpu

F32 = jnp.float32
BF16 = jnp.bfloat16

N_MOD = 6
N_MIXERS = 3
GRID_W = 64
ROPE_THETA = 10000.0
NORM_EPS = 1e-6
HEAD_DIM = 128
GA_GROUP = 4
RW_HEAD_DIM = 64
RW_LN_EPS = 64e-5

V7X_LANES = 128
V7X_SUBLANES = 8
V7X_VMEM_BYTES = 64 * 1024 * 1024
V7X_VMEM_RESERVE = 6 * 1024 * 1024
FFN_PAD = 1024


def _vmem_limit(block_bytes, temp_bytes=0):
    return int(min(2 * block_bytes + temp_bytes + V7X_VMEM_RESERVE, V7X_VMEM_BYTES - V7X_VMEM_RESERVE))


def _params(semantics, block_bytes, temp_bytes=0):
    return pltpu.CompilerParams(dimension_semantics=semantics,
                                vmem_limit_bytes=_vmem_limit(block_bytes, temp_bytes))


def _tile(n, prefs):
    for t in prefs:
        if n % t == 0:
            return t
    return n


def _nbytes(shape, dtype):
    return math.prod(shape) * jnp.dtype(dtype).itemsize


def _row_ids(i, rows, shape):
    return i * rows + lax.broadcasted_iota(jnp.int32, shape, 0)


def _ada_kernel(cond_ref, down_ref, up_ref, b_ref, o_ref):
    s = jax.nn.silu(cond_ref[...])
    t = jnp.dot(s, down_ref[...], preferred_element_type=F32, precision=lax.Precision.HIGHEST)
    m = jnp.dot(t, up_ref[...], preferred_element_type=F32, precision=lax.Precision.HIGHEST)
    o_ref[...] = m + b_ref[...]


def ada_modulation(cond, ada_down, ada_up, ada_b):
    depth, d, r = ada_down.shape
    n = ada_up.shape[-1]
    tn = _tile(n, (4096, 2048, 1024, 512))
    blocks = _nbytes((8, d), F32) + _nbytes((d, r), F32) + _nbytes((r, tn), F32) + 2 * _nbytes((8, tn), F32)
    return pl.pallas_call(
        _ada_kernel,
        out_shape=jax.ShapeDtypeStruct((depth, 8, n), F32),
        grid=(depth, n // tn),
        in_specs=[
            pl.BlockSpec((8, d), lambda l, j: (0, 0)),
            pl.BlockSpec((None, d, r), lambda l, j: (l, 0, 0)),
            pl.BlockSpec((None, r, tn), lambda l, j: (l, 0, j)),
            pl.BlockSpec((None, 1, tn), lambda l, j: (l, 0, j)),
        ],
        out_specs=pl.BlockSpec((None, 8, tn), lambda l, j: (l, 0, j)),
        compiler_params=_params(("arbitrary", "arbitrary"), blocks),
        name="ada_modulation",
    )(cond, ada_down, ada_up, ada_b.reshape(depth, 1, n))


def _norm_mod_rows(x, g, shift, scale, is_ctx):
    y = x * lax.rsqrt(jnp.mean(x * x, axis=-1, keepdims=True) + NORM_EPS) * g
    sc = jnp.where(is_ctx, scale[0:1, :], scale[1:2, :])
    sh = jnp.where(is_ctx, shift[0:1, :], shift[1:2, :])
    return y * (1.0 + sc) + sh


def _norm_mod_kernel(x_ref, g_ref, sh_ref, sc_ref, o_ref, *, rows, ctx):
    is_ctx = _row_ids(pl.program_id(0), rows, (rows, 1)) < ctx
    o_ref[...] = _norm_mod_rows(x_ref[...], g_ref[...], sh_ref[...], sc_ref[...], is_ctx).astype(o_ref.dtype)


def norm_mod(x, g, shift, scale, ctx):
    t, d = x.shape
    rows = _tile(t, (256, 128, 64, 32, 16, 8))
    blocks = _nbytes((rows, d), F32) + _nbytes((rows, d), BF16) + 5 * _nbytes((8, d), F32)
    return pl.pallas_call(
        functools.partial(_norm_mod_kernel, rows=rows, ctx=ctx),
        out_shape=jax.ShapeDtypeStruct((t, d), BF16),
        grid=(t // rows,),
        in_specs=[
            pl.BlockSpec((rows, d), lambda i: (i, 0)),
            pl.BlockSpec((1, d), lambda i: (0, 0)),
            pl.BlockSpec((2, d), lambda i: (0, 0)),
            pl.BlockSpec((2, d), lambda i: (0, 0)),
        ],
        out_specs=pl.BlockSpec((rows, d), lambda i: (i, 0)),
        compiler_params=_params(("arbitrary",), blocks, 4 * _nbytes((rows, d), F32)),
        name="norm_mod",
    )(x, g.reshape(1, d), shift, scale)


def _mm_accumulate(a_ref, b_ref, acc_ref, nk, finish):
    if nk == 1:
        finish(jnp.dot(a_ref[...], b_ref[...], preferred_element_type=F32))
        return
    k = pl.program_id(2)

    @pl.when(k == 0)
    def _():
        acc_ref[...] = jnp.zeros_like(acc_ref)

    acc_ref[...] += jnp.dot(a_ref[...], b_ref[...], preferred_element_type=F32)

    @pl.when(k == nk - 1)
    def _():
        finish(acc_ref[...])


def _mm_plain_kernel(a_ref, b_ref, o_ref, *scratch, nk, act):
    def finish(acc):
        if act == "tanh":
            acc = jnp.tanh(acc)
        elif act == "sigmoid":
            acc = jax.nn.sigmoid(acc)
        o_ref[...] = acc.astype(o_ref.dtype)

    _mm_accumulate(a_ref, b_ref, scratch[0] if scratch else None, nk, finish)


def _mm_residual_kernel(a_ref, b_ref, x_ref, g_ref, o_ref, *scratch, nk, tm, ctx):
    def finish(acc):
        is_ctx = _row_ids(pl.program_id(1), tm, (tm, 1)) < ctx
        gate = jnp.where(is_ctx, g_ref[0:1, :], g_ref[1:2, :])
        o_ref[...] = x_ref[...] + gate * acc

    _mm_accumulate(a_ref, b_ref, scratch[0] if scratch else None, nk, finish)


def _mm_headnorm_rope_kernel(a_ref, b_ref, g_ref, cc_ref, ss_ref, o_ref, *, tn):
    acc = jnp.dot(a_ref[...], b_ref[...], preferred_element_type=F32)
    cc = cc_ref[...]
    ss = ss_ref[...]
    for h in range(tn // HEAD_DIM):
        sl = slice(h * HEAD_DIM, (h + 1) * HEAD_DIM)
        z = acc[:, sl]
        y = z * lax.rsqrt(jnp.mean(z * z, axis=-1, keepdims=True) + NORM_EPS) * g_ref[:, sl]
        o_ref[:, sl] = (y * cc + pltpu.roll(y, HEAD_DIM // 2, 1) * ss).astype(o_ref.dtype)


def _mm_tiles(m, k, n):
    tm = _tile(m, (640, 512, 384, 256, 128, 64, 32, 16, 8))
    tn = _tile(n, (1024, 512, 256, 128))
    tk = k if k <= 4096 else _tile(k, (2816, 2048, 1024, 512))
    return tm, tn, tk


def _mm_call(kernel, a, b, extra_inputs, extra_specs, out_dtype, extra_bytes, name):
    m, k = a.shape
    n = b.shape[1]
    tm, tn, tk = _mm_tiles(m, k, n)
    nk = k // tk
    blocks = (_nbytes((tm, tk), a.dtype) + _nbytes((tk, tn), b.dtype) + _nbytes((tm, tn), out_dtype)
              + extra_bytes(tm, tn))
    scratch = [] if nk == 1 else [pltpu.VMEM((tm, tn), F32)]
    scratch_bytes = 0 if nk == 1 else _nbytes((tm, tn), F32)
    grid = (n // tn, m // tm) + (() if nk == 1 else (nk,))
    if nk == 1:
        a_map, b_map, o_map = (lambda j, i: (i, 0)), (lambda j, i: (0, j)), (lambda j, i: (i, j))
    else:
        a_map, b_map, o_map = (lambda j, i, kk: (i, kk)), (lambda j, i, kk: (kk, j)), (lambda j, i, kk: (i, j))
    return pl.pallas_call(
        functools.partial(kernel, nk=nk),
        out_shape=jax.ShapeDtypeStruct((m, n), out_dtype),
        grid=grid,
        in_specs=[pl.BlockSpec((tm, tk), a_map), pl.BlockSpec((tk, tn), b_map)] + extra_specs(tm, tn, nk),
        out_specs=pl.BlockSpec((tm, tn), o_map),
        scratch_shapes=scratch,
        compiler_params=pltpu.CompilerParams(
            dimension_semantics=("arbitrary",) * len(grid),
            vmem_limit_bytes=_vmem_limit(blocks, scratch_bytes + 2 * _nbytes((tm, tn), F32))),
        name=name,
    )(a, b, *extra_inputs)


def matmul(a, b, out_dtype, act=None, name="matmul"):
    return _mm_call(functools.partial(_mm_plain_kernel, act=act), a, b, (), lambda tm, tn, nk: [],
                    out_dtype, lambda tm, tn: 0, name)


def matmul_residual(a, b, x, gate, ctx, name="matmul_residual"):
    tm = _mm_tiles(a.shape[0], a.shape[1], b.shape[1])[0]

    def specs(tm_, tn, nk):
        if nk == 1:
            return [pl.BlockSpec((tm_, tn), lambda j, i: (i, j)), pl.BlockSpec((2, tn), lambda j, i: (0, j))]
        return [pl.BlockSpec((tm_, tn), lambda j, i, kk: (i, j)), pl.BlockSpec((2, tn), lambda j, i, kk: (0, j))]

    return _mm_call(functools.partial(_mm_residual_kernel, tm=tm, ctx=ctx), a, b, (x, gate), specs, F32,
                    lambda tm_, tn: _nbytes((tm_, tn), F32) + _nbytes((8, tn), F32), name)


def matmul_headnorm_rope(a, b, gain, cc, ss, name="matmul_headnorm_rope"):
    m, k = a.shape
    n = b.shape[1]
    tm, tn, tk = _mm_tiles(m, k, n)
    assert tk == k
    blocks = (_nbytes((tm, k), a.dtype) + _nbytes((k, tn), b.dtype) + _nbytes((tm, tn), BF16)
              + _nbytes((8, tn), F32) + 2 * _nbytes((tm, HEAD_DIM), F32) + _nbytes((tm, tn), F32))
    return pl.pallas_call(
        functools.partial(_mm_headnorm_rope_kernel, tn=tn),
        out_shape=jax.ShapeDtypeStruct((m, n), BF16),
        grid=(n // tn, m // tm),
        in_specs=[
            pl.BlockSpec((tm, k), lambda j, i: (i, 0)),
            pl.BlockSpec((k, tn), lambda j, i: (0, j)),
            pl.BlockSpec((1, tn), lambda j, i: (0, j)),
            pl.BlockSpec((tm, HEAD_DIM), lambda j, i: (i, 0)),
            pl.BlockSpec((tm, HEAD_DIM), lambda j, i: (i, 0)),
        ],
        out_specs=pl.BlockSpec((tm, tn), lambda j, i: (i, j)),
        compiler_params=_params(("arbitrary", "arbitrary"), blocks, 2 * _nbytes((tm, tn), F32)),
        name=name,
    )(a, b, gain.reshape(1, n), cc, ss)


SCORE_BOUND_DIRECT = 64.0


def _flash_finalize(o_ref, l_rows, acc_ref, lam_ref, subln_ref, *, n_sub, lam_init):
    if lam_init is None:
        for s in range(n_sub):
            o_ref[:, s * HEAD_DIM:(s + 1) * HEAD_DIM] = (acc_ref[s] / l_rows[s]).astype(o_ref.dtype)
        return
    lv = lam_ref[...]
    lam = (jnp.exp(jnp.sum(lv[0:1, :] * lv[1:2, :], axis=-1, keepdims=True))
           - jnp.exp(jnp.sum(lv[2:3, :] * lv[3:4, :], axis=-1, keepdims=True)) + lam_init)
    o = acc_ref[0] / l_rows[0] - lam * (acc_ref[1] / l_rows[1])
    o = o * lax.rsqrt(jnp.mean(o * o, axis=-1, keepdims=True) + NORM_EPS) * subln_ref[...]
    o_ref[...] = (o * (1.0 - lam_init)).astype(o_ref.dtype)


def _flash_direct_kernel(q_ref, kt_ref, v_ref, *rest, n_sub, per_sub_k, stack, tq, tc, ctx, n_chunks, lam_init):
    rest = list(rest)
    lam_ref, subln_ref = (rest.pop(0), rest.pop(0)) if lam_init is not None else (None, None)
    o_ref, l_ref, acc_ref = rest[:3]
    n_acc = 1 if stack else n_sub
    m_rows = n_sub * tq if stack else tq
    if stack:
        qs_ref = rest[3]
        for g in range(n_sub):
            qs_ref[g * tq:(g + 1) * tq, :] = q_ref[:, g * HEAD_DIM:(g + 1) * HEAD_DIM]
    l_ref[...] = jnp.zeros_like(l_ref)
    acc_ref[...] = jnp.zeros_like(acc_ref)

    def accumulate(chunk, width, v, r0, nr):
        for s in range(n_acc):
            krows = slice(s * HEAD_DIM, (s + 1) * HEAD_DIM) if per_sub_k else slice(None)
            q = qs_ref[r0:r0 + nr, :] if stack else q_ref[r0:r0 + nr, s * HEAD_DIM:(s + 1) * HEAD_DIM]
            p = jnp.exp2(jnp.dot(q, kt_ref[chunk, krows, 0:width], preferred_element_type=F32))
            part = p[:, 0:V7X_LANES]
            for j in range(1, width // V7X_LANES):
                part = part + p[:, j * V7X_LANES:(j + 1) * V7X_LANES]
            l_ref[s, r0:r0 + nr, :] += part
            acc_ref[s, r0:r0 + nr, :] += jnp.dot(p.astype(v.dtype), v, preferred_element_type=F32)

    def all_keys(r0, nr):
        def body(c, carry):
            accumulate(c, tc, v_ref[pl.ds(pl.multiple_of(c * tc, tc), tc), :], r0, nr)
            return carry

        lax.fori_loop(0, n_chunks, body, 0)

    ctx_rows = m_rows if tq == ctx else ctx

    @pl.when(pl.program_id(1) > 0)
    def _():
        all_keys(0, m_rows)

    @pl.when(pl.program_id(1) == 0)
    def _():
        accumulate(0, ctx, v_ref[0:ctx, :], 0, ctx_rows)
        if ctx_rows < m_rows:
            all_keys(ctx_rows, m_rows - ctx_rows)

    if stack:
        l_all = jnp.sum(l_ref[0], axis=-1, keepdims=True)
        for g in range(n_sub):
            rows = slice(g * tq, (g + 1) * tq)
            o_ref[:, g * HEAD_DIM:(g + 1) * HEAD_DIM] = (acc_ref[0, rows, :] / l_all[rows]).astype(o_ref.dtype)
    else:
        l_rows = [jnp.sum(l_ref[s], axis=-1, keepdims=True) for s in range(n_sub)]
        _flash_finalize(o_ref, l_rows, acc_ref, lam_ref, subln_ref, n_sub=n_sub, lam_init=lam_init)


def _flash_online_step(q_ref, kt_ref, v_ref, m_ref, l_ref, acc_ref, *, n_sub, per_sub_k, tk, ctx, masked):
    ki = pl.program_id(2)
    v = v_ref[...]
    for s in range(n_sub):
        q = q_ref[:, s * HEAD_DIM:(s + 1) * HEAD_DIM]
        kt = kt_ref[s * HEAD_DIM:(s + 1) * HEAD_DIM, :] if per_sub_k else kt_ref[...]
        sc = jnp.dot(q, kt, preferred_element_type=F32)
        if masked:
            col = ki * tk + lax.broadcasted_iota(jnp.int32, sc.shape, 1)
            sc = jnp.where(col < ctx, sc, -jnp.inf)
        m_prev = m_ref[s]
        m_new = jnp.maximum(m_prev, jnp.max(sc, axis=-1, keepdims=True))
        alpha = jnp.exp2(m_prev - m_new)
        p = jnp.exp2(sc - m_new)
        l_ref[s] = alpha * l_ref[s] + jnp.sum(p, axis=-1, keepdims=True)
        acc_ref[s] = alpha * acc_ref[s] + jnp.dot(p.astype(v.dtype), v, preferred_element_type=F32)
        m_ref[s] = m_new


def _flash_online_kernel(q_ref, kt_ref, v_ref, *rest, n_sub, per_sub_k, tq, tk, ctx, nkv, lam_init):
    if lam_init is None:
        lam_ref = subln_ref = None
        o_ref, m_ref, l_ref, acc_ref = rest
    else:
        lam_ref, subln_ref, o_ref, m_ref, l_ref, acc_ref = rest
    qi = pl.program_id(1)
    ki = pl.program_id(2)
    q_is_ctx = (qi + 1) * tq <= ctx
    step = functools.partial(_flash_online_step, q_ref, kt_ref, v_ref, m_ref, l_ref, acc_ref,
                             n_sub=n_sub, per_sub_k=per_sub_k, tk=tk, ctx=ctx)

    @pl.when(ki == 0)
    def _():
        m_ref[...] = jnp.full(m_ref.shape, -jnp.inf, F32)
        l_ref[...] = jnp.zeros_like(l_ref)
        acc_ref[...] = jnp.zeros_like(acc_ref)

    @pl.when(jnp.logical_not(q_is_ctx))
    def _():
        step(masked=False)

    @pl.when(jnp.logical_and(q_is_ctx, ki * tk < ctx))
    def _():
        step(masked=True)

    @pl.when(ki == nkv - 1)
    def _():
        _flash_finalize(o_ref, [l_ref[s] for s in range(n_sub)], acc_ref, lam_ref, subln_ref, n_sub=n_sub,
                        lam_init=lam_init)


def flash_attention(q, kt, v, ctx, score_bound, *, groups, n_sub, per_sub_k, lam=None, subln=None, lam_init=None,
                    name="flash"):
    t = q.shape[0]
    g = groups
    qw = n_sub * HEAD_DIM
    kw = kt.shape[0] // g
    dv = v.shape[1] // g
    ow = dv if lam_init is not None else qw
    tq = _tile(ctx, (256, 128, 64, 32, 16, 8))
    tk = _tile(t, (1280, 1024, 768, 512, 256, 128))
    assert t % tq == 0 and ctx % tq == 0 and ctx <= tk and ctx % V7X_LANES == 0
    nkv = t // tk
    n_ctx_q = ctx // tq
    extra_specs3 = extra_specs2 = []
    extra_inputs = []
    if lam_init is not None:
        extra_specs3 = [pl.BlockSpec(lam.shape, lambda h, qi, ki: (0, 0)),
                        pl.BlockSpec((1, dv), lambda h, qi, ki: (0, 0))]
        extra_specs2 = [pl.BlockSpec(lam.shape, lambda h, qi: (0, 0)), pl.BlockSpec((1, dv), lambda h, qi: (0, 0))]
        extra_inputs = [lam, subln.reshape(1, dv)]
    out_shape = jax.ShapeDtypeStruct((t, g * ow), BF16)
    tile_bytes = _nbytes((tq, qw), BF16) + _nbytes((tq, ow), BF16)
    temp_bytes = 4 * _nbytes((tq, tk), F32) + n_sub * _nbytes((tq, dv), F32)

    def direct(q, kt, v):
        stack = not per_sub_k
        tqd = tq if stack else next(c for c in (1280, 768, 512, 256, tq) if t % c == 0 and c % ctx == 0)
        m_rows = n_sub * tqd if stack else tqd
        n_acc = 1 if stack else n_sub
        kt_chunks = kt.reshape(g, kw, nkv, tk).transpose(0, 2, 1, 3)
        blocks = _nbytes((tqd, qw), BF16) + _nbytes((tqd, ow), BF16)
        resident = _nbytes((kw, t), BF16) + _nbytes((t, dv), BF16)
        scratch = [pltpu.VMEM((n_acc, m_rows, V7X_LANES), F32), pltpu.VMEM((n_acc, m_rows, dv), F32)]
        scratch_bytes = n_acc * (_nbytes((m_rows, V7X_LANES), F32) + _nbytes((m_rows, dv), F32))
        if stack:
            scratch.append(pltpu.VMEM((m_rows, HEAD_DIM), BF16))
            scratch_bytes += _nbytes((m_rows, HEAD_DIM), BF16)
        once = pl.Buffered(1)
        return pl.pallas_call(
            functools.partial(_flash_direct_kernel, n_sub=n_sub, per_sub_k=per_sub_k, stack=stack, tq=tqd, tc=tk,
                              ctx=ctx, n_chunks=nkv, lam_init=lam_init),
            out_shape=out_shape,
            grid=(g, t // tqd),
            in_specs=[pl.BlockSpec((tqd, qw), lambda h, qi: (qi, h)),
                      pl.BlockSpec((None, nkv, kw, tk), lambda h, qi: (h, 0, 0, 0), pipeline_mode=once),
                      pl.BlockSpec((t, dv), lambda h, qi: (0, h), pipeline_mode=once)] + extra_specs2,
            out_specs=pl.BlockSpec((tqd, ow), lambda h, qi: (qi, h)),
            scratch_shapes=scratch,
            compiler_params=pltpu.CompilerParams(
                dimension_semantics=("arbitrary", "arbitrary"),
                vmem_limit_bytes=_vmem_limit(
                    blocks, resident + scratch_bytes + 3 * _nbytes((m_rows, tk), F32) + 3 * _nbytes((m_rows, dv), F32))),
            name=name + "_direct",
        )(q, kt_chunks, v, *extra_inputs)

    def online(q, kt, v):
        last_ctx_kv = (ctx - 1) // tk

        def kv_index(qi, ki):
            return jnp.where(qi < n_ctx_q, jnp.minimum(ki, last_ctx_kv), ki)

        blocks = tile_bytes + _nbytes((kw, tk), BF16) + _nbytes((tk, dv), BF16)
        return pl.pallas_call(
            functools.partial(_flash_online_kernel, n_sub=n_sub, per_sub_k=per_sub_k, tq=tq, tk=tk, ctx=ctx, nkv=nkv,
                              lam_init=lam_init),
            out_shape=out_shape,
            grid=(g, t // tq, nkv),
            in_specs=[pl.BlockSpec((tq, qw), lambda h, qi, ki: (qi, h)),
                      pl.BlockSpec((kw, tk), lambda h, qi, ki: (h, kv_index(qi, ki))),
                      pl.BlockSpec((tk, dv), lambda h, qi, ki: (kv_index(qi, ki), h))] + extra_specs3,
            out_specs=pl.BlockSpec((tq, ow), lambda h, qi, ki: (qi, h)),
            scratch_shapes=[pltpu.VMEM((n_sub, tq, 1), F32), pltpu.VMEM((n_sub, tq, 1), F32),
                            pltpu.VMEM((n_sub, tq, dv), F32)],
            compiler_params=pltpu.CompilerParams(
                dimension_semantics=("arbitrary", "arbitrary", "arbitrary"),
                vmem_limit_bytes=_vmem_limit(blocks, temp_bytes + 2 * n_sub * _nbytes((tq, V7X_LANES), F32))),
            name=name + "_online",
        )(q, kt, v, *extra_inputs)

    return lax.cond(score_bound <= SCORE_BOUND_DIRECT, direct, online, q, kt, v)


def _seq_neighbors(buf_ref, cur, prev8, next8, rows, row0, ctx, total):
    buf_ref[0:V7X_SUBLANES, :] = prev8
    buf_ref[V7X_SUBLANES:V7X_SUBLANES + rows, :] = cur
    buf_ref[V7X_SUBLANES + rows:, :] = next8
    t = row0 + lax.broadcasted_iota(jnp.int32, (rows, 1), 0)
    has_prev = jnp.logical_and(t != 0, t != ctx)
    has_next = jnp.logical_and(t != ctx - 1, t != total - 1)
    before = jnp.where(has_prev, buf_ref[V7X_SUBLANES - 1:V7X_SUBLANES - 1 + rows, :], 0.0)
    after = jnp.where(has_next, buf_ref[V7X_SUBLANES + 1:V7X_SUBLANES + 1 + rows, :], 0.0)
    return before, after


BF16_ROWS = 16


def _ffn_in_kernel(h_ref, hp_ref, hn_ref, wg_ref, wu_ref, cw_ref, cb_ref, o_ref, abuf, gbuf, *, tm, ctx, total):
    abuf[0:BF16_ROWS, :] = hp_ref[...]
    abuf[BF16_ROWS:BF16_ROWS + tm, :] = h_ref[...]
    abuf[BF16_ROWS + tm:, :] = hn_ref[...]
    gbuf[...] = jnp.dot(abuf[...], wg_ref[...], preferred_element_type=F32)
    up = jnp.dot(h_ref[...], wu_ref[...], preferred_element_type=F32)
    t = pl.program_id(1) * tm + lax.broadcasted_iota(jnp.int32, (tm, 1), 0)
    has_prev = jnp.logical_and(t != 0, t != ctx)
    has_next = jnp.logical_and(t != ctx - 1, t != total - 1)
    before = jnp.where(has_prev, gbuf[BF16_ROWS - 1:BF16_ROWS - 1 + tm, :], 0.0)
    after = jnp.where(has_next, gbuf[BF16_ROWS + 1:BF16_ROWS + 1 + tm, :], 0.0)
    conv = (before * cw_ref[0:1, :] + gbuf[BF16_ROWS:BF16_ROWS + tm, :] * cw_ref[1:2, :] + after * cw_ref[2:3, :]
            + cb_ref[...])
    o_ref[...] = (jax.nn.silu(conv) * up).astype(o_ref.dtype)


def ffn_in_glu(h, w_gate, w_up, conv_w, conv_b, ctx):
    t, d = h.shape
    fp = w_gate.shape[1]
    tm, tn, _ = _mm_tiles(t, d, fp)
    hb = tm // BF16_ROWS
    last = t // BF16_ROWS - 1
    once = pl.Buffered(1)
    blocks = _nbytes((tm + 2 * BF16_ROWS, d), BF16) + _nbytes((tm, tn), BF16) + 4 * _nbytes((8, tn), F32)
    resident = 2 * _nbytes((d, tn), BF16)
    scratch = _nbytes((tm + 2 * BF16_ROWS, d), BF16) + _nbytes((tm + 2 * BF16_ROWS, tn), F32)
    return pl.pallas_call(
        functools.partial(_ffn_in_kernel, tm=tm, ctx=ctx, total=t),
        out_shape=jax.ShapeDtypeStruct((t, fp), BF16),
        grid=(fp // tn, t // tm),
        in_specs=[
            pl.BlockSpec((tm, d), lambda j, i: (i, 0)),
            pl.BlockSpec((BF16_ROWS, d), lambda j, i: (jnp.maximum(i * hb - 1, 0), 0)),
            pl.BlockSpec((BF16_ROWS, d), lambda j, i: (jnp.minimum((i + 1) * hb, last), 0)),
            pl.BlockSpec((d, tn), lambda j, i: (0, j), pipeline_mode=once),
            pl.BlockSpec((d, tn), lambda j, i: (0, j), pipeline_mode=once),
            pl.BlockSpec((3, tn), lambda j, i: (0, j)),
            pl.BlockSpec((1, tn), lambda j, i: (0, j)),
        ],
        out_specs=pl.BlockSpec((tm, tn), lambda j, i: (i, j)),
        scratch_shapes=[pltpu.VMEM((tm + 2 * BF16_ROWS, d), BF16), pltpu.VMEM((tm + 2 * BF16_ROWS, tn), F32)],
        compiler_params=pltpu.CompilerParams(
            dimension_semantics=("arbitrary", "arbitrary"),
            vmem_limit_bytes=_vmem_limit(blocks, resident + scratch + 5 * _nbytes((tm, tn), F32))),
        name="ffn_in_glu",
    )(h, h, h, w_gate, w_up, conv_w, conv_b.reshape(1, fp))


def _rw_mix_kernel(x_ref, xp_ref, xn_ref, g_ref, sh_ref, sc_ref, mix_ref, *rest, rows, ctx, total):
    outs, buf_ref = rest[:6], rest[6]
    row0 = pl.program_id(0) * rows
    g, sh, sc = g_ref[...], sh_ref[...], sc_ref[...]

    def nm(x, first_row):
        t = first_row + lax.broadcasted_iota(jnp.int32, (x.shape[0], 1), 0)
        return _norm_mod_rows(x, g, sh, sc, t < ctx)

    h = nm(x_ref[...], row0)
    hp = nm(xp_ref[...], row0 - V7X_SUBLANES)
    hn = nm(xn_ref[...], row0 + rows)
    before, after = _seq_neighbors(buf_ref, h, hp, hn, rows, row0, ctx, total)
    xx = 0.5 * (before + after) - h
    for n in range(6):
        outs[n][...] = (h + xx * mix_ref[n:n + 1, :]).astype(outs[n].dtype)


def rw_token_mix(x, g, shift, scale, mix, ctx):
    t, d = x.shape
    rows = _tile(t, (128, 64, 32, 16, 8))
    rb = rows // V7X_SUBLANES
    last8 = t // V7X_SUBLANES - 1
    blocks = _nbytes((rows + 16, d), F32) + 6 * _nbytes((rows, d), BF16) + 4 * _nbytes((8, d), F32)
    return pl.pallas_call(
        functools.partial(_rw_mix_kernel, rows=rows, ctx=ctx, total=t),
        out_shape=[jax.ShapeDtypeStruct((t, d), BF16)] * 6,
        grid=(t // rows,),
        in_specs=[
            pl.BlockSpec((rows, d), lambda i: (i, 0)),
            pl.BlockSpec((V7X_SUBLANES, d), lambda i: (jnp.maximum(i * rb - 1, 0), 0)),
            pl.BlockSpec((V7X_SUBLANES, d), lambda i: (jnp.minimum((i + 1) * rb, last8), 0)),
            pl.BlockSpec((1, d), lambda i: (0, 0)),
            pl.BlockSpec((2, d), lambda i: (0, 0)),
            pl.BlockSpec((2, d), lambda i: (0, 0)),
            pl.BlockSpec((6, d), lambda i: (0, 0)),
        ],
        out_specs=[pl.BlockSpec((rows, d), lambda i: (i, 0))] * 6,
        scratch_shapes=[pltpu.VMEM((rows + 2 * V7X_SUBLANES, d), F32)],
        compiler_params=pltpu.CompilerParams(
            dimension_semantics=("arbitrary",),
            vmem_limit_bytes=_vmem_limit(blocks, 6 * _nbytes((rows + 16, d), F32))),
        name="rw_token_mix",
    )(x, x, x, g.reshape(1, d), shift, scale, mix)


SCAN_STEPS = 16


def _mirror_block(i, n_blocks, n_ctx_blocks):
    return jnp.where(i < n_ctx_blocks, n_ctx_blocks - 1 - i, n_blocks + n_ctx_blocks - 1 - i)


def _scan_specs(t, chans, lanes, ctx):
    assert ctx % SCAN_STEPS == 0 and t % SCAN_STEPS == 0
    nb, nbc = t // SCAN_STEPS, ctx // SCAN_STEPS
    here = pl.BlockSpec((SCAN_STEPS, chans, lanes), lambda i: (i, 0, 0))
    mirror = pl.BlockSpec((SCAN_STEPS, chans, lanes), lambda i: (_mirror_block(i, nb, nbc), 0, 0))
    return nb, here, mirror


def _rw_prep_kernel(ra, rb, ka, kb, va, vb, wla, wlb, ala, alb, w0_ref, a0_ref, kk_ref, ka_ref, rk_ref,
                    w_o, a_o, b_o, kd_o, wr_o, v_o, bonus_o, br_o, kr_o):
    lanes = wla.shape[-1]
    is_fwd = lax.broadcasted_iota(jnp.int32, (1, lanes), 1) < lanes // 2

    def body(s, carry):
        sb = SCAN_STEPS - 1 - s

        def pick(xa, xb):
            return jnp.where(is_fwd, xa[s], xb[sb])

        def join(xa, xb):
            return jnp.concatenate([xa[s], xb[sb]], axis=-1)

        r, k, v = join(ra, rb), join(ka, kb), join(va, vb)
        kk = k * kk_ref[...]
        kk = kk * lax.rsqrt(jnp.maximum(jnp.sum(kk * kk, axis=0, keepdims=True), 1e-24))
        w = jnp.exp(-jnp.exp(-jax.nn.softplus(-(w0_ref[...] + pick(wla, wlb))) - 0.5))
        iclr = jax.nn.sigmoid(a0_ref[...] + pick(ala, alb))
        kd = k * (1.0 + (iclr - 1.0) * ka_ref[...])
        b = kk * iclr
        w_o[s] = w
        a_o[s] = -kk
        b_o[s] = b
        kd_o[s] = kd
        wr_o[s] = w * r
        v_o[s] = v
        br_o[s] = jnp.sum(b * r, axis=0, keepdims=True)
        kr_o[s] = jnp.sum(kd * r, axis=0, keepdims=True)
        bonus_o[s] = jnp.sum(r * kd * rk_ref[...], axis=0, keepdims=True) * v
        return carry

    lax.fori_loop(0, SCAN_STEPS, body, 0)


def rw_prepare(r, k, v, wl, al, w0, a0, k_k, k_a, r_k, ctx):
    t, chans, lanes = wl.shape
    nb, here, mirror = _scan_specs(t, chans, lanes, ctx)
    _, here_h, mirror_h = _scan_specs(t, chans, lanes // 2, ctx)
    par = pl.BlockSpec((chans, lanes), lambda i: (0, 0))
    small = pl.BlockSpec((SCAN_STEPS, 1, lanes), lambda i: (i, 0, 0))
    big = jax.ShapeDtypeStruct((t, chans, lanes), F32)
    row = jax.ShapeDtypeStruct((t, 1, lanes), F32)
    blocks = 17 * _nbytes((SCAN_STEPS, chans, lanes), F32) + 5 * _nbytes((chans, lanes), F32)
    return pl.pallas_call(
        _rw_prep_kernel,
        out_shape=[big] * 7 + [row] * 2,
        grid=(nb,),
        in_specs=[here_h, mirror_h] * 3 + [here, mirror] * 2 + [par] * 5,
        out_specs=[here] * 7 + [small] * 2,
        compiler_params=_params(("arbitrary",), blocks, 16 * _nbytes((chans, lanes), F32)),
        name="rw_prepare",
    )(r, r, k, k, v, v, wl, wl, al, al, w0, a0, k_k, k_a, r_k)


def _wkv_kernel(w_ref, a_ref, b_ref, kd_ref, wr_ref, v_ref, br_ref, kr_ref, y_ref, s_ref, *, steps, chans):
    @pl.when(pl.program_id(0) == 0)
    def _():
        s_ref[...] = jnp.zeros_like(s_ref)

    def step(i, carry):
        sa = jnp.zeros(s_ref.shape[1:], F32)
        u = jnp.zeros(s_ref.shape[1:], F32)
        for c in range(chans):
            sc = s_ref[c]
            sa = sa + sc * a_ref[i, c:c + 1, :]
            u = u + sc * wr_ref[i, c:c + 1, :]
        vv = v_ref[i]
        for c in range(chans):
            s_ref[c] = s_ref[c] * w_ref[i, c:c + 1, :] + sa * b_ref[i, c:c + 1, :] + vv * kd_ref[i, c:c + 1, :]
        y_ref[i] = u + sa * br_ref[i] + vv * kr_ref[i]
        return carry

    lax.fori_loop(0, steps, step, 0)


def wkv_scan(w, a, b, kd, wr, v, br, kr):
    t, chans, lanes = w.shape
    steps = SCAN_STEPS
    big = pl.BlockSpec((steps, chans, lanes), lambda i: (i, 0, 0))
    small = pl.BlockSpec((steps, 1, lanes), lambda i: (i, 0, 0))
    blocks = 7 * _nbytes((steps, chans, lanes), F32) + 2 * _nbytes((steps, 8, lanes), F32)
    state_bytes = _nbytes((chans, chans, lanes), F32)
    return pl.pallas_call(
        functools.partial(_wkv_kernel, steps=steps, chans=chans),
        out_shape=jax.ShapeDtypeStruct((t, chans, lanes), F32),
        grid=(t // steps,),
        in_specs=[big, big, big, big, big, big, small, small],
        out_specs=big,
        scratch_shapes=[pltpu.VMEM((chans, chans, lanes), F32)],
        compiler_params=pltpu.CompilerParams(
            dimension_semantics=("arbitrary",), vmem_limit_bytes=_vmem_limit(blocks, state_bytes)),
        name="wkv_scan",
    )(w, a, b, kd, wr, v, br, kr)


def _rw_finish_kernel(ya, yb, ba, bb, gate_ref, lnw_ref, lnb_ref, o_ref):
    half = ya.shape[-1] // 2

    def body(s, carry):
        sb = SCAN_STEPS - 1 - s
        y = (ya[s] + pltpu.roll(yb[sb], half, 1))[:, :half]
        bonus = (ba[s] + pltpu.roll(bb[sb], half, 1))[:, :half]
        mu = jnp.mean(y, axis=0, keepdims=True)
        yc = y - mu
        var = jnp.mean(yc * yc, axis=0, keepdims=True)
        z = yc * lax.rsqrt(var + RW_LN_EPS) * lnw_ref[...] + lnb_ref[...] + bonus
        o_ref[s] = (z * gate_ref[s]).astype(o_ref.dtype)
        return carry

    lax.fori_loop(0, SCAN_STEPS, body, 0)


def rw_finish(y, bonus, gate, ln_w, ln_b, ctx):
    t, chans, lanes = y.shape
    heads = lanes // 2
    nb, here, mirror = _scan_specs(t, chans, lanes, ctx)
    half = pl.BlockSpec((SCAN_STEPS, chans, heads), lambda i: (i, 0, 0))
    par = pl.BlockSpec((chans, heads), lambda i: (0, 0))
    blocks = 6 * _nbytes((SCAN_STEPS, chans, lanes), F32) + 2 * _nbytes((chans, lanes), F32)
    return pl.pallas_call(
        _rw_finish_kernel,
        out_shape=jax.ShapeDtypeStruct((t, chans, heads), BF16),
        grid=(nb,),
        in_specs=[here, mirror, here, mirror, half, par, par],
        out_specs=half,
        compiler_params=_params(("arbitrary",), blocks, 8 * _nbytes((chans, lanes), F32)),
        name="rw_finish",
    )(y, y, bonus, bonus, gate, ln_w, ln_b)


def _rope_tables(ctx, seq):
    pos = jnp.arange(seq)
    row = (pos // GRID_W).astype(F32)
    col = (pos % GRID_W).astype(F32)
    n_freq = HEAD_DIM // 4
    inv_freq = ROPE_THETA ** (-jnp.arange(n_freq, dtype=F32) / n_freq)
    ang = jnp.concatenate([row[:, None] * inv_freq, col[:, None] * inv_freq], axis=-1)
    cos = jnp.concatenate([jnp.ones((ctx, HEAD_DIM // 2), F32), jnp.cos(ang)], axis=0)
    sin = jnp.concatenate([jnp.zeros((ctx, HEAD_DIM // 2), F32), jnp.sin(ang)], axis=0)
    return jnp.concatenate([cos, cos], axis=-1), jnp.concatenate([-sin, sin], axis=-1)


def _channel_major(w):
    heads = w.shape[-1] // RW_HEAD_DIM
    return jnp.swapaxes(w.reshape(w.shape[:-1] + (heads, RW_HEAD_DIM)), -1, -2)


def _per_direction(w2):
    cm = _channel_major(w2)
    zero = jnp.zeros_like(cm[0])
    top = jnp.concatenate([cm[0], zero], axis=-1)
    bottom = jnp.concatenate([zero, cm[1]], axis=-1)
    return jnp.concatenate([top, bottom], axis=0).reshape(2 * w2.shape[1], -1)


def _score_gain(q_g, k_g):
    q_gain = q_g * (HEAD_DIM ** -0.5 * math.log2(math.e))
    bound = HEAD_DIM * jnp.max(jnp.abs(q_gain)) * jnp.max(jnp.abs(k_g))
    return q_gain, bound


def _diff_lambda_init(layer_idx):
    return 0.8 - 0.6 * math.exp(-0.3 * layer_idx)


def _diff_attention(h, cc, ss, wqkv, wo, q_g, k_g, lam_vecs, subln_g, lam_init, x, gate, ctx):
    d = h.shape[1]
    n_heads = d // HEAD_DIM
    q_gain, bound = _score_gain(q_g, k_g)
    gain = jnp.concatenate([jnp.tile(q_gain, n_heads), jnp.tile(k_g, n_heads)])
    qk = matmul_headnorm_rope(h, wqkv[:, :2 * d].astype(BF16), gain, cc, ss, name="da_qk_proj")
    v = matmul(h, wqkv[:, 2 * d:].astype(BF16), BF16, name="da_v_proj")
    o = flash_attention(qk, qk[:, d:].T, v, ctx, bound, groups=n_heads // 2, n_sub=2, per_sub_k=True, lam=lam_vecs,
                        subln=subln_g, lam_init=lam_init, name="da_flash")
    return matmul_residual(o, wo.astype(BF16), x, gate, ctx, name="da_out_proj")


def _gqa_attention(h, cc, ss, wqkv, wo, q_g, k_g, x, gate, ctx):
    d = h.shape[1]
    kv_w = d // GA_GROUP
    q_gain, bound = _score_gain(q_g, k_g)
    gain = jnp.concatenate([jnp.tile(q_gain, d // HEAD_DIM), jnp.tile(k_g, kv_w // HEAD_DIM)])
    qk = matmul_headnorm_rope(h, wqkv[:, :d + kv_w].astype(BF16), gain, cc, ss, name="ga_qk_proj")
    v = matmul(h, wqkv[:, d + kv_w:].astype(BF16), BF16, name="ga_v_proj")
    o = flash_attention(qk, qk[:, d:].T, v, ctx, bound, groups=kv_w // HEAD_DIM, n_sub=GA_GROUP, per_sub_k=False,
                        name="ga_flash")
    return matmul_residual(o, wo.astype(BF16), x, gate, ctx, name="ga_out_proj")


def _pad_cols(w, n):
    return jnp.pad(w, ((0, 0), (0, n - w.shape[1])))


def _rwkv7(x, g, shift, scale, mix, wrkv, wo, w0, w1, w2, a0, a1, a2, g1, g2, k_k, k_a, r_k, ln_w, ln_b, gate,
           ctx):
    t, d = x.shape
    heads = d // RW_HEAD_DIM
    scan_shape = (t, RW_HEAD_DIM, 2 * heads)
    xr, xw, xk, xv, xa, xg = rw_token_mix(x, g, shift, scale, mix, ctx)

    def shared(xin, w, name):
        w_cm = _channel_major(w).reshape(w.shape)
        return matmul(xin, w_cm.astype(BF16), F32, name=name).reshape(t, RW_HEAD_DIM, heads)

    r = shared(xr, wrkv[0], "rw_r_proj")
    k = shared(xk, wrkv[1], "rw_k_proj")
    v = shared(xv, wrkv[2], "rw_v_proj")
    hw = matmul(xw, jnp.concatenate([w1[0], w1[1]], axis=1).astype(BF16), BF16, act="tanh", name="rw_w_lora_in")
    ha = matmul(xa, jnp.concatenate([a1[0], a1[1]], axis=1).astype(BF16), BF16, name="rw_a_lora_in")
    wl = matmul(hw, _per_direction(w2).astype(BF16), F32, name="rw_w_lora_out").reshape(scan_shape)
    al = matmul(ha, _per_direction(a2).astype(BF16), F32, name="rw_a_lora_out").reshape(scan_shape)
    lg = -(-g1.shape[1] // V7X_LANES) * V7X_LANES
    hg = matmul(xg, _pad_cols(g1, lg).astype(BF16), BF16, act="sigmoid", name="rw_g_lora_in")
    g2p = jnp.pad(_channel_major(g2).reshape(g2.shape), ((0, lg - g2.shape[0]), (0, 0))).astype(BF16)
    out_gate = matmul(hg, g2p, F32, name="rw_g_lora_out").reshape(t, RW_HEAD_DIM, heads)

    def dir_param(p):
        return jnp.concatenate([_channel_major(p[0]), _channel_major(p[1])], axis=-1)

    def shared_param(p):
        return jnp.concatenate([_channel_major(p)] * 2, axis=-1)

    w, a, b, kd, wr, vs, bonus, br, kr = rw_prepare(
        r, k, v, wl, al, dir_param(w0), dir_param(a0), shared_param(k_k), shared_param(k_a),
        shared_param(r_k.reshape(d)), ctx)
    y = wkv_scan(w, a, b, kd, wr, vs, br, kr)
    z = rw_finish(y, bonus, out_gate, _channel_major(ln_w), _channel_major(ln_b), ctx).reshape(t, d)
    wo_cm = jnp.swapaxes(wo.reshape(heads, RW_HEAD_DIM, d), 0, 1).reshape(d, d)
    return matmul_residual(z, wo_cm.astype(BF16), x, gate, ctx, name="rw_out_proj")


def _conv_glu(h, w_in, conv_w, conv_b, w_out, x, gate, ctx):
    f = conv_w.shape[1]
    fp = -(-f // FFN_PAD) * FFN_PAD
    act = ffn_in_glu(h, _pad_cols(w_in[:, :f], fp).astype(BF16), _pad_cols(w_in[:, f:], fp).astype(BF16),
                     _pad_cols(conv_w, fp), jnp.pad(conv_b, (0, fp - f)), ctx)
    w_out_p = jnp.pad(w_out, ((0, fp - f), (0, 0))).astype(BF16)
    return matmul_residual(act, w_out_p, x, gate, ctx, name="ffn_out_proj")


def kernel(x, c, ctx, c_ctx, ada_down, ada_up, ada_b, norm_g, ffn_in, ffn_conv, ffn_conv_b, ffn_out, da_wqkv, da_wo, da_q_g, da_k_g, da_lambda, da_subln_g, rw_mix, rw_wrkv, rw_wo, rw_w0, rw_w1, rw_w2, rw_a0, rw_a1, rw_a2, rw_g1, rw_g2, rw_k_k, rw_k_a, rw_r_k, rw_ln_w, rw_ln_b, ga_wqkv, ga_wo, ga_q_g, ga_k_g):
    assert x.shape[0] == 1
    seq, d = x.shape[1], x.shape[2]
    n_ctx = ctx.shape[1]
    depth = ada_down.shape[0]
    stream = jnp.concatenate([ctx[0], x[0]], axis=0)
    cond = jnp.zeros((V7X_SUBLANES, d), F32).at[0].set(c_ctx).at[1].set(c[0])
    mod = ada_modulation(cond, ada_down, ada_up, ada_b)[:, :2].reshape(depth, 2, N_MOD, d)
    cc, ss = _rope_tables(n_ctx, seq)
    for i in range(depth):
        kind, j = i % N_MIXERS, i // N_MIXERS
        m = [mod[i, :, n, :] for n in range(N_MOD)]
        if kind == 1:
            stream = _rwkv7(stream, norm_g[i, 0], m[0], m[1], rw_mix[j], rw_wrkv[j], rw_wo[j], rw_w0[j], rw_w1[j],
                            rw_w2[j], rw_a0[j], rw_a1[j], rw_a2[j], rw_g1[j], rw_g2[j], rw_k_k[j], rw_k_a[j],
                            rw_r_k[j], rw_ln_w[j], rw_ln_b[j], m[2], n_ctx)
        else:
            h = norm_mod(stream, norm_g[i, 0], m[0], m[1], n_ctx)
            if kind == 0:
                stream = _diff_attention(h, cc, ss, da_wqkv[j], da_wo[j], da_q_g[j], da_k_g[j], da_lambda[j],
                                         da_subln_g[j], _diff_lambda_init(i), stream, m[2], n_ctx)
            else:
                stream = _gqa_attention(h, cc, ss, ga_wqkv[j], ga_wo[j], ga_q_g[j], ga_k_g[j], stream, m[2], n_ctx)
        h2 = norm_mod(stream, norm_g[i, 1], m[3], m[4], n_ctx)
        stream = _conv_glu(h2, ffn_in[i], ffn_conv[i], ffn_conv_b[i], ffn_out[i], stream, m[5], n_ctx)
    return stream[n_ctx:][None]
```

```python
import functools
import math

import jax
import jax.numpy as jnp
from jax import lax
from jax.experimental import pallas as pl
from jax.experimental.pallas import tpu as pltpu

F32 = jnp.float32
BF16 = jnp.bfloat16

N_MOD = 6
N_MIXERS = 3
GRID_W = 64
ROPE_THETA = 10000.0
NORM_EPS = 1e-6
HEAD_DIM = 128
GA_GROUP = 4
RW_HEAD_DIM = 64
RW_LN_EPS = 64e-5

V7X_LANES = 128
V7X_SUBLANES = 8
V7X_MXU_WIDTH = 256
V7X_VMEM_BYTES = 64 * 1024 * 1024
V7X_VMEM_RESERVE = 6 * 1024 * 1024
FFN_PAD = 1024


def _vmem_limit(block_bytes, temp_bytes=0):
    return int(min(2 * block_bytes + temp_bytes + V7X_VMEM_RESERVE, V7X_VMEM_BYTES - V7X_VMEM_RESERVE))


def _params(semantics, block_bytes, temp_bytes=0):
    return pltpu.CompilerParams(dimension_semantics=semantics,
                                vmem_limit_bytes=_vmem_limit(block_bytes, temp_bytes))


def _tile(n, prefs):
    for t in prefs:
        if n % t == 0:
            return t
    return n


def _nbytes(shape, dtype):
    return math.prod(shape) * jnp.dtype(dtype).itemsize


def _row_ids(i, rows, shape):
    return i * rows + lax.broadcasted_iota(jnp.int32, shape, 0)


def _ada_kernel(cond_ref, down_ref, up_ref, b_ref, o_ref):
    s = jax.nn.silu(cond_ref[...])
    t = jnp.dot(s, down_ref[...], preferred_element_type=F32, precision=lax.Precision.HIGHEST)
    m = jnp.dot(t, up_ref[...], preferred_element_type=F32, precision=lax.Precision.HIGHEST)
    o_ref[...] = m + b_ref[...]


def ada_modulation(cond, ada_down, ada_up, ada_b):
    depth, d, r = ada_down.shape
    n = ada_up.shape[-1]
    tn = _tile(n, (4096, 2048, 1024, 512))
    blocks = _nbytes((8, d), F32) + _nbytes((d, r), F32) + _nbytes((r, tn), F32) + 2 * _nbytes((8, tn), F32)
    return pl.pallas_call(
        _ada_kernel,
        out_shape=jax.ShapeDtypeStruct((depth, 8, n), F32),
        grid=(depth, n // tn),
        in_specs=[
            pl.BlockSpec((8, d), lambda l, j: (0, 0)),
            pl.BlockSpec((None, d, r), lambda l, j: (l, 0, 0)),
            pl.BlockSpec((None, r, tn), lambda l, j: (l, 0, j)),
            pl.BlockSpec((None, 1, tn), lambda l, j: (l, 0, j)),
        ],
        out_specs=pl.BlockSpec((None, 8, tn), lambda l, j: (l, 0, j)),
        compiler_params=_params(("arbitrary", "arbitrary"), blocks),
        name="ada_modulation",
    )(cond, ada_down, ada_up, ada_b.reshape(depth, 1, n))


def _norm_mod_rows(x, g, shift, scale, is_ctx):
    y = x * lax.rsqrt(jnp.mean(x * x, axis=-1, keepdims=True) + NORM_EPS) * g
    sc = jnp.where(is_ctx, scale[0:1, :], scale[1:2, :])
    sh = jnp.where(is_ctx, shift[0:1, :], shift[1:2, :])
    return y * (1.0 + sc) + sh


def _norm_mod_kernel(x_ref, g_ref, sh_ref, sc_ref, o_ref, *, rows, ctx):
    is_ctx = _row_ids(pl.program_id(0), rows, (rows, 1)) < ctx
    o_ref[...] = _norm_mod_rows(x_ref[...], g_ref[...], sh_ref[...], sc_ref[...], is_ctx).astype(o_ref.dtype)


def norm_mod(x, g, shift, scale, ctx):
    t, d = x.shape
    rows = _tile(t, (256, 128, 64, 32, 16, 8))
    blocks = _nbytes((rows, d), F32) + _nbytes((rows, d), BF16) + 5 * _nbytes((8, d), F32)
    return pl.pallas_call(
        functools.partial(_norm_mod_kernel, rows=rows, ctx=ctx),
        out_shape=jax.ShapeDtypeStruct((t, d), BF16),
        grid=(t // rows,),
        in_specs=[
            pl.BlockSpec((rows, d), lambda i: (i, 0)),
            pl.BlockSpec((1, d), lambda i: (0, 0)),
            pl.BlockSpec((2, d), lambda i: (0, 0)),
            pl.BlockSpec((2, d), lambda i: (0, 0)),
        ],
        out_specs=pl.BlockSpec((rows, d), lambda i: (i, 0)),
        compiler_params=_params(("arbitrary",), blocks, 4 * _nbytes((rows, d), F32)),
        name="norm_mod",
    )(x, g.reshape(1, d), shift, scale)


def _mm_accumulate(a_ref, b_ref, acc_ref, nk, finish):
    if nk == 1:
        finish(jnp.dot(a_ref[...], b_ref[...], preferred_element_type=F32))
        return
    k = pl.program_id(2)

    @pl.when(k == 0)
    def _():
        acc_ref[...] = jnp.zeros_like(acc_ref)

    acc_ref[...] += jnp.dot(a_ref[...], b_ref[...], preferred_element_type=F32)

    @pl.when(k == nk - 1)
    def _():
        finish(acc_ref[...])


def _mm_plain_kernel(a_ref, b_ref, o_ref, *scratch, nk, act):
    def finish(acc):
        if act == "tanh":
            acc = jnp.tanh(acc)
        elif act == "sigmoid":
            acc = jax.nn.sigmoid(acc)
        o_ref[...] = acc.astype(o_ref.dtype)

    _mm_accumulate(a_ref, b_ref, scratch[0] if scratch else None, nk, finish)


def _mm_residual_kernel(a_ref, b_ref, x_ref, g_ref, o_ref, *scratch, nk, tm, ctx):
    def finish(acc):
        is_ctx = _row_ids(pl.program_id(1), tm, (tm, 1)) < ctx
        gate = jnp.where(is_ctx, g_ref[0:1, :], g_ref[1:2, :])
        o_ref[...] = x_ref[...] + gate * acc

    _mm_accumulate(a_ref, b_ref, scratch[0] if scratch else None, nk, finish)


def _mm_headnorm_rope_kernel(a_ref, b_ref, g_ref, cc_ref, ss_ref, o_ref, *, tn):
    cc = cc_ref[...]
    ss = ss_ref[...]
    gw = V7X_MXU_WIDTH if tn % V7X_MXU_WIDTH == 0 else HEAD_DIM
    for c in range(tn // gw):
        acc = jnp.dot(a_ref[...], b_ref[:, c * gw:(c + 1) * gw], preferred_element_type=F32)
        for h in range(gw // HEAD_DIM):
            sl = slice(c * gw + h * HEAD_DIM, c * gw + (h + 1) * HEAD_DIM)
            z = acc[:, h * HEAD_DIM:(h + 1) * HEAD_DIM]
            y = z * lax.rsqrt(jnp.mean(z * z, axis=-1, keepdims=True) + NORM_EPS) * g_ref[:, sl]
            o_ref[:, sl] = (y * cc + pltpu.roll(y, HEAD_DIM // 2, 1) * ss).astype(o_ref.dtype)


def _mm_tiles(m, k, n):
    tm = _tile(m, (640, 512, 384, 256, 128, 64, 32, 16, 8))
    tn = _tile(n, (1024, 512, 256, 128))
    tk = k if k <= 4096 else _tile(k, (2816, 2048, 1024, 512))
    return tm, tn, tk


def _mm_call(kernel, a, b, extra_inputs, extra_specs, out_dtype, extra_bytes, name):
    m, k = a.shape
    n = b.shape[1]
    tm, tn, tk = _mm_tiles(m, k, n)
    nk = k // tk
    blocks = (_nbytes((tm, tk), a.dtype) + _nbytes((tk, tn), b.dtype) + _nbytes((tm, tn), out_dtype)
              + extra_bytes(tm, tn))
    scratch = [] if nk == 1 else [pltpu.VMEM((tm, tn), F32)]
    scratch_bytes = 0 if nk == 1 else _nbytes((tm, tn), F32)
    grid = (n // tn, m // tm) + (() if nk == 1 else (nk,))
    if nk == 1:
        a_map, b_map, o_map = (lambda j, i: (i, 0)), (lambda j, i: (0, j)), (lambda j, i: (i, j))
    else:
        a_map, b_map, o_map = (lambda j, i, kk: (i, kk)), (lambda j, i, kk: (kk, j)), (lambda j, i, kk: (i, j))
    return pl.pallas_call(
        functools.partial(kernel, nk=nk),
        out_shape=jax.ShapeDtypeStruct((m, n), out_dtype),
        grid=grid,
        in_specs=[pl.BlockSpec((tm, tk), a_map), pl.BlockSpec((tk, tn), b_map)] + extra_specs(tm, tn, nk),
        out_specs=pl.BlockSpec((tm, tn), o_map),
        scratch_shapes=scratch,
        compiler_params=pltpu.CompilerParams(
            dimension_semantics=("arbitrary",) * len(grid),
            vmem_limit_bytes=_vmem_limit(blocks, scratch_bytes + 2 * _nbytes((tm, tn), F32))),
        name=name,
    )(a, b, *extra_inputs)


def matmul(a, b, out_dtype, act=None, name="matmul"):
    return _mm_call(functools.partial(_mm_plain_kernel, act=act), a, b, (), lambda tm, tn, nk: [],
                    out_dtype, lambda tm, tn: 0, name)


def matmul_residual(a, b, x, gate, ctx, name="matmul_residual"):
    tm = _mm_tiles(a.shape[0], a.shape[1], b.shape[1])[0]

    def specs(tm_, tn, nk):
        if nk == 1:
            return [pl.BlockSpec((tm_, tn), lambda j, i: (i, j)), pl.BlockSpec((2, tn), lambda j, i: (0, j))]
        return [pl.BlockSpec((tm_, tn), lambda j, i, kk: (i, j)), pl.BlockSpec((2, tn), lambda j, i, kk: (0, j))]

    return _mm_call(functools.partial(_mm_residual_kernel, tm=tm, ctx=ctx), a, b, (x, gate), specs, F32,
                    lambda tm_, tn: _nbytes((tm_, tn), F32) + _nbytes((8, tn), F32), name)


def matmul_headnorm_rope(a, b, gain, cc, ss, name="matmul_headnorm_rope"):
    m, k = a.shape
    n = b.shape[1]
    tm, tn, tk = _mm_tiles(m, k, n)
    assert tk == k
    blocks = (_nbytes((tm, k), a.dtype) + _nbytes((k, tn), b.dtype) + _nbytes((tm, tn), BF16)
              + _nbytes((8, tn), F32) + 2 * _nbytes((tm, HEAD_DIM), F32) + _nbytes((tm, tn), F32))
    return pl.pallas_call(
        functools.partial(_mm_headnorm_rope_kernel, tn=tn),
        out_shape=jax.ShapeDtypeStruct((m, n), BF16),
        grid=(n // tn, m // tm),
        in_specs=[
            pl.BlockSpec((tm, k), lambda j, i: (i, 0)),
            pl.BlockSpec((k, tn), lambda j, i: (0, j)),
            pl.BlockSpec((1, tn), lambda j, i: (0, j)),
            pl.BlockSpec((tm, HEAD_DIM), lambda j, i: (i, 0)),
            pl.BlockSpec((tm, HEAD_DIM), lambda j, i: (i, 0)),
        ],
        out_specs=pl.BlockSpec((tm, tn), lambda j, i: (i, j)),
        compiler_params=_params(("arbitrary", "arbitrary"), blocks, 2 * _nbytes((tm, tn), F32)),
        name=name,
    )(a, b, gain.reshape(1, n), cc, ss)


SCORE_BOUND_DIRECT = 64.0


def _flash_finalize(o_ref, l_rows, acc_ref, lam_ref, subln_ref, *, n_sub, lam_init):
    if lam_init is None:
        for s in range(n_sub):
            o_ref[:, s * HEAD_DIM:(s + 1) * HEAD_DIM] = (acc_ref[s] / l_rows[s]).astype(o_ref.dtype)
        return
    lv = lam_ref[...]
    lam = (jnp.exp(jnp.sum(lv[0:1, :] * lv[1:2, :], axis=-1, keepdims=True))
           - jnp.exp(jnp.sum(lv[2:3, :] * lv[3:4, :], axis=-1, keepdims=True)) + lam_init)
    o = acc_ref[0] / l_rows[0] - lam * (acc_ref[1] / l_rows[1])
    o = o * lax.rsqrt(jnp.mean(o * o, axis=-1, keepdims=True) + NORM_EPS) * subln_ref[...]
    o_ref[...] = (o * (1.0 - lam_init)).astype(o_ref.dtype)


def _flash_direct_kernel(q_ref, kt_ref, v_ref, lam_ref, subln_ref, o_ref, l_ref, acc_ref, *, n_sub, tq, tc, ctx,
                         n_chunks, lam_init):
    l_ref[...] = jnp.zeros_like(l_ref)
    acc_ref[...] = jnp.zeros_like(acc_ref)

    def accumulate(chunk, width, v, r0, nr):
        for s in range(n_sub):
            cols = slice(s * HEAD_DIM, (s + 1) * HEAD_DIM)
            p = jnp.exp2(jnp.dot(q_ref[r0:r0 + nr, cols], kt_ref[chunk, cols, 0:width], preferred_element_type=F32))
            part = p[:, 0:V7X_LANES]
            for j in range(1, width // V7X_LANES):
                part = part + p[:, j * V7X_LANES:(j + 1) * V7X_LANES]
            l_ref[s, r0:r0 + nr, :] += part
            acc_ref[s, r0:r0 + nr, :] += jnp.dot(p.astype(v.dtype), v, preferred_element_type=F32)

    def all_keys(r0, nr):
        def body(c, carry):
            accumulate(c, tc, v_ref[pl.ds(pl.multiple_of(c * tc, tc), tc), :], r0, nr)
            return carry

        lax.fori_loop(0, n_chunks, body, 0)

    @pl.when(pl.program_id(1) > 0)
    def _():
        all_keys(0, tq)

    @pl.when(pl.program_id(1) == 0)
    def _():
        accumulate(0, ctx, v_ref[0:ctx, :], 0, ctx)
        if ctx < tq:
            all_keys(ctx, tq - ctx)

    l_rows = [jnp.sum(l_ref[s], axis=-1, keepdims=True) for s in range(n_sub)]
    _flash_finalize(o_ref, l_rows, acc_ref, lam_ref, subln_ref, n_sub=n_sub, lam_init=lam_init)


def _flash_direct_shared_kernel(qt_ref, k_ref, vt_ref, ot_ref, qs_ref, l_ref, acc_ref, *, n_sub, tq, tc, ctx, n_chunks):
    for g in range(n_sub):
        qs_ref[:, g * tq:(g + 1) * tq] = qt_ref[g * HEAD_DIM:(g + 1) * HEAD_DIM, :]
    l_ref[...] = jnp.zeros_like(l_ref)
    acc_ref[...] = jnp.zeros_like(acc_ref)

    def accumulate(k, vt):
        pt = jnp.exp2(jnp.dot(k, qs_ref[...], preferred_element_type=F32))
        l_ref[...] += jnp.sum(pt, axis=0, keepdims=True)
        acc_ref[...] += jnp.dot(vt, pt.astype(vt.dtype), preferred_element_type=F32)

    @pl.when(pl.program_id(1) > 0)
    def _():
        def body(c, carry):
            accumulate(k_ref[pl.ds(pl.multiple_of(c * tc, tc), tc), :], vt_ref[c])
            return carry

        lax.fori_loop(0, n_chunks, body, 0)

    @pl.when(pl.program_id(1) == 0)
    def _():
        accumulate(k_ref[0:ctx, :], vt_ref[0, :, 0:ctx])

    out = acc_ref[...] / l_ref[...]
    for g in range(n_sub):
        ot_ref[g * HEAD_DIM:(g + 1) * HEAD_DIM, :] = out[:, g * tq:(g + 1) * tq].astype(ot_ref.dtype)


def _flash_online_step(q_ref, kt_ref, v_ref, m_ref, l_ref, acc_ref, *, n_sub, per_sub_k, tk, ctx, masked):
    ki = pl.program_id(2)
    v = v_ref[...]
    for s in range(n_sub):
        q = q_ref[:, s * HEAD_DIM:(s + 1) * HEAD_DIM]
        kt = kt_ref[s * HEAD_DIM:(s + 1) * HEAD_DIM, :] if per_sub_k else kt_ref[...]
        sc = jnp.dot(q, kt, preferred_element_type=F32)
        if masked:
            col = ki * tk + lax.broadcasted_iota(jnp.int32, sc.shape, 1)
            sc = jnp.where(col < ctx, sc, -jnp.inf)
        m_prev = m_ref[s]
        m_new = jnp.maximum(m_prev, jnp.max(sc, axis=-1, keepdims=True))
        alpha = jnp.exp2(m_prev - m_new)
        p = jnp.exp2(sc - m_new)
        l_ref[s] = alpha * l_ref[s] + jnp.sum(p, axis=-1, keepdims=True)
        acc_ref[s] = alpha * acc_ref[s] + jnp.dot(p.astype(v.dtype), v, preferred_element_type=F32)
        m_ref[s] = m_new


def _flash_online_kernel(q_ref, kt_ref, v_ref, *rest, n_sub, per_sub_k, tq, tk, ctx, nkv, lam_init):
    if lam_init is None:
        lam_ref = subln_ref = None
        o_ref, m_ref, l_ref, acc_ref = rest
    else:
        lam_ref, subln_ref, o_ref, m_ref, l_ref, acc_ref = rest
    qi = pl.program_id(1)
    ki = pl.program_id(2)
    q_is_ctx = (qi + 1) * tq <= ctx
    step = functools.partial(_flash_online_step, q_ref, kt_ref, v_ref, m_ref, l_ref, acc_ref,
                             n_sub=n_sub, per_sub_k=per_sub_k, tk=tk, ctx=ctx)

    @pl.when(ki == 0)
    def _():
        m_ref[...] = jnp.full(m_ref.shape, -jnp.inf, F32)
        l_ref[...] = jnp.zeros_like(l_ref)
        acc_ref[...] = jnp.zeros_like(acc_ref)

    @pl.when(jnp.logical_not(q_is_ctx))
    def _():
        step(masked=False)

    @pl.when(jnp.logical_and(q_is_ctx, ki * tk < ctx))
    def _():
        step(masked=True)

    @pl.when(ki == nkv - 1)
    def _():
        _flash_finalize(o_ref, [l_ref[s] for s in range(n_sub)], acc_ref, lam_ref, subln_ref, n_sub=n_sub,
                        lam_init=lam_init)


def flash_attention(qk, v, ctx, score_bound, *, groups, n_sub, per_sub_k, lam=None, subln=None, lam_init=None,
                    name="flash"):
    t = qk.shape[0]
    g = groups
    qw = n_sub * HEAD_DIM
    kw = qk.shape[1] // g - qw
    dv = v.shape[1] // g
    ow = dv if lam_init is not None else qw
    tq = _tile(ctx, (256, 128, 64, 32, 16, 8))
    tk = _tile(t, (1280, 1024, 768, 512, 256, 128))
    assert t % tq == 0 and ctx % tq == 0 and ctx <= tk and ctx % V7X_LANES == 0
    assert per_sub_k == (lam_init is not None)
    nkv = t // tk
    n_ctx_q = ctx // tq
    extra_specs3 = extra_specs2 = []
    extra_inputs = []
    if lam_init is not None:
        extra_specs3 = [pl.BlockSpec(lam.shape, lambda h, qi, ki: (0, 0)),
                        pl.BlockSpec((1, dv), lambda h, qi, ki: (0, 0))]
        extra_specs2 = [pl.BlockSpec(lam.shape, lambda h, qi: (0, 0)), pl.BlockSpec((1, dv), lambda h, qi: (0, 0))]
        extra_inputs = [lam, subln.reshape(1, dv)]
    out_shape = jax.ShapeDtypeStruct((t, g * ow), BF16)
    tile_bytes = _nbytes((tq, qw), BF16) + _nbytes((tq, ow), BF16)
    temp_bytes = 4 * _nbytes((tq, tk), F32) + n_sub * _nbytes((tq, dv), F32)
    once = pl.Buffered(1)

    def keys_t(qk):
        return qk[:, g * qw:].T

    def direct_per_sub_k(qk, v):
        tqd = next(c for c in (1280, 768, 512, 256, tq) if t % c == 0 and c % ctx == 0)
        kt_chunks = keys_t(qk).reshape(g, kw, nkv, tk).transpose(0, 2, 1, 3)
        blocks = _nbytes((tqd, qw), BF16) + _nbytes((tqd, ow), BF16)
        resident = _nbytes((kw, t), BF16) + _nbytes((t, dv), BF16)
        scratch_bytes = n_sub * (_nbytes((tqd, V7X_LANES), F32) + _nbytes((tqd, dv), F32))
        return pl.pallas_call(
            functools.partial(_flash_direct_kernel, n_sub=n_sub, tq=tqd, tc=tk, ctx=ctx, n_chunks=nkv,
                              lam_init=lam_init),
            out_shape=out_shape,
            grid=(g, t // tqd),
            in_specs=[pl.BlockSpec((tqd, qw), lambda h, qi: (qi, h)),
                      pl.BlockSpec((None, nkv, kw, tk), lambda h, qi: (h, 0, 0, 0), pipeline_mode=once),
                      pl.BlockSpec((t, dv), lambda h, qi: (0, h), pipeline_mode=once)] + extra_specs2,
            out_specs=pl.BlockSpec((tqd, ow), lambda h, qi: (qi, h)),
            scratch_shapes=[pltpu.VMEM((n_sub, tqd, V7X_LANES), F32), pltpu.VMEM((n_sub, tqd, dv), F32)],
            compiler_params=pltpu.CompilerParams(
                dimension_semantics=("arbitrary", "arbitrary"),
                vmem_limit_bytes=_vmem_limit(
                    blocks, resident + scratch_bytes + 3 * _nbytes((tqd, tk), F32) + 3 * _nbytes((tqd, dv), F32))),
            name=name + "_direct",
        )(qk, kt_chunks, v, *extra_inputs)

    def direct_shared_k(qk, v):
        cols = n_sub * tq
        qt = qk[:, :g * qw].T
        vt_chunks = v.T.reshape(g, dv, nkv, tk).transpose(0, 2, 1, 3)
        k_col0 = g * qw // kw
        blocks = 2 * _nbytes((qw, tq), BF16)
        resident = _nbytes((t, kw), BF16) + _nbytes((dv, t), BF16)
        scratch_bytes = _nbytes((HEAD_DIM, cols), BF16) + _nbytes((8, cols), F32) + _nbytes((dv, cols), F32)
        out_t = pl.pallas_call(
            functools.partial(_flash_direct_shared_kernel, n_sub=n_sub, tq=tq, tc=tk, ctx=ctx, n_chunks=nkv),
            out_shape=jax.ShapeDtypeStruct((g * qw, t), BF16),
            grid=(g, t // tq),
            in_specs=[pl.BlockSpec((qw, tq), lambda h, qi: (h, qi)),
                      pl.BlockSpec((t, kw), lambda h, qi: (0, k_col0 + h), pipeline_mode=once),
                      pl.BlockSpec((None, nkv, dv, tk), lambda h, qi: (h, 0, 0, 0), pipeline_mode=once)],
            out_specs=pl.BlockSpec((qw, tq), lambda h, qi: (h, qi)),
            scratch_shapes=[pltpu.VMEM((HEAD_DIM, cols), BF16), pltpu.VMEM((1, cols), F32),
                            pltpu.VMEM((dv, cols), F32)],
            compiler_params=pltpu.CompilerParams(
                dimension_semantics=("arbitrary", "arbitrary"),
                vmem_limit_bytes=_vmem_limit(blocks, resident + scratch_bytes + 3 * _nbytes((tk, cols), F32))),
            name=name + "_direct",
        )(qt, qk, vt_chunks)
        return out_t.T

    def online(qk, v):
        kt = keys_t(qk)
        q = qk
        last_ctx_kv = (ctx - 1) // tk

        def kv_index(qi, ki):
            return jnp.where(qi < n_ctx_q, jnp.minimum(ki, last_ctx_kv), ki)

        blocks = tile_bytes + _nbytes((kw, tk), BF16) + _nbytes((tk, dv), BF16)
        return pl.pallas_call(
            functools.partial(_flash_online_kernel, n_sub=n_sub, per_sub_k=per_sub_k, tq=tq, tk=tk, ctx=ctx, nkv=nkv,
                              lam_init=lam_init),
            out_shape=out_shape,
            grid=(g, t // tq, nkv),
            in_specs=[pl.BlockSpec((tq, qw), lambda h, qi, ki: (qi, h)),
                      pl.BlockSpec((kw, tk), lambda h, qi, ki: (h, kv_index(qi, ki))),
                      pl.BlockSpec((tk, dv), lambda h, qi, ki: (kv_index(qi, ki), h))] + extra_specs3,
            out_specs=pl.BlockSpec((tq, ow), lambda h, qi, ki: (qi, h)),
            scratch_shapes=[pltpu.VMEM((n_sub, tq, 1), F32), pltpu.VMEM((n_sub, tq, 1), F32),
                            pltpu.VMEM((n_sub, tq, dv), F32)],
            compiler_params=pltpu.CompilerParams(
                dimension_semantics=("arbitrary", "arbitrary", "arbitrary"),
                vmem_limit_bytes=_vmem_limit(blocks, temp_bytes + 2 * n_sub * _nbytes((tq, V7X_LANES), F32))),
            name=name + "_online",
        )(q, kt, v, *extra_inputs)

    direct = direct_per_sub_k if per_sub_k else direct_shared_k
    return lax.cond(score_bound <= SCORE_BOUND_DIRECT, direct, online, qk, v)


def _seq_neighbors(buf_ref, cur, prev8, next8, rows, row0, ctx, total):
    buf_ref[0:V7X_SUBLANES, :] = prev8
    buf_ref[V7X_SUBLANES:V7X_SUBLANES + rows, :] = cur
    buf_ref[V7X_SUBLANES + rows:, :] = next8
    t = row0 + lax.broadcasted_iota(jnp.int32, (rows, 1), 0)
    has_prev = jnp.logical_and(t != 0, t != ctx)
    has_next = jnp.logical_and(t != ctx - 1, t != total - 1)
    before = jnp.where(has_prev, buf_ref[V7X_SUBLANES - 1:V7X_SUBLANES - 1 + rows, :], 0.0)
    after = jnp.where(has_next, buf_ref[V7X_SUBLANES + 1:V7X_SUBLANES + 1 + rows, :], 0.0)
    return before, after


BF16_ROWS = 16


def _ffn_in_kernel(h_ref, hp_ref, hn_ref, wg_ref, wu_ref, cw_ref, cb_ref, o_ref, abuf, gbuf, *, tm, ctx, total):
    abuf[0:BF16_ROWS, :] = hp_ref[...]
    abuf[BF16_ROWS:BF16_ROWS + tm, :] = h_ref[...]
    abuf[BF16_ROWS + tm:, :] = hn_ref[...]
    gbuf[...] = jnp.dot(abuf[...], wg_ref[...], preferred_element_type=F32)
    up = jnp.dot(h_ref[...], wu_ref[...], preferred_element_type=F32)
    t = pl.program_id(1) * tm + lax.broadcasted_iota(jnp.int32, (tm, 1), 0)
    has_prev = jnp.logical_and(t != 0, t != ctx)
    has_next = jnp.logical_and(t != ctx - 1, t != total - 1)
    before = jnp.where(has_prev, gbuf[BF16_ROWS - 1:BF16_ROWS - 1 + tm, :], 0.0)
    after = jnp.where(has_next, gbuf[BF16_ROWS + 1:BF16_ROWS + 1 + tm, :], 0.0)
    conv = (before * cw_ref[0:1, :] + gbuf[BF16_ROWS:BF16_ROWS + tm, :] * cw_ref[1:2, :] + after * cw_ref[2:3, :]
            + cb_ref[...])
    o_ref[...] = (jax.nn.silu(conv) * up).astype(o_ref.dtype)


def ffn_in_glu(h, w_gate, w_up, conv_w, conv_b, ctx):
    t, d = h.shape
    fp = w_gate.shape[1]
    tm, tn, _ = _mm_tiles(t, d, fp)
    hb = tm // BF16_ROWS
    last = t // BF16_ROWS - 1
    once = pl.Buffered(1)
    blocks = _nbytes((tm + 2 * BF16_ROWS, d), BF16) + _nbytes((tm, tn), BF16) + 4 * _nbytes((8, tn), F32)
    resident = 2 * _nbytes((d, tn), BF16)
    scratch = _nbytes((tm + 2 * BF16_ROWS, d), BF16) + _nbytes((tm + 2 * BF16_ROWS, tn), F32)
    return pl.pallas_call(
        functools.partial(_ffn_in_kernel, tm=tm, ctx=ctx, total=t),
        out_shape=jax.ShapeDtypeStruct((t, fp), BF16),
        grid=(fp // tn, t // tm),
        in_specs=[
            pl.BlockSpec((tm, d), lambda j, i: (i, 0)),
            pl.BlockSpec((BF16_ROWS, d), lambda j, i: (jnp.maximum(i * hb - 1, 0), 0)),
            pl.BlockSpec((BF16_ROWS, d), lambda j, i: (jnp.minimum((i + 1) * hb, last), 0)),
            pl.BlockSpec((d, tn), lambda j, i: (0, j), pipeline_mode=once),
            pl.BlockSpec((d, tn), lambda j, i: (0, j), pipeline_mode=once),
            pl.BlockSpec((3, tn), lambda j, i: (0, j)),
            pl.BlockSpec((1, tn), lambda j, i: (0, j)),
        ],
        out_specs=pl.BlockSpec((tm, tn), lambda j, i: (i, j)),
        scratch_shapes=[pltpu.VMEM((tm + 2 * BF16_ROWS, d), BF16), pltpu.VMEM((tm + 2 * BF16_ROWS, tn), F32)],
        compiler_params=pltpu.CompilerParams(
            dimension_semantics=("arbitrary", "arbitrary"),
            vmem_limit_bytes=_vmem_limit(blocks, resident + scratch + 5 * _nbytes((tm, tn), F32))),
        name="ffn_in_glu",
    )(h, h, h, w_gate, w_up, conv_w, conv_b.reshape(1, fp))


def _rw_mix_kernel(x_ref, xp_ref, xn_ref, g_ref, sh_ref, sc_ref, mix_ref, *rest, rows, ctx, total):
    outs, buf_ref = rest[:6], rest[6]
    row0 = pl.program_id(0) * rows
    g, sh, sc = g_ref[...], sh_ref[...], sc_ref[...]

    def nm(x, first_row):
        t = first_row + lax.broadcasted_iota(jnp.int32, (x.shape[0], 1), 0)
        return _norm_mod_rows(x, g, sh, sc, t < ctx)

    h = nm(x_ref[...], row0)
    hp = nm(xp_ref[...], row0 - V7X_SUBLANES)
    hn = nm(xn_ref[...], row0 + rows)
    before, after = _seq_neighbors(buf_ref, h, hp, hn, rows, row0, ctx, total)
    xx = 0.5 * (before + after) - h
    for n in range(6):
        outs[n][...] = (h + xx * mix_ref[n:n + 1, :]).astype(outs[n].dtype)


def rw_token_mix(x, g, shift, scale, mix, ctx):
    t, d = x.shape
    rows = _tile(t, (128, 64, 32, 16, 8))
    rb = rows // V7X_SUBLANES
    last8 = t // V7X_SUBLANES - 1
    blocks = _nbytes((rows + 16, d), F32) + 6 * _nbytes((rows, d), BF16) + 4 * _nbytes((8, d), F32)
    return pl.pallas_call(
        functools.partial(_rw_mix_kernel, rows=rows, ctx=ctx, total=t),
        out_shape=[jax.ShapeDtypeStruct((t, d), BF16)] * 6,
        grid=(t // rows,),
        in_specs=[
            pl.BlockSpec((rows, d), lambda i: (i, 0)),
            pl.BlockSpec((V7X_SUBLANES, d), lambda i: (jnp.maximum(i * rb - 1, 0), 0)),
            pl.BlockSpec((V7X_SUBLANES, d), lambda i: (jnp.minimum((i + 1) * rb, last8), 0)),
            pl.BlockSpec((1, d), lambda i: (0, 0)),
            pl.BlockSpec((2, d), lambda i: (0, 0)),
            pl.BlockSpec((2, d), lambda i: (0, 0)),
            pl.BlockSpec((6, d), lambda i: (0, 0)),
        ],
        out_specs=[pl.BlockSpec((rows, d), lambda i: (i, 0))] * 6,
        scratch_shapes=[pltpu.VMEM((rows + 2 * V7X_SUBLANES, d), F32)],
        compiler_params=pltpu.CompilerParams(
            dimension_semantics=("arbitrary",),
            vmem_limit_bytes=_vmem_limit(blocks, 6 * _nbytes((rows + 16, d), F32))),
        name="rw_token_mix",
    )(x, x, x, g.reshape(1, d), shift, scale, mix)


SCAN_STEPS = 16


def _mirror_block(i, n_blocks, n_ctx_blocks):
    return jnp.where(i < n_ctx_blocks, n_ctx_blocks - 1 - i, n_blocks + n_ctx_blocks - 1 - i)


def _scan_specs(t, chans, lanes, ctx):
    assert ctx % SCAN_STEPS == 0 and t % SCAN_STEPS == 0
    nb, nbc = t // SCAN_STEPS, ctx // SCAN_STEPS
    here = pl.BlockSpec((SCAN_STEPS, chans, lanes), lambda i: (i, 0, 0))
    mirror = pl.BlockSpec((SCAN_STEPS, chans, lanes), lambda i: (_mirror_block(i, nb, nbc), 0, 0))
    return nb, here, mirror


def _rw_prep_kernel(ra, rb, ka, kb, va, vb, wla, wlb, ala, alb, w0_ref, a0_ref, kk_ref, ka_ref, rk_ref,
                    w_o, a_o, b_o, kd_o, wr_o, v_o, bonus_o, br_o, kr_o):
    lanes = wla.shape[-1]
    is_fwd = lax.broadcasted_iota(jnp.int32, (1, lanes), 1) < lanes // 2

    def body(s, carry):
        sb = SCAN_STEPS - 1 - s

        def pick(xa, xb):
            return jnp.where(is_fwd, xa[s], xb[sb])

        def join(xa, xb):
            return jnp.concatenate([xa[s], xb[sb]], axis=-1)

        r, k, v = join(ra, rb), join(ka, kb), join(va, vb)
        kk = k * kk_ref[...]
        kk = kk * lax.rsqrt(jnp.maximum(jnp.sum(kk * kk, axis=0, keepdims=True), 1e-24))
        w = jnp.exp(-jnp.exp(-jax.nn.softplus(-(w0_ref[...] + pick(wla, wlb))) - 0.5))
        iclr = jax.nn.sigmoid(a0_ref[...] + pick(ala, alb))
        kd = k * (1.0 + (iclr - 1.0) * ka_ref[...])
        b = kk * iclr
        w_o[s] = w
        a_o[s] = -kk
        b_o[s] = b
        kd_o[s] = kd
        wr_o[s] = w * r
        v_o[s] = v
        br_o[s] = jnp.sum(b * r, axis=0, keepdims=True)
        kr_o[s] = jnp.sum(kd * r, axis=0, keepdims=True)
        bonus_o[s] = jnp.sum(r * kd * rk_ref[...], axis=0, keepdims=True) * v
        return carry

    lax.fori_loop(0, SCAN_STEPS, body, 0)


def _wkv_kernel(*refs, chans):
    prep_in, (y_ref, bonus_ref), scratch = refs[:15], refs[15:17], refs[17:]
    w_ref, a_ref, b_ref, kd_ref, wr_ref, v_ref, br_ref, kr_ref, s_ref = scratch
    _rw_prep_kernel(*prep_in, w_ref, a_ref, b_ref, kd_ref, wr_ref, v_ref, bonus_ref, br_ref, kr_ref)

    @pl.when(pl.program_id(0) == 0)
    def _():
        s_ref[...] = jnp.zeros_like(s_ref)

    def step(i, carry):
        sa = jnp.zeros(s_ref.shape[1:], F32)
        u = jnp.zeros(s_ref.shape[1:], F32)
        for c in range(chans):
            sc = s_ref[c]
            sa = sa + sc * a_ref[i, c:c + 1, :]
            u = u + sc * wr_ref[i, c:c + 1, :]
        vv = v_ref[i]
        for c in range(chans):
            s_ref[c] = s_ref[c] * w_ref[i, c:c + 1, :] + sa * b_ref[i, c:c + 1, :] + vv * kd_ref[i, c:c + 1, :]
        y_ref[i] = u + sa * br_ref[i] + vv * kr_ref[i]
        return carry

    lax.fori_loop(0, SCAN_STEPS, step, 0)


def wkv_scan(r, k, v, wl, al, w0, a0, k_k, k_a, r_k, ctx):
    t, chans, lanes = wl.shape
    nb, here, mirror = _scan_specs(t, chans, lanes, ctx)
    _, here_h, mirror_h = _scan_specs(t, chans, lanes // 2, ctx)
    par = pl.BlockSpec((chans, lanes), lambda i: (0, 0))
    big = jax.ShapeDtypeStruct((t, chans, lanes), F32)
    step_block = pltpu.VMEM((SCAN_STEPS, chans, lanes), F32)
    step_row = pltpu.VMEM((SCAN_STEPS, 1, lanes), F32)
    blocks = 12 * _nbytes((SCAN_STEPS, chans, lanes), F32) + 5 * _nbytes((chans, lanes), F32)
    scratch_bytes = (6 * _nbytes((SCAN_STEPS, chans, lanes), F32) + 2 * _nbytes((SCAN_STEPS, 8, lanes), F32)
                     + _nbytes((chans, chans, lanes), F32))
    return pl.pallas_call(
        functools.partial(_wkv_kernel, chans=chans),
        out_shape=[big, big],
        grid=(nb,),
        in_specs=[here_h, mirror_h] * 3 + [here, mirror] * 2 + [par] * 5,
        out_specs=[here, here],
        scratch_shapes=[step_block] * 6 + [step_row] * 2 + [pltpu.VMEM((chans, chans, lanes), F32)],
        compiler_params=_params(("arbitrary",), blocks, scratch_bytes + 16 * _nbytes((chans, lanes), F32)),
        name="wkv_scan",
    )(r, r, k, k, v, v, wl, wl, al, al, w0, a0, k_k, k_a, r_k)


def _rw_finish_kernel(ya, yb, ba, bb, gate_ref, lnw_ref, lnb_ref, o_ref):
    half = ya.shape[-1] // 2

    def body(s, carry):
        sb = SCAN_STEPS - 1 - s
        y = (ya[s] + pltpu.roll(yb[sb], half, 1))[:, :half]
        bonus = (ba[s] + pltpu.roll(bb[sb], half, 1))[:, :half]
        mu = jnp.mean(y, axis=0, keepdims=True)
        yc = y - mu
        var = jnp.mean(yc * yc, axis=0, keepdims=True)
        z = yc * lax.rsqrt(var + RW_LN_EPS) * lnw_ref[...] + lnb_ref[...] + bonus
        o_ref[s] = (z * gate_ref[s]).astype(o_ref.dtype)
        return carry

    lax.fori_loop(0, SCAN_STEPS, body, 0)


def rw_finish(y, bonus, gate, ln_w, ln_b, ctx):
    t, chans, lanes = y.shape
    heads = lanes // 2
    nb, here, mirror = _scan_specs(t, chans, lanes, ctx)
    half = pl.BlockSpec((SCAN_STEPS, chans, heads), lambda i: (i, 0, 0))
    par = pl.BlockSpec((chans, heads), lambda i: (0, 0))
    blocks = 6 * _nbytes((SCAN_STEPS, chans, lanes), F32) + 2 * _nbytes((chans, lanes), F32)
    return pl.pallas_call(
        _rw_finish_kernel,
        out_shape=jax.ShapeDtypeStruct((t, chans, heads), BF16),
        grid=(nb,),
        in_specs=[here, mirror, here, mirror, half, par, par],
        out_specs=half,
        compiler_params=_params(("arbitrary",), blocks, 8 * _nbytes((chans, lanes), F32)),
        name="rw_finish",
    )(y, y, bonus, bonus, gate, ln_w, ln_b)


def _rope_tables(ctx, seq):
    pos = jnp.arange(seq)
    row = (pos // GRID_W).astype(F32)
    col = (pos % GRID_W).astype(F32)
    n_freq = HEAD_DIM // 4
    inv_freq = ROPE_THETA ** (-jnp.arange(n_freq, dtype=F32) / n_freq)
    ang = jnp.concatenate([row[:, None] * inv_freq, col[:, None] * inv_freq], axis=-1)
    cos = jnp.concatenate([jnp.ones((ctx, HEAD_DIM // 2), F32), jnp.cos(ang)], axis=0)
    sin = jnp.concatenate([jnp.zeros((ctx, HEAD_DIM // 2), F32), jnp.sin(ang)], axis=0)
    return jnp.concatenate([cos, cos], axis=-1), jnp.concatenate([-sin, sin], axis=-1)


def _channel_major(w):
    heads = w.shape[-1] // RW_HEAD_DIM
    return jnp.swapaxes(w.reshape(w.shape[:-1] + (heads, RW_HEAD_DIM)), -1, -2)


def _per_direction(w2):
    cm = _channel_major(w2)
    zero = jnp.zeros_like(cm[0])
    top = jnp.concatenate([cm[0], zero], axis=-1)
    bottom = jnp.concatenate([zero, cm[1]], axis=-1)
    return jnp.concatenate([top, bottom], axis=0).reshape(2 * w2.shape[1], -1)


def _score_gain(q_g, k_g):
    q_gain = q_g * (HEAD_DIM ** -0.5 * math.log2(math.e))
    bound = HEAD_DIM * jnp.max(jnp.abs(q_gain)) * jnp.max(jnp.abs(k_g))
    return q_gain, bound


def _diff_lambda_init(layer_idx):
    return 0.8 - 0.6 * math.exp(-0.3 * layer_idx)


def _diff_attention(h, cc, ss, wqkv, wo, q_g, k_g, lam_vecs, subln_g, lam_init, x, gate, ctx):
    d = h.shape[1]
    n_heads = d // HEAD_DIM
    q_gain, bound = _score_gain(q_g, k_g)
    gain = jnp.concatenate([jnp.tile(q_gain, n_heads), jnp.tile(k_g, n_heads)])
    qk = matmul_headnorm_rope(h, wqkv[:, :2 * d].astype(BF16), gain, cc, ss, name="da_qk_proj")
    v = matmul(h, wqkv[:, 2 * d:].astype(BF16), BF16, name="da_v_proj")
    o = flash_attention(qk, v, ctx, bound, groups=n_heads // 2, n_sub=2, per_sub_k=True, lam=lam_vecs,
                        subln=subln_g, lam_init=lam_init, name="da_flash")
    return matmul_residual(o, wo.astype(BF16), x, gate, ctx, name="da_out_proj")


def _gqa_attention(h, cc, ss, wqkv, wo, q_g, k_g, x, gate, ctx):
    d = h.shape[1]
    kv_w = d // GA_GROUP
    q_gain, bound = _score_gain(q_g, k_g)
    gain = jnp.concatenate([jnp.tile(q_gain, d // HEAD_DIM), jnp.tile(k_g, kv_w // HEAD_DIM)])
    qk = matmul_headnorm_rope(h, wqkv[:, :d + kv_w].astype(BF16), gain, cc, ss, name="ga_qk_proj")
    v = matmul(h, wqkv[:, d + kv_w:].astype(BF16), BF16, name="ga_v_proj")
    o = flash_attention(qk, v, ctx, bound, groups=kv_w // HEAD_DIM, n_sub=GA_GROUP, per_sub_k=False,
                        name="ga_flash")
    return matmul_residual(o, wo.astype(BF16), x, gate, ctx, name="ga_out_proj")


def _pad_cols(w, n):
    return jnp.pad(w, ((0, 0), (0, n - w.shape[1])))


def _rwkv7(x, g, shift, scale, mix, wrkv, wo, w0, w1, w2, a0, a1, a2, g1, g2, k_k, k_a, r_k, ln_w, ln_b, gate,
           ctx):
    t, d = x.shape
    heads = d // RW_HEAD_DIM
    scan_shape = (t, RW_HEAD_DIM, 2 * heads)
    xr, xw, xk, xv, xa, xg = rw_token_mix(x, g, shift, scale, mix, ctx)

    def shared(xin, w, name):
        w_cm = _channel_major(w).reshape(w.shape)
        return matmul(xin, w_cm.astype(BF16), F32, name=name).reshape(t, RW_HEAD_DIM, heads)

    r = shared(xr, wrkv[0], "rw_r_proj")
    k = shared(xk, wrkv[1], "rw_k_proj")
    v = shared(xv, wrkv[2], "rw_v_proj")
    hw = matmul(xw, jnp.concatenate([w1[0], w1[1]], axis=1).astype(BF16), BF16, act="tanh", name="rw_w_lora_in")
    ha = matmul(xa, jnp.concatenate([a1[0], a1[1]], axis=1).astype(BF16), BF16, name="rw_a_lora_in")
    wl = matmul(hw, _per_direction(w2).astype(BF16), F32, name="rw_w_lora_out").reshape(scan_shape)
    al = matmul(ha, _per_direction(a2).astype(BF16), F32, name="rw_a_lora_out").reshape(scan_shape)
    lg = -(-g1.shape[1] // V7X_LANES) * V7X_LANES
    hg = matmul(xg, _pad_cols(g1, lg).astype(BF16), BF16, act="sigmoid", name="rw_g_lora_in")
    g2p = jnp.pad(_channel_major(g2).reshape(g2.shape), ((0, lg - g2.shape[0]), (0, 0))).astype(BF16)
    out_gate = matmul(hg, g2p, F32, name="rw_g_lora_out").reshape(t, RW_HEAD_DIM, heads)

    def dir_param(p):
        return jnp.concatenate([_channel_major(p[0]), _channel_major(p[1])], axis=-1)

    def shared_param(p):
        return jnp.concatenate([_channel_major(p)] * 2, axis=-1)

    y, bonus = wkv_scan(r, k, v, wl, al, dir_param(w0), dir_param(a0), shared_param(k_k), shared_param(k_a),
                        shared_param(r_k.reshape(d)), ctx)
    z = rw_finish(y, bonus, out_gate, _channel_major(ln_w), _channel_major(ln_b), ctx).reshape(t, d)
    wo_cm = jnp.swapaxes(wo.reshape(heads, RW_HEAD_DIM, d), 0, 1).reshape(d, d)
    return matmul_residual(z, wo_cm.astype(BF16), x, gate, ctx, name="rw_out_proj")


def _conv_glu(h, w_in, conv_w, conv_b, w_out, x, gate, ctx):
    f = conv_w.shape[1]
    fp = -(-f // FFN_PAD) * FFN_PAD
    act = ffn_in_glu(h, _pad_cols(w_in[:, :f], fp).astype(BF16), _pad_cols(w_in[:, f:], fp).astype(BF16),
                     _pad_cols(conv_w, fp), jnp.pad(conv_b, (0, fp - f)), ctx)
    w_out_p = jnp.pad(w_out, ((0, fp - f), (0, 0))).astype(BF16)
    return matmul_residual(act, w_out_p, x, gate, ctx, name="ffn_out_proj")


def kernel(x, c, ctx, c_ctx, ada_down, ada_up, ada_b, norm_g, ffn_in, ffn_conv, ffn_conv_b, ffn_out, da_wqkv, da_wo, da_q_g, da_k_g, da_lambda, da_subln_g, rw_mix, rw_wrkv, rw_wo, rw_w0, rw_w1, rw_w2, rw_a0, rw_a1, rw_a2, rw_g1, rw_g2, rw_k_k, rw_k_a, rw_r_k, rw_ln_w, rw_ln_b, ga_wqkv, ga_wo, ga_q_g, ga_k_g):
    assert x.shape[0] == 1
    seq, d = x.shape[1], x.shape[2]
    n_ctx = ctx.shape[1]
    depth = ada_down.shape[0]
    stream = jnp.concatenate([ctx[0], x[0]], axis=0)
    cond = jnp.zeros((V7X_SUBLANES, d), F32).at[0].set(c_ctx).at[1].set(c[0])
    mod = ada_modulation(cond, ada_down, ada_up, ada_b)[:, :2].reshape(depth, 2, N_MOD, d)
    cc, ss = _rope_tables(n_ctx, seq)
    for i in range(depth):
        kind, j = i % N_MIXERS, i // N_MIXERS
        m = [mod[i, :, n, :] for n in range(N_MOD)]
        if kind == 1:
            stream = _rwkv7(stream, norm_g[i, 0], m[0], m[1], rw_mix[j], rw_wrkv[j], rw_wo[j], rw_w0[j], rw_w1[j],
                            rw_w2[j], rw_a0[j], rw_a1[j], rw_a2[j], rw_g1[j], rw_g2[j], rw_k_k[j], rw_k_a[j],
                            rw_r_k[j], rw_ln_w[j], rw_ln_b[j], m[2], n_ctx)
        else:
            h = norm_mod(stream, norm_g[i, 0], m[0], m[1], n_ctx)
            if kind == 0:
                stream = _diff_attention(h, cc, ss, da_wqkv[j], da_wo[j], da_q_g[j], da_k_g[j], da_lambda[j],
                                         da_subln_g[j], _diff_lambda_init(i), stream, m[2], n_ctx)
            else:
                stream = _gqa_attention(h, cc, ss, ga_wqkv[j], ga_wo[j], ga_q_g[j], ga_k_g[j], stream, m[2], n_ctx)
        h2 = norm_mod(stream, norm_g[i, 1], m[3], m[4], n_ctx)
        stream = _conv_glu(h2, ffn_in[i], ffn_conv[i], ffn_conv_b[i], ffn_out[i], stream, m[5], n_ctx)
    return stream[n_ctx:][None]
```

```python
import functools
import math

import jax
import jax.numpy as jnp
from jax import lax
from jax.experimental import pallas as pl
from jax.experimental.pallas import tpu as pltpu

F32 = jnp.float32
BF16 = jnp.bfloat16

N_MOD = 6
N_MIXERS = 3
GRID_W = 64
ROPE_THETA = 10000.0
NORM_EPS = 1e-6
HEAD_DIM = 128
GA_GROUP = 4
RW_HEAD_DIM = 64
RW_LN_EPS = 64e-5

V7X_LANES = 128
V7X_SUBLANES = 8
V7X_MXU_WIDTH = 256
V7X_VMEM_BYTES = 64 * 1024 * 1024
V7X_VMEM_RESERVE = 6 * 1024 * 1024
FFN_PAD = 1024


def _vmem_limit(block_bytes, temp_bytes=0):
    return int(min(2 * block_bytes + temp_bytes + V7X_VMEM_RESERVE, V7X_VMEM_BYTES - V7X_VMEM_RESERVE))


def _params(semantics, block_bytes, temp_bytes=0):
    return pltpu.CompilerParams(dimension_semantics=semantics,
                                vmem_limit_bytes=_vmem_limit(block_bytes, temp_bytes))


def _tile(n, prefs):
    for t in prefs:
        if n % t == 0:
            return t
    return n


def _nbytes(shape, dtype):
    return math.prod(shape) * jnp.dtype(dtype).itemsize


def _row_ids(i, rows, shape):
    return i * rows + lax.broadcasted_iota(jnp.int32, shape, 0)


def _ada_kernel(cond_ref, down_ref, up_ref, b_ref, o_ref):
    s = jax.nn.silu(cond_ref[...])
    t = jnp.dot(s, down_ref[...], preferred_element_type=F32, precision=lax.Precision.HIGHEST)
    m = jnp.dot(t, up_ref[...], preferred_element_type=F32, precision=lax.Precision.HIGHEST)
    o_ref[...] = m + b_ref[...]


def ada_modulation(cond, ada_down, ada_up, ada_b):
    depth, d, r = ada_down.shape
    n = ada_up.shape[-1]
    tn = _tile(n, (4096, 2048, 1024, 512))
    blocks = _nbytes((8, d), F32) + _nbytes((d, r), F32) + _nbytes((r, tn), F32) + 2 * _nbytes((8, tn), F32)
    return pl.pallas_call(
        _ada_kernel,
        out_shape=jax.ShapeDtypeStruct((depth, 8, n), F32),
        grid=(depth, n // tn),
        in_specs=[
            pl.BlockSpec((8, d), lambda l, j: (0, 0)),
            pl.BlockSpec((None, d, r), lambda l, j: (l, 0, 0)),
            pl.BlockSpec((None, r, tn), lambda l, j: (l, 0, j)),
            pl.BlockSpec((None, 1, tn), lambda l, j: (l, 0, j)),
        ],
        out_specs=pl.BlockSpec((None, 8, tn), lambda l, j: (l, 0, j)),
        compiler_params=_params(("arbitrary", "arbitrary"), blocks),
        name="ada_modulation",
    )(cond, ada_down, ada_up, ada_b.reshape(depth, 1, n))


def _norm_mod_rows(x, g, shift, scale, is_ctx):
    y = x * lax.rsqrt(jnp.mean(x * x, axis=-1, keepdims=True) + NORM_EPS) * g
    sc = jnp.where(is_ctx, scale[0:1, :], scale[1:2, :])
    sh = jnp.where(is_ctx, shift[0:1, :], shift[1:2, :])
    return y * (1.0 + sc) + sh


def _norm_mod_kernel(x_ref, g_ref, sh_ref, sc_ref, o_ref, *, rows, ctx):
    is_ctx = _row_ids(pl.program_id(0), rows, (rows, 1)) < ctx
    o_ref[...] = _norm_mod_rows(x_ref[...], g_ref[...], sh_ref[...], sc_ref[...], is_ctx).astype(o_ref.dtype)


def norm_mod(x, g, shift, scale, ctx):
    t, d = x.shape
    rows = _tile(t, (256, 128, 64, 32, 16, 8))
    blocks = _nbytes((rows, d), F32) + _nbytes((rows, d), BF16) + 5 * _nbytes((8, d), F32)
    return pl.pallas_call(
        functools.partial(_norm_mod_kernel, rows=rows, ctx=ctx),
        out_shape=jax.ShapeDtypeStruct((t, d), BF16),
        grid=(t // rows,),
        in_specs=[
            pl.BlockSpec((rows, d), lambda i: (i, 0)),
            pl.BlockSpec((1, d), lambda i: (0, 0)),
            pl.BlockSpec((2, d), lambda i: (0, 0)),
            pl.BlockSpec((2, d), lambda i: (0, 0)),
        ],
        out_specs=pl.BlockSpec((rows, d), lambda i: (i, 0)),
        compiler_params=_params(("arbitrary",), blocks, 4 * _nbytes((rows, d), F32)),
        name="norm_mod",
    )(x, g.reshape(1, d), shift, scale)


def _mm_plain_kernel(a_ref, b_ref, o_ref, *, act):
    acc = jnp.dot(a_ref[...], b_ref[...], preferred_element_type=F32)
    if act == "tanh":
        acc = jnp.tanh(acc)
    elif act == "sigmoid":
        acc = jax.nn.sigmoid(acc)
    o_ref[...] = acc.astype(o_ref.dtype)


def _mm_residual_kernel(a_ref, b_ref, x_ref, g_ref, o_ref, *, tm, ctx):
    acc = jnp.dot(a_ref[...], b_ref[...], preferred_element_type=F32)
    is_ctx = _row_ids(pl.program_id(1), tm, (tm, 1)) < ctx
    gate = jnp.where(is_ctx, g_ref[0:1, :], g_ref[1:2, :])
    o_ref[...] = x_ref[...] + gate * acc


def _mm_headnorm_rope_kernel(a_ref, b_ref, g_ref, cc_ref, ss_ref, o_ref, *, tn):
    cc = cc_ref[...]
    ss = ss_ref[...]
    gw = V7X_MXU_WIDTH if tn % V7X_MXU_WIDTH == 0 else HEAD_DIM
    for c in range(tn // gw):
        acc = jnp.dot(a_ref[...], b_ref[:, c * gw:(c + 1) * gw], preferred_element_type=F32)
        for h in range(gw // HEAD_DIM):
            sl = slice(c * gw + h * HEAD_DIM, c * gw + (h + 1) * HEAD_DIM)
            z = acc[:, h * HEAD_DIM:(h + 1) * HEAD_DIM]
            y = z * lax.rsqrt(jnp.mean(z * z, axis=-1, keepdims=True) + NORM_EPS) * g_ref[:, sl]
            o_ref[:, sl] = (y * cc + pltpu.roll(y, HEAD_DIM // 2, 1) * ss).astype(o_ref.dtype)


MM_WEIGHT_TILE_BYTES = 12 * 1024 * 1024
MM_DOUBLE_BUFFER_BYTES = 8 * 1024 * 1024


def _mm_tiles(m, k, n):
    tm = _tile(m, (640, 512, 384, 256, 128, 64, 32, 16, 8))
    tn = next(c for c in (1024, 512, 256, 128, n) if n % c == 0 and _nbytes((k, c), BF16) <= MM_WEIGHT_TILE_BYTES)
    return tm, tn, _nbytes((k, tn), BF16) > MM_DOUBLE_BUFFER_BYTES


def _mm_call(kernel, a, b, extra_inputs, extra_specs, out_dtype, extra_bytes, name):
    m, k = a.shape
    n = b.shape[1]
    tm, tn, deep = _mm_tiles(m, k, n)
    weights = _nbytes((k, tn), b.dtype)
    blocks = _nbytes((tm, k), a.dtype) + _nbytes((tm, tn), out_dtype) + extra_bytes(tm, tn)
    b_spec = pl.BlockSpec((k, tn), lambda j, i: (0, j), **({"pipeline_mode": pl.Buffered(1)} if deep else {}))
    return pl.pallas_call(
        kernel,
        out_shape=jax.ShapeDtypeStruct((m, n), out_dtype),
        grid=(n // tn, m // tm),
        in_specs=[pl.BlockSpec((tm, k), lambda j, i: (i, 0)), b_spec] + extra_specs(tm, tn),
        out_specs=pl.BlockSpec((tm, tn), lambda j, i: (i, j)),
        compiler_params=pltpu.CompilerParams(
            dimension_semantics=("arbitrary", "arbitrary"),
            vmem_limit_bytes=_vmem_limit(blocks, (1 if deep else 2) * weights + 2 * _nbytes((tm, tn), F32))),
        name=name,
    )(a, b, *extra_inputs)


def matmul(a, b, out_dtype, act=None, name="matmul"):
    return _mm_call(functools.partial(_mm_plain_kernel, act=act), a, b, (), lambda tm, tn: [],
                    out_dtype, lambda tm, tn: 0, name)


def matmul_residual(a, b, x, gate, ctx, name="matmul_residual"):
    tm = _mm_tiles(a.shape[0], a.shape[1], b.shape[1])[0]

    def specs(tm_, tn):
        return [pl.BlockSpec((tm_, tn), lambda j, i: (i, j)), pl.BlockSpec((2, tn), lambda j, i: (0, j))]

    return _mm_call(functools.partial(_mm_residual_kernel, tm=tm, ctx=ctx), a, b, (x, gate), specs, F32,
                    lambda tm_, tn: _nbytes((tm_, tn), F32) + _nbytes((8, tn), F32), name)


def matmul_headnorm_rope(a, b, gain, cc, ss, name="matmul_headnorm_rope"):
    m, k = a.shape
    n = b.shape[1]
    tm, tn, _ = _mm_tiles(m, k, n)
    blocks = (_nbytes((tm, k), a.dtype) + _nbytes((k, tn), b.dtype) + _nbytes((tm, tn), BF16)
              + _nbytes((8, tn), F32) + 2 * _nbytes((tm, HEAD_DIM), F32) + _nbytes((tm, tn), F32))
    return pl.pallas_call(
        functools.partial(_mm_headnorm_rope_kernel, tn=tn),
        out_shape=jax.ShapeDtypeStruct((m, n), BF16),
        grid=(n // tn, m // tm),
        in_specs=[
            pl.BlockSpec((tm, k), lambda j, i: (i, 0)),
            pl.BlockSpec((k, tn), lambda j, i: (0, j)),
            pl.BlockSpec((1, tn), lambda j, i: (0, j)),
            pl.BlockSpec((tm, HEAD_DIM), lambda j, i: (i, 0)),
            pl.BlockSpec((tm, HEAD_DIM), lambda j, i: (i, 0)),
        ],
        out_specs=pl.BlockSpec((tm, tn), lambda j, i: (i, j)),
        compiler_params=_params(("arbitrary", "arbitrary"), blocks, 2 * _nbytes((tm, tn), F32)),
        name=name,
    )(a, b, gain.reshape(1, n), cc, ss)


SCORE_BOUND_DIRECT = 64.0


def _flash_finalize(o_ref, l_rows, acc_ref, lam_ref, subln_ref, *, n_sub, lam_init):
    if lam_init is None:
        for s in range(n_sub):
            o_ref[:, s * HEAD_DIM:(s + 1) * HEAD_DIM] = (acc_ref[s] / l_rows[s]).astype(o_ref.dtype)
        return
    lv = lam_ref[...]
    lam = (jnp.exp(jnp.sum(lv[0:1, :] * lv[1:2, :], axis=-1, keepdims=True))
           - jnp.exp(jnp.sum(lv[2:3, :] * lv[3:4, :], axis=-1, keepdims=True)) + lam_init)
    o = acc_ref[0] / l_rows[0] - lam * (acc_ref[1] / l_rows[1])
    o = o * lax.rsqrt(jnp.mean(o * o, axis=-1, keepdims=True) + NORM_EPS) * subln_ref[...]
    o_ref[...] = (o * (1.0 - lam_init)).astype(o_ref.dtype)


def _flash_direct_kernel(q_ref, kt_ref, v_ref, lam_ref, subln_ref, o_ref, l_ref, acc_ref, *, n_sub, tq, tc, ctx,
                         n_chunks, lam_init):
    l_ref[...] = jnp.zeros_like(l_ref)
    acc_ref[...] = jnp.zeros_like(acc_ref)

    def accumulate(chunk, width, v, r0, nr):
        for s in range(n_sub):
            cols = slice(s * HEAD_DIM, (s + 1) * HEAD_DIM)
            p = jnp.exp2(jnp.dot(q_ref[r0:r0 + nr, cols], kt_ref[chunk, cols, 0:width], preferred_element_type=F32))
            part = p[:, 0:V7X_LANES]
            for j in range(1, width // V7X_LANES):
                part = part + p[:, j * V7X_LANES:(j + 1) * V7X_LANES]
            l_ref[s, r0:r0 + nr, :] += part
            acc_ref[s, r0:r0 + nr, :] += jnp.dot(p.astype(v.dtype), v, preferred_element_type=F32)

    def all_keys(r0, nr):
        def body(c, carry):
            accumulate(c, tc, v_ref[pl.ds(pl.multiple_of(c * tc, tc), tc), :], r0, nr)
            return carry

        lax.fori_loop(0, n_chunks, body, 0)

    @pl.when(pl.program_id(1) > 0)
    def _():
        all_keys(0, tq)

    @pl.when(pl.program_id(1) == 0)
    def _():
        accumulate(0, ctx, v_ref[0:ctx, :], 0, ctx)
        if ctx < tq:
            all_keys(ctx, tq - ctx)

    l_rows = [jnp.sum(l_ref[s], axis=-1, keepdims=True) for s in range(n_sub)]
    _flash_finalize(o_ref, l_rows, acc_ref, lam_ref, subln_ref, n_sub=n_sub, lam_init=lam_init)


def _flash_direct_shared_kernel(qt_ref, k_ref, vt_ref, ot_ref, qs_ref, l_ref, acc_ref, *, n_sub, tq, tc, ctx, n_chunks):
    for g in range(n_sub):
        qs_ref[:, g * tq:(g + 1) * tq] = qt_ref[g * HEAD_DIM:(g + 1) * HEAD_DIM, :]
    l_ref[...] = jnp.zeros_like(l_ref)
    acc_ref[...] = jnp.zeros_like(acc_ref)

    def accumulate(k, vt):
        pt = jnp.exp2(jnp.dot(k, qs_ref[...], preferred_element_type=F32))
        l_ref[...] += jnp.sum(pt, axis=0, keepdims=True)
        acc_ref[...] += jnp.dot(vt, pt.astype(vt.dtype), preferred_element_type=F32)

    @pl.when(pl.program_id(1) > 0)
    def _():
        def body(c, carry):
            accumulate(k_ref[pl.ds(pl.multiple_of(c * tc, tc), tc), :], vt_ref[c])
            return carry

        lax.fori_loop(0, n_chunks, body, 0)

    @pl.when(pl.program_id(1) == 0)
    def _():
        accumulate(k_ref[0:ctx, :], vt_ref[0, :, 0:ctx])

    out = acc_ref[...] / l_ref[...]
    for g in range(n_sub):
        ot_ref[g * HEAD_DIM:(g + 1) * HEAD_DIM, :] = out[:, g * tq:(g + 1) * tq].astype(ot_ref.dtype)


def _flash_online_step(q_ref, kt_ref, v_ref, m_ref, l_ref, acc_ref, *, n_sub, per_sub_k, tk, ctx, masked):
    ki = pl.program_id(2)
    v = v_ref[...]
    for s in range(n_sub):
        q = q_ref[:, s * HEAD_DIM:(s + 1) * HEAD_DIM]
        kt = kt_ref[s * HEAD_DIM:(s + 1) * HEAD_DIM, :] if per_sub_k else kt_ref[...]
        sc = jnp.dot(q, kt, preferred_element_type=F32)
        if masked:
            col = ki * tk + lax.broadcasted_iota(jnp.int32, sc.shape, 1)
            sc = jnp.where(col < ctx, sc, -jnp.inf)
        m_prev = m_ref[s]
        m_new = jnp.maximum(m_prev, jnp.max(sc, axis=-1, keepdims=True))
        alpha = jnp.exp2(m_prev - m_new)
        p = jnp.exp2(sc - m_new)
        l_ref[s] = alpha * l_ref[s] + jnp.sum(p, axis=-1, keepdims=True)
        acc_ref[s] = alpha * acc_ref[s] + jnp.dot(p.astype(v.dtype), v, preferred_element_type=F32)
        m_ref[s] = m_new


def _flash_online_kernel(q_ref, kt_ref, v_ref, *rest, n_sub, per_sub_k, tq, tk, ctx, nkv, lam_init):
    if lam_init is None:
        lam_ref = subln_ref = None
        o_ref, m_ref, l_ref, acc_ref = rest
    else:
        lam_ref, subln_ref, o_ref, m_ref, l_ref, acc_ref = rest
    qi = pl.program_id(1)
    ki = pl.program_id(2)
    q_is_ctx = (qi + 1) * tq <= ctx
    step = functools.partial(_flash_online_step, q_ref, kt_ref, v_ref, m_ref, l_ref, acc_ref,
                             n_sub=n_sub, per_sub_k=per_sub_k, tk=tk, ctx=ctx)

    @pl.when(ki == 0)
    def _():
        m_ref[...] = jnp.full(m_ref.shape, -jnp.inf, F32)
        l_ref[...] = jnp.zeros_like(l_ref)
        acc_ref[...] = jnp.zeros_like(acc_ref)

    @pl.when(jnp.logical_not(q_is_ctx))
    def _():
        step(masked=False)

    @pl.when(jnp.logical_and(q_is_ctx, ki * tk < ctx))
    def _():
        step(masked=True)

    @pl.when(ki == nkv - 1)
    def _():
        _flash_finalize(o_ref, [l_ref[s] for s in range(n_sub)], acc_ref, lam_ref, subln_ref, n_sub=n_sub,
                        lam_init=lam_init)


def flash_attention(qk, v, ctx, score_bound, *, groups, n_sub, per_sub_k, lam=None, subln=None, lam_init=None,
                    name="flash"):
    t = qk.shape[0]
    g = groups
    qw = n_sub * HEAD_DIM
    kw = qk.shape[1] // g - qw
    dv = v.shape[1] // g
    ow = dv if lam_init is not None else qw
    tq = _tile(ctx, (256, 128, 64, 32, 16, 8))
    tk = _tile(t, (1280, 1024, 768, 512, 256, 128))
    assert t % tq == 0 and ctx % tq == 0 and ctx <= tk and ctx % V7X_LANES == 0
    assert per_sub_k == (lam_init is not None)
    nkv = t // tk
    n_ctx_q = ctx // tq
    extra_specs3 = extra_specs2 = []
    extra_inputs = []
    if lam_init is not None:
        extra_specs3 = [pl.BlockSpec(lam.shape, lambda h, qi, ki: (0, 0)),
                        pl.BlockSpec((1, dv), lambda h, qi, ki: (0, 0))]
        extra_specs2 = [pl.BlockSpec(lam.shape, lambda h, qi: (0, 0)), pl.BlockSpec((1, dv), lambda h, qi: (0, 0))]
        extra_inputs = [lam, subln.reshape(1, dv)]
    out_shape = jax.ShapeDtypeStruct((t, g * ow), BF16)
    tile_bytes = _nbytes((tq, qw), BF16) + _nbytes((tq, ow), BF16)
    temp_bytes = 4 * _nbytes((tq, tk), F32) + n_sub * _nbytes((tq, dv), F32)
    once = pl.Buffered(1)

    def keys_t(qk):
        return qk[:, g * qw:].T

    def direct_per_sub_k(qk, v):
        tqd = next(c for c in (1280, 768, 512, 256, tq) if t % c == 0 and c % ctx == 0)
        kt_chunks = keys_t(qk).reshape(g, kw, nkv, tk).transpose(0, 2, 1, 3)
        blocks = _nbytes((tqd, qw), BF16) + _nbytes((tqd, ow), BF16)
        resident = _nbytes((kw, t), BF16) + _nbytes((t, dv), BF16)
        scratch_bytes = n_sub * (_nbytes((tqd, V7X_LANES), F32) + _nbytes((tqd, dv), F32))
        return pl.pallas_call(
            functools.partial(_flash_direct_kernel, n_sub=n_sub, tq=tqd, tc=tk, ctx=ctx, n_chunks=nkv,
                              lam_init=lam_init),
            out_shape=out_shape,
            grid=(g, t // tqd),
            in_specs=[pl.BlockSpec((tqd, qw), lambda h, qi: (qi, h)),
                      pl.BlockSpec((None, nkv, kw, tk), lambda h, qi: (h, 0, 0, 0), pipeline_mode=once),
                      pl.BlockSpec((t, dv), lambda h, qi: (0, h), pipeline_mode=once)] + extra_specs2,
            out_specs=pl.BlockSpec((tqd, ow), lambda h, qi: (qi, h)),
            scratch_shapes=[pltpu.VMEM((n_sub, tqd, V7X_LANES), F32), pltpu.VMEM((n_sub, tqd, dv), F32)],
            compiler_params=pltpu.CompilerParams(
                dimension_semantics=("arbitrary", "arbitrary"),
                vmem_limit_bytes=_vmem_limit(
                    blocks, resident + scratch_bytes + 3 * _nbytes((tqd, tk), F32) + 3 * _nbytes((tqd, dv), F32))),
            name=name + "_direct",
        )(qk, kt_chunks, v, *extra_inputs)

    def direct_shared_k(qk, v):
        cols = n_sub * tq
        qt = qk[:, :g * qw].T
        vt_chunks = v.T.reshape(g, dv, nkv, tk).transpose(0, 2, 1, 3)
        k_col0 = g * qw // kw
        blocks = 2 * _nbytes((qw, tq), BF16)
        resident = _nbytes((t, kw), BF16) + _nbytes((dv, t), BF16)
        scratch_bytes = _nbytes((HEAD_DIM, cols), BF16) + _nbytes((8, cols), F32) + _nbytes((dv, cols), F32)
        out_t = pl.pallas_call(
            functools.partial(_flash_direct_shared_kernel, n_sub=n_sub, tq=tq, tc=tk, ctx=ctx, n_chunks=nkv),
            out_shape=jax.ShapeDtypeStruct((g * qw, t), BF16),
            grid=(g, t // tq),
            in_specs=[pl.BlockSpec((qw, tq), lambda h, qi: (h, qi)),
                      pl.BlockSpec((t, kw), lambda h, qi: (0, k_col0 + h), pipeline_mode=once),
                      pl.BlockSpec((None, nkv, dv, tk), lambda h, qi: (h, 0, 0, 0), pipeline_mode=once)],
            out_specs=pl.BlockSpec((qw, tq), lambda h, qi: (h, qi)),
            scratch_shapes=[pltpu.VMEM((HEAD_DIM, cols), BF16), pltpu.VMEM((1, cols), F32),
                            pltpu.VMEM((dv, cols), F32)],
            compiler_params=pltpu.CompilerParams(
                dimension_semantics=("arbitrary", "arbitrary"),
                vmem_limit_bytes=_vmem_limit(blocks, resident + scratch_bytes + 3 * _nbytes((tk, cols), F32))),
            name=name + "_direct",
        )(qt, qk, vt_chunks)
        return out_t.T

    def online(qk, v):
        kt = keys_t(qk)
        q = qk
        last_ctx_kv = (ctx - 1) // tk

        def kv_index(qi, ki):
            return jnp.where(qi < n_ctx_q, jnp.minimum(ki, last_ctx_kv), ki)

        blocks = tile_bytes + _nbytes((kw, tk), BF16) + _nbytes((tk, dv), BF16)
        return pl.pallas_call(
            functools.partial(_flash_online_kernel, n_sub=n_sub, per_sub_k=per_sub_k, tq=tq, tk=tk, ctx=ctx, nkv=nkv,
                              lam_init=lam_init),
            out_shape=out_shape,
            grid=(g, t // tq, nkv),
            in_specs=[pl.BlockSpec((tq, qw), lambda h, qi, ki: (qi, h)),
                      pl.BlockSpec((kw, tk), lambda h, qi, ki: (h, kv_index(qi, ki))),
                      pl.BlockSpec((tk, dv), lambda h, qi, ki: (kv_index(qi, ki), h))] + extra_specs3,
            out_specs=pl.BlockSpec((tq, ow), lambda h, qi, ki: (qi, h)),
            scratch_shapes=[pltpu.VMEM((n_sub, tq, 1), F32), pltpu.VMEM((n_sub, tq, 1), F32),
                            pltpu.VMEM((n_sub, tq, dv), F32)],
            compiler_params=pltpu.CompilerParams(
                dimension_semantics=("arbitrary", "arbitrary", "arbitrary"),
                vmem_limit_bytes=_vmem_limit(blocks, temp_bytes + 2 * n_sub * _nbytes((tq, V7X_LANES), F32))),
            name=name + "_online",
        )(q, kt, v, *extra_inputs)

    direct = direct_per_sub_k if per_sub_k else direct_shared_k
    return lax.cond(score_bound <= SCORE_BOUND_DIRECT, direct, online, qk, v)


def _seq_neighbors(buf_ref, cur, prev8, next8, rows, row0, ctx, total):
    buf_ref[0:V7X_SUBLANES, :] = prev8
    buf_ref[V7X_SUBLANES:V7X_SUBLANES + rows, :] = cur
    buf_ref[V7X_SUBLANES + rows:, :] = next8
    t = row0 + lax.broadcasted_iota(jnp.int32, (rows, 1), 0)
    has_prev = jnp.logical_and(t != 0, t != ctx)
    has_next = jnp.logical_and(t != ctx - 1, t != total - 1)
    before = jnp.where(has_prev, buf_ref[V7X_SUBLANES - 1:V7X_SUBLANES - 1 + rows, :], 0.0)
    after = jnp.where(has_next, buf_ref[V7X_SUBLANES + 1:V7X_SUBLANES + 1 + rows, :], 0.0)
    return before, after


BF16_ROWS = 16


def _ffn_in_kernel(h_ref, hp_ref, hn_ref, wg_ref, wu_ref, cw_ref, cb_ref, o_ref, abuf, gbuf, *, tm, ctx, total):
    abuf[0:BF16_ROWS, :] = hp_ref[...]
    abuf[BF16_ROWS:BF16_ROWS + tm, :] = h_ref[...]
    abuf[BF16_ROWS + tm:, :] = hn_ref[...]
    gbuf[...] = jnp.dot(abuf[...], wg_ref[...], preferred_element_type=F32)
    up = jnp.dot(h_ref[...], wu_ref[...], preferred_element_type=F32)
    t = pl.program_id(1) * tm + lax.broadcasted_iota(jnp.int32, (tm, 1), 0)
    has_prev = jnp.logical_and(t != 0, t != ctx)
    has_next = jnp.logical_and(t != ctx - 1, t != total - 1)
    before = jnp.where(has_prev, gbuf[BF16_ROWS - 1:BF16_ROWS - 1 + tm, :], 0.0)
    after = jnp.where(has_next, gbuf[BF16_ROWS + 1:BF16_ROWS + 1 + tm, :], 0.0)
    conv = (before * cw_ref[0:1, :] + gbuf[BF16_ROWS:BF16_ROWS + tm, :] * cw_ref[1:2, :] + after * cw_ref[2:3, :]
            + cb_ref[...])
    o_ref[...] = (jax.nn.silu(conv) * up).astype(o_ref.dtype)


def ffn_in_glu(h, w_gate, w_up, conv_w, conv_b, ctx):
    t, d = h.shape
    fp = w_gate.shape[1]
    tm, tn, _ = _mm_tiles(t, d, fp)
    hb = tm // BF16_ROWS
    last = t // BF16_ROWS - 1
    once = pl.Buffered(1)
    blocks = _nbytes((tm + 2 * BF16_ROWS, d), BF16) + _nbytes((tm, tn), BF16) + 4 * _nbytes((8, tn), F32)
    resident = 2 * _nbytes((d, tn), BF16)
    scratch = _nbytes((tm + 2 * BF16_ROWS, d), BF16) + _nbytes((tm + 2 * BF16_ROWS, tn), F32)
    return pl.pallas_call(
        functools.partial(_ffn_in_kernel, tm=tm, ctx=ctx, total=t),
        out_shape=jax.ShapeDtypeStruct((t, fp), BF16),
        grid=(fp // tn, t // tm),
        in_specs=[
            pl.BlockSpec((tm, d), lambda j, i: (i, 0)),
            pl.BlockSpec((BF16_ROWS, d), lambda j, i: (jnp.maximum(i * hb - 1, 0), 0)),
            pl.BlockSpec((BF16_ROWS, d), lambda j, i: (jnp.minimum((i + 1) * hb, last), 0)),
            pl.BlockSpec((d, tn), lambda j, i: (0, j), pipeline_mode=once),
            pl.BlockSpec((d, tn), lambda j, i: (0, j), pipeline_mode=once),
            pl.BlockSpec((3, tn), lambda j, i: (0, j)),
            pl.BlockSpec((1, tn), lambda j, i: (0, j)),
        ],
        out_specs=pl.BlockSpec((tm, tn), lambda j, i: (i, j)),
        scratch_shapes=[pltpu.VMEM((tm + 2 * BF16_ROWS, d), BF16), pltpu.VMEM((tm + 2 * BF16_ROWS, tn), F32)],
        compiler_params=pltpu.CompilerParams(
            dimension_semantics=("arbitrary", "arbitrary"),
            vmem_limit_bytes=_vmem_limit(blocks, resident + scratch + 5 * _nbytes((tm, tn), F32))),
        name="ffn_in_glu",
    )(h, h, h, w_gate, w_up, conv_w, conv_b.reshape(1, fp))


def _rw_mix_kernel(x_ref, xp_ref, xn_ref, g_ref, sh_ref, sc_ref, mix_ref, *rest, rows, ctx, total):
    outs, buf_ref = rest[:6], rest[6]
    row0 = pl.program_id(0) * rows
    g, sh, sc = g_ref[...], sh_ref[...], sc_ref[...]

    def nm(x, first_row):
        t = first_row + lax.broadcasted_iota(jnp.int32, (x.shape[0], 1), 0)
        return _norm_mod_rows(x, g, sh, sc, t < ctx)

    h = nm(x_ref[...], row0)
    hp = nm(xp_ref[...], row0 - V7X_SUBLANES)
    hn = nm(xn_ref[...], row0 + rows)
    before, after = _seq_neighbors(buf_ref, h, hp, hn, rows, row0, ctx, total)
    xx = 0.5 * (before + after) - h
    for n in range(6):
        outs[n][...] = (h + xx * mix_ref[n:n + 1, :]).astype(outs[n].dtype)


def rw_token_mix(x, g, shift, scale, mix, ctx):
    t, d = x.shape
    rows = _tile(t, (128, 64, 32, 16, 8))
    rb = rows // V7X_SUBLANES
    last8 = t // V7X_SUBLANES - 1
    blocks = _nbytes((rows + 16, d), F32) + 6 * _nbytes((rows, d), BF16) + 4 * _nbytes((8, d), F32)
    return pl.pallas_call(
        functools.partial(_rw_mix_kernel, rows=rows, ctx=ctx, total=t),
        out_shape=[jax.ShapeDtypeStruct((t, d), BF16)] * 6,
        grid=(t // rows,),
        in_specs=[
            pl.BlockSpec((rows, d), lambda i: (i, 0)),
            pl.BlockSpec((V7X_SUBLANES, d), lambda i: (jnp.maximum(i * rb - 1, 0), 0)),
            pl.BlockSpec((V7X_SUBLANES, d), lambda i: (jnp.minimum((i + 1) * rb, last8), 0)),
            pl.BlockSpec((1, d), lambda i: (0, 0)),
            pl.BlockSpec((2, d), lambda i: (0, 0)),
            pl.BlockSpec((2, d), lambda i: (0, 0)),
            pl.BlockSpec((6, d), lambda i: (0, 0)),
        ],
        out_specs=[pl.BlockSpec((rows, d), lambda i: (i, 0))] * 6,
        scratch_shapes=[pltpu.VMEM((rows + 2 * V7X_SUBLANES, d), F32)],
        compiler_params=pltpu.CompilerParams(
            dimension_semantics=("arbitrary",),
            vmem_limit_bytes=_vmem_limit(blocks, 6 * _nbytes((rows + 16, d), F32))),
        name="rw_token_mix",
    )(x, x, x, g.reshape(1, d), shift, scale, mix)


SCAN_STEPS = 16
PREP_UNROLL = 4


def _mirror_block(i, n_blocks, n_ctx_blocks):
    return jnp.where(i < n_ctx_blocks, n_ctx_blocks - 1 - i, n_blocks + n_ctx_blocks - 1 - i)


def _scan_specs(t, chans, lanes, ctx):
    assert ctx % SCAN_STEPS == 0 and t % SCAN_STEPS == 0
    nb, nbc = t // SCAN_STEPS, ctx // SCAN_STEPS
    here = pl.BlockSpec((SCAN_STEPS, chans, lanes), lambda i: (i, 0, 0))
    mirror = pl.BlockSpec((SCAN_STEPS, chans, lanes), lambda i: (_mirror_block(i, nb, nbc), 0, 0))
    return nb, here, mirror


def _rw_prep_kernel(ra, rb, ka, kb, va, vb, wla, wlb, ala, alb, w0_ref, a0_ref, kk_ref, ka_ref, rk_ref,
                    w_o, a_o, b_o, kd_o, wr_o, v_o, bonus_o, br_o, kr_o):
    lanes = wla.shape[-1]
    half = lanes // 2
    is_fwd = lax.broadcasted_iota(jnp.int32, (1, lanes), 1) < half

    def body(s, carry):
        sb = SCAN_STEPS - 1 - s

        def pick(xa, xb):
            return jnp.where(is_fwd, xa[s], xb[sb])

        def join(xa, xb):
            fwd, bwd = xa[s], xb[sb]
            low = jnp.where(is_fwd, fwd, pltpu.roll(bwd, half, 1))
            high = jnp.where(is_fwd, pltpu.roll(fwd, half, 1), bwd)
            return jnp.concatenate([low, high], axis=0)

        r, k, v = join(ra, rb), join(ka, kb), join(va, vb)
        kk = k * kk_ref[...]
        kk = kk * lax.rsqrt(jnp.maximum(jnp.sum(kk * kk, axis=0, keepdims=True), 1e-24))
        w = jnp.exp(-jnp.exp(-jax.nn.softplus(-(w0_ref[...] + pick(wla, wlb))) - 0.5))
        iclr = jax.nn.sigmoid(a0_ref[...] + pick(ala, alb))
        kd = k * (1.0 + (iclr - 1.0) * ka_ref[...])
        b = kk * iclr
        w_o[s] = w
        a_o[s] = -kk
        b_o[s] = b
        kd_o[s] = kd
        wr_o[s] = w * r
        v_o[s] = v
        br_o[s] = jnp.sum(b * r, axis=0, keepdims=True)
        kr_o[s] = jnp.sum(kd * r, axis=0, keepdims=True)
        bonus_o[s] = jnp.sum(r * kd * rk_ref[...], axis=0, keepdims=True) * v
        return carry

    lax.fori_loop(0, SCAN_STEPS, body, 0, unroll=PREP_UNROLL)


def _wkv_kernel(*refs, chans):
    prep_in, (y_ref, bonus_ref), scratch = refs[:15], refs[15:17], refs[17:]
    w_ref, a_ref, b_ref, kd_ref, wr_ref, v_ref, br_ref, kr_ref, s_ref = scratch
    _rw_prep_kernel(*prep_in, w_ref, a_ref, b_ref, kd_ref, wr_ref, v_ref, bonus_ref, br_ref, kr_ref)

    @pl.when(pl.program_id(0) == 0)
    def _():
        s_ref[...] = jnp.zeros_like(s_ref)

    def reduce_state(i):
        sa = jnp.zeros(s_ref.shape[1:], F32)
        u = jnp.zeros(s_ref.shape[1:], F32)
        for c in range(chans):
            sc = s_ref[c]
            sa = sa + sc * a_ref[i, c:c + 1, :]
            u = u + sc * wr_ref[i, c:c + 1, :]
        return sa, u

    def emit(i, sa, u, vv):
        y_ref[i] = u + sa * br_ref[i] + vv * kr_ref[i]

    def update(i, c, sa, vv):
        return s_ref[c] * w_ref[i, c:c + 1, :] + sa * b_ref[i, c:c + 1, :] + vv * kd_ref[i, c:c + 1, :]

    def step(i, carry):
        sa, u = carry
        vv = v_ref[i]
        emit(i, sa, u, vv)
        sa_next = jnp.zeros(s_ref.shape[1:], F32)
        u_next = jnp.zeros(s_ref.shape[1:], F32)
        for c in range(chans):
            sn = update(i, c, sa, vv)
            s_ref[c] = sn
            sa_next = sa_next + sn * a_ref[i + 1, c:c + 1, :]
            u_next = u_next + sn * wr_ref[i + 1, c:c + 1, :]
        return sa_next, u_next

    last = SCAN_STEPS - 1
    sa, u = lax.fori_loop(0, last, step, reduce_state(0))
    vv = v_ref[last]
    emit(last, sa, u, vv)
    for c in range(chans):
        s_ref[c] = update(last, c, sa, vv)


def wkv_scan(r, k, v, wl, al, w0, a0, k_k, k_a, r_k, ctx):
    t, chans, lanes = wl.shape
    nb, here, mirror = _scan_specs(t, chans, lanes, ctx)
    _, here_h, mirror_h = _scan_specs(t, chans // 2, lanes, ctx)
    par = pl.BlockSpec((chans, lanes), lambda i: (0, 0))
    big = jax.ShapeDtypeStruct((t, chans, lanes), F32)
    step_block = pltpu.VMEM((SCAN_STEPS, chans, lanes), F32)
    step_row = pltpu.VMEM((SCAN_STEPS, 1, lanes), F32)
    blocks = 12 * _nbytes((SCAN_STEPS, chans, lanes), F32) + 5 * _nbytes((chans, lanes), F32)
    scratch_bytes = (6 * _nbytes((SCAN_STEPS, chans, lanes), F32) + 2 * _nbytes((SCAN_STEPS, 8, lanes), F32)
                     + _nbytes((chans, chans, lanes), F32))
    return pl.pallas_call(
        functools.partial(_wkv_kernel, chans=chans),
        out_shape=[big, big],
        grid=(nb,),
        in_specs=[here_h, mirror_h] * 3 + [here, mirror] * 2 + [par] * 5,
        out_specs=[here, here],
        scratch_shapes=[step_block] * 6 + [step_row] * 2 + [pltpu.VMEM((chans, chans, lanes), F32)],
        compiler_params=_params(("arbitrary",), blocks, scratch_bytes + 16 * _nbytes((chans, lanes), F32)),
        name="wkv_scan",
    )(r, r, k, k, v, v, wl, wl, al, al, w0, a0, k_k, k_a, r_k)


def _rw_finish_kernel(ya, yb, ba, bb, gate_ref, lnw_ref, lnb_ref, o_ref):
    lanes = ya.shape[-1]
    half = lanes // 2
    packed = ya.shape[1] // 2
    is_low = lax.broadcasted_iota(jnp.int32, (1, lanes), 1) < half

    def body(s, carry):
        sb = SCAN_STEPS - 1 - s
        y = ya[s] + pltpu.roll(yb[sb], half, 1)
        bonus = ba[s] + pltpu.roll(bb[sb], half, 1)
        mu = jnp.mean(y, axis=0, keepdims=True)
        yc = y - mu
        var = jnp.mean(yc * yc, axis=0, keepdims=True)
        z = yc * lax.rsqrt(var + RW_LN_EPS) * lnw_ref[...] + lnb_ref[...] + bonus
        z_packed = jnp.where(is_low, z[:packed], pltpu.roll(z[packed:], half, 1))
        o_ref[s] = (z_packed * gate_ref[s]).astype(o_ref.dtype)
        return carry

    lax.fori_loop(0, SCAN_STEPS, body, 0, unroll=PREP_UNROLL)


def rw_finish(y, bonus, gate, ln_w, ln_b, ctx):
    t, chans, lanes = y.shape
    nb, here, mirror = _scan_specs(t, chans, lanes, ctx)
    half = pl.BlockSpec((SCAN_STEPS, chans // 2, lanes), lambda i: (i, 0, 0))
    par = pl.BlockSpec((chans, lanes), lambda i: (0, 0))
    blocks = 6 * _nbytes((SCAN_STEPS, chans, lanes), F32) + 2 * _nbytes((chans, lanes), F32)
    return pl.pallas_call(
        _rw_finish_kernel,
        out_shape=jax.ShapeDtypeStruct((t, chans // 2, lanes), BF16),
        grid=(nb,),
        in_specs=[here, mirror, here, mirror, half, par, par],
        out_specs=half,
        compiler_params=_params(("arbitrary",), blocks, 8 * _nbytes((chans, lanes), F32)),
        name="rw_finish",
    )(y, y, bonus, bonus, gate, ln_w, ln_b)


def _rope_tables(ctx, seq):
    pos = jnp.arange(seq)
    row = (pos // GRID_W).astype(F32)
    col = (pos % GRID_W).astype(F32)
    n_freq = HEAD_DIM // 4
    inv_freq = ROPE_THETA ** (-jnp.arange(n_freq, dtype=F32) / n_freq)
    ang = jnp.concatenate([row[:, None] * inv_freq, col[:, None] * inv_freq], axis=-1)
    cos = jnp.concatenate([jnp.ones((ctx, HEAD_DIM // 2), F32), jnp.cos(ang)], axis=0)
    sin = jnp.concatenate([jnp.zeros((ctx, HEAD_DIM // 2), F32), jnp.sin(ang)], axis=0)
    return jnp.concatenate([cos, cos], axis=-1), jnp.concatenate([-sin, sin], axis=-1)


def _channel_major(w):
    heads = w.shape[-1] // RW_HEAD_DIM
    return jnp.swapaxes(w.reshape(w.shape[:-1] + (heads, RW_HEAD_DIM)), -1, -2)


def _channel_packed(w):
    heads = w.shape[-1] // RW_HEAD_DIM
    w4 = w.reshape(w.shape[:-1] + (heads, 2, RW_HEAD_DIM // 2))
    return jnp.moveaxis(w4, -3, -1).swapaxes(-3, -2).reshape(w.shape)


def _per_direction(w2):
    cm = _channel_major(w2)
    zero = jnp.zeros_like(cm[0])
    top = jnp.concatenate([cm[0], zero], axis=-1)
    bottom = jnp.concatenate([zero, cm[1]], axis=-1)
    return jnp.concatenate([top, bottom], axis=0).reshape(2 * w2.shape[1], -1)


def _score_gain(q_g, k_g):
    q_gain = q_g * (HEAD_DIM ** -0.5 * math.log2(math.e))
    bound = HEAD_DIM * jnp.max(jnp.abs(q_gain)) * jnp.max(jnp.abs(k_g))
    return q_gain, bound


def _diff_lambda_init(layer_idx):
    return 0.8 - 0.6 * math.exp(-0.3 * layer_idx)


def _diff_attention(h, cc, ss, wqkv, wo, q_g, k_g, lam_vecs, subln_g, lam_init, x, gate, ctx):
    d = h.shape[1]
    n_heads = d // HEAD_DIM
    q_gain, bound = _score_gain(q_g, k_g)
    gain = jnp.concatenate([jnp.tile(q_gain, n_heads), jnp.tile(k_g, n_heads)])
    qk = matmul_headnorm_rope(h, wqkv[:, :2 * d].astype(BF16), gain, cc, ss, name="da_qk_proj")
    v = matmul(h, wqkv[:, 2 * d:].astype(BF16), BF16, name="da_v_proj")
    o = flash_attention(qk, v, ctx, bound, groups=n_heads // 2, n_sub=2, per_sub_k=True, lam=lam_vecs,
                        subln=subln_g, lam_init=lam_init, name="da_flash")
    return matmul_residual(o, wo.astype(BF16), x, gate, ctx, name="da_out_proj")


def _gqa_attention(h, cc, ss, wqkv, wo, q_g, k_g, x, gate, ctx):
    d = h.shape[1]
    kv_w = d // GA_GROUP
    q_gain, bound = _score_gain(q_g, k_g)
    gain = jnp.concatenate([jnp.tile(q_gain, d // HEAD_DIM), jnp.tile(k_g, kv_w // HEAD_DIM)])
    qk = matmul_headnorm_rope(h, wqkv[:, :d + kv_w].astype(BF16), gain, cc, ss, name="ga_qk_proj")
    v = matmul(h, wqkv[:, d + kv_w:].astype(BF16), BF16, name="ga_v_proj")
    o = flash_attention(qk, v, ctx, bound, groups=kv_w // HEAD_DIM, n_sub=GA_GROUP, per_sub_k=False,
                        name="ga_flash")
    return matmul_residual(o, wo.astype(BF16), x, gate, ctx, name="ga_out_proj")


def _pad_cols(w, n):
    return jnp.pad(w, ((0, 0), (0, n - w.shape[1])))


def _rwkv7(x, g, shift, scale, mix, wrkv, wo, w0, w1, w2, a0, a1, a2, g1, g2, k_k, k_a, r_k, ln_w, ln_b, gate,
           ctx):
    t, d = x.shape
    heads = d // RW_HEAD_DIM
    scan_shape = (t, RW_HEAD_DIM, 2 * heads)
    packed_shape = (t, RW_HEAD_DIM // 2, 2 * heads)
    xr, xw, xk, xv, xa, xg = rw_token_mix(x, g, shift, scale, mix, ctx)

    def shared(xin, w, name):
        return matmul(xin, _channel_packed(w).astype(BF16), F32, name=name).reshape(packed_shape)

    r = shared(xr, wrkv[0], "rw_r_proj")
    k = shared(xk, wrkv[1], "rw_k_proj")
    v = shared(xv, wrkv[2], "rw_v_proj")
    hw = matmul(xw, jnp.concatenate([w1[0], w1[1]], axis=1).astype(BF16), BF16, act="tanh", name="rw_w_lora_in")
    ha = matmul(xa, jnp.concatenate([a1[0], a1[1]], axis=1).astype(BF16), BF16, name="rw_a_lora_in")
    wl = matmul(hw, _per_direction(w2).astype(BF16), F32, name="rw_w_lora_out").reshape(scan_shape)
    al = matmul(ha, _per_direction(a2).astype(BF16), F32, name="rw_a_lora_out").reshape(scan_shape)
    lg = -(-g1.shape[1] // V7X_LANES) * V7X_LANES
    hg = matmul(xg, _pad_cols(g1, lg).astype(BF16), BF16, act="sigmoid", name="rw_g_lora_in")
    g2p = jnp.pad(_channel_packed(g2), ((0, lg - g2.shape[0]), (0, 0))).astype(BF16)
    out_gate = matmul(hg, g2p, F32, name="rw_g_lora_out").reshape(packed_shape)

    def dir_param(p):
        return jnp.concatenate([_channel_major(p[0]), _channel_major(p[1])], axis=-1)

    def shared_param(p):
        return jnp.concatenate([_channel_major(p)] * 2, axis=-1)

    y, bonus = wkv_scan(r, k, v, wl, al, dir_param(w0), dir_param(a0), shared_param(k_k), shared_param(k_a),
                        shared_param(r_k.reshape(d)), ctx)
    z = rw_finish(y, bonus, out_gate, shared_param(ln_w), shared_param(ln_b), ctx).reshape(t, d)
    wo_packed = _channel_packed(wo.T).T
    return matmul_residual(z, wo_packed.astype(BF16), x, gate, ctx, name="rw_out_proj")


def _conv_glu(h, w_in, conv_w, conv_b, w_out, x, gate, ctx):
    f = conv_w.shape[1]
    fp = -(-f // FFN_PAD) * FFN_PAD
    act = ffn_in_glu(h, _pad_cols(w_in[:, :f], fp).astype(BF16), _pad_cols(w_in[:, f:], fp).astype(BF16),
                     _pad_cols(conv_w, fp), jnp.pad(conv_b, (0, fp - f)), ctx)
    w_out_p = jnp.pad(w_out, ((0, fp - f), (0, 0))).astype(BF16)
    return matmul_residual(act, w_out_p, x, gate, ctx, name="ffn_out_proj")


def kernel(x, c, ctx, c_ctx, ada_down, ada_up, ada_b, norm_g, ffn_in, ffn_conv, ffn_conv_b, ffn_out, da_wqkv, da_wo, da_q_g, da_k_g, da_lambda, da_subln_g, rw_mix, rw_wrkv, rw_wo, rw_w0, rw_w1, rw_w2, rw_a0, rw_a1, rw_a2, rw_g1, rw_g2, rw_k_k, rw_k_a, rw_r_k, rw_ln_w, rw_ln_b, ga_wqkv, ga_wo, ga_q_g, ga_k_g):
    assert x.shape[0] == 1
    seq, d = x.shape[1], x.shape[2]
    n_ctx = ctx.shape[1]
    depth = ada_down.shape[0]
    stream = jnp.concatenate([ctx[0], x[0]], axis=0)
    cond = jnp.zeros((V7X_SUBLANES, d), F32).at[0].set(c_ctx).at[1].set(c[0])
    mod = ada_modulation(cond, ada_down, ada_up, ada_b)[:, :2].reshape(depth, 2, N_MOD, d)
    cc, ss = _rope_tables(n_ctx, seq)
    for i in range(depth):
        kind, j = i % N_MIXERS, i // N_MIXERS
        m = [mod[i, :, n, :] for n in range(N_MOD)]
        if kind == 1:
            stream = _rwkv7(stream, norm_g[i, 0], m[0], m[1], rw_mix[j], rw_wrkv[j], rw_wo[j], rw_w0[j], rw_w1[j],
                            rw_w2[j], rw_a0[j], rw_a1[j], rw_a2[j], rw_g1[j], rw_g2[j], rw_k_k[j], rw_k_a[j],
                            rw_r_k[j], rw_ln_w[j], rw_ln_b[j], m[2], n_ctx)
        else:
            h = norm_mod(stream, norm_g[i, 0], m[0], m[1], n_ctx)
            if kind == 0:
                stream = _diff_attention(h, cc, ss, da_wqkv[j], da_wo[j], da_q_g[j], da_k_g[j], da_lambda[j],
                                         da_subln_g[j], _diff_lambda_init(i), stream, m[2], n_ctx)
            else:
                stream = _gqa_attention(h, cc, ss, ga_wqkv[j], ga_wo[j], ga_q_g[j], ga_k_g[j], stream, m[2], n_ctx)
        h2 = norm_mod(stream, norm_g[i, 1], m[3], m[4], n_ctx)
        stream = _conv_glu(h2, ffn_in[i], ffn_conv[i], ffn_conv_b[i], ffn_out[i], stream, m[5], n_ctx)
    return stream[n_ctx:][None]
```

```python
import functools
import math

import jax
import jax.numpy as jnp
from jax import lax
from jax.experimental import pallas as pl
from jax.experimental.pallas import tpu as pltpu

F32 = jnp.float32
BF16 = jnp.bfloat16

N_MOD = 6
N_MIXERS = 3
GRID_W = 64
ROPE_THETA = 10000.0
NORM_EPS = 1e-6
HEAD_DIM = 128
GA_GROUP = 4
RW_HEAD_DIM = 64
RW_LN_EPS = 64e-5

V7X_LANES = 128
V7X_SUBLANES = 8
BF16_ROWS = 16
QK_ROW_SLABS = 4
V7X_VMEM_BYTES = 64 * 1024 * 1024
V7X_VMEM_RESERVE = 6 * 1024 * 1024
FFN_PAD = 1024


def _vmem_limit(block_bytes, temp_bytes=0):
    return int(min(2 * block_bytes + temp_bytes + V7X_VMEM_RESERVE, V7X_VMEM_BYTES - V7X_VMEM_RESERVE))


def _params(semantics, block_bytes, temp_bytes=0):
    return pltpu.CompilerParams(dimension_semantics=semantics,
                                vmem_limit_bytes=_vmem_limit(block_bytes, temp_bytes))


def _tile(n, prefs):
    for t in prefs:
        if n % t == 0:
            return t
    return n


def _nbytes(shape, dtype):
    return math.prod(shape) * jnp.dtype(dtype).itemsize


def _row_ids(i, rows, shape):
    return i * rows + lax.broadcasted_iota(jnp.int32, shape, 0)


def _ada_kernel(cond_ref, down_ref, up_ref, b_ref, o_ref):
    s = jax.nn.silu(cond_ref[...])
    t = jnp.dot(s, down_ref[...], preferred_element_type=F32, precision=lax.Precision.HIGHEST)
    m = jnp.dot(t, up_ref[...], preferred_element_type=F32, precision=lax.Precision.HIGHEST)
    o_ref[...] = m + b_ref[...]


def ada_modulation(cond, ada_down, ada_up, ada_b):
    depth, d, r = ada_down.shape
    n = ada_up.shape[-1]
    tn = _tile(n, (4096, 2048, 1024, 512))
    blocks = _nbytes((8, d), F32) + _nbytes((d, r), F32) + _nbytes((r, tn), F32) + 2 * _nbytes((8, tn), F32)
    return pl.pallas_call(
        _ada_kernel,
        out_shape=jax.ShapeDtypeStruct((depth, 8, n), F32),
        grid=(depth, n // tn),
        in_specs=[
            pl.BlockSpec((8, d), lambda l, j: (0, 0)),
            pl.BlockSpec((None, d, r), lambda l, j: (l, 0, 0)),
            pl.BlockSpec((None, r, tn), lambda l, j: (l, 0, j)),
            pl.BlockSpec((None, 1, tn), lambda l, j: (l, 0, j)),
        ],
        out_specs=pl.BlockSpec((None, 8, tn), lambda l, j: (l, 0, j)),
        compiler_params=_params(("arbitrary", "arbitrary"), blocks),
        name="ada_modulation",
    )(cond, ada_down, ada_up, ada_b.reshape(depth, 1, n))


def _norm_mod_rows(x, g, shift, scale, is_ctx):
    y = x * lax.rsqrt(jnp.mean(x * x, axis=-1, keepdims=True) + NORM_EPS) * g
    sc = jnp.where(is_ctx, scale[0:1, :], scale[1:2, :])
    sh = jnp.where(is_ctx, shift[0:1, :], shift[1:2, :])
    return y * (1.0 + sc) + sh


def _norm_mod_kernel(x_ref, g_ref, sh_ref, sc_ref, o_ref, *, rows, ctx):
    is_ctx = _row_ids(pl.program_id(0), rows, (rows, 1)) < ctx
    o_ref[...] = _norm_mod_rows(x_ref[...], g_ref[...], sh_ref[...], sc_ref[...], is_ctx).astype(o_ref.dtype)


def norm_mod(x, g, shift, scale, ctx):
    t, d = x.shape
    rows = _tile(t, (256, 128, 64, 32, 16, 8))
    blocks = _nbytes((rows, d), F32) + _nbytes((rows, d), BF16) + 5 * _nbytes((8, d), F32)
    return pl.pallas_call(
        functools.partial(_norm_mod_kernel, rows=rows, ctx=ctx),
        out_shape=jax.ShapeDtypeStruct((t, d), BF16),
        grid=(t // rows,),
        in_specs=[
            pl.BlockSpec((rows, d), lambda i: (i, 0)),
            pl.BlockSpec((1, d), lambda i: (0, 0)),
            pl.BlockSpec((2, d), lambda i: (0, 0)),
            pl.BlockSpec((2, d), lambda i: (0, 0)),
        ],
        out_specs=pl.BlockSpec((rows, d), lambda i: (i, 0)),
        compiler_params=_params(("arbitrary",), blocks, 4 * _nbytes((rows, d), F32)),
        name="norm_mod",
    )(x, g.reshape(1, d), shift, scale)


def _mm_plain_kernel(a_ref, b_ref, o_ref, *, act):
    acc = jnp.dot(a_ref[...], b_ref[...], preferred_element_type=F32)
    if act == "tanh":
        acc = jnp.tanh(acc)
    elif act == "sigmoid":
        acc = jax.nn.sigmoid(acc)
    o_ref[...] = acc.astype(o_ref.dtype)


def _mm_residual_kernel(a_ref, b_ref, x_ref, g_ref, o_ref, *, tm, ctx):
    acc = jnp.dot(a_ref[...], b_ref[...], preferred_element_type=F32)
    is_ctx = _row_ids(pl.program_id(1), tm, (tm, 1)) < ctx
    gate = jnp.where(is_ctx, g_ref[0:1, :], g_ref[1:2, :])
    o_ref[...] = x_ref[...] + gate * acc


def _mm_headnorm_rope_kernel(a_ref, b_ref, g_ref, cc_ref, ss_ref, o_ref, *, tn):
    tm = a_ref.shape[0]
    n_slabs = QK_ROW_SLABS if tm % (QK_ROW_SLABS * BF16_ROWS) == 0 else 1
    rows = tm // n_slabs
    for r in range(n_slabs):
        rs = slice(r * rows, (r + 1) * rows)
        cc = cc_ref[rs, :]
        ss = ss_ref[rs, :]
        acc = jnp.dot(a_ref[rs, :], b_ref[...], preferred_element_type=F32)
        for h in range(tn // HEAD_DIM):
            sl = slice(h * HEAD_DIM, (h + 1) * HEAD_DIM)
            z = acc[:, sl]
            y = z * lax.rsqrt(jnp.mean(z * z, axis=-1, keepdims=True) + NORM_EPS) * g_ref[:, sl]
            o_ref[rs, sl] = (y * cc + pltpu.roll(y, HEAD_DIM // 2, 1) * ss).astype(o_ref.dtype)


MM_WEIGHT_TILE_BYTES = 12 * 1024 * 1024
MM_DOUBLE_BUFFER_BYTES = 8 * 1024 * 1024


def _mm_tiles(m, k, n):
    tm = _tile(m, (640, 512, 384, 256, 128, 64, 32, 16, 8))
    tn = next(c for c in (1024, 512, 256, 128, n) if n % c == 0 and _nbytes((k, c), BF16) <= MM_WEIGHT_TILE_BYTES)
    return tm, tn, _nbytes((k, tn), BF16) > MM_DOUBLE_BUFFER_BYTES


def _mm_call(kernel, a, b, extra_inputs, extra_specs, out_dtype, extra_bytes, name):
    m, k = a.shape
    n = b.shape[1]
    tm, tn, deep = _mm_tiles(m, k, n)
    weights = _nbytes((k, tn), b.dtype)
    blocks = _nbytes((tm, k), a.dtype) + _nbytes((tm, tn), out_dtype) + extra_bytes(tm, tn)
    b_spec = pl.BlockSpec((k, tn), lambda j, i: (0, j), **({"pipeline_mode": pl.Buffered(1)} if deep else {}))
    return pl.pallas_call(
        kernel,
        out_shape=jax.ShapeDtypeStruct((m, n), out_dtype),
        grid=(n // tn, m // tm),
        in_specs=[pl.BlockSpec((tm, k), lambda j, i: (i, 0)), b_spec] + extra_specs(tm, tn),
        out_specs=pl.BlockSpec((tm, tn), lambda j, i: (i, j)),
        compiler_params=pltpu.CompilerParams(
            dimension_semantics=("arbitrary", "arbitrary"),
            vmem_limit_bytes=_vmem_limit(blocks, (1 if deep else 2) * weights + 2 * _nbytes((tm, tn), F32))),
        name=name,
    )(a, b, *extra_inputs)


def matmul(a, b, out_dtype, act=None, name="matmul"):
    return _mm_call(functools.partial(_mm_plain_kernel, act=act), a, b, (), lambda tm, tn: [],
                    out_dtype, lambda tm, tn: 0, name)


def matmul_residual(a, b, x, gate, ctx, name="matmul_residual"):
    tm = _mm_tiles(a.shape[0], a.shape[1], b.shape[1])[0]

    def specs(tm_, tn):
        return [pl.BlockSpec((tm_, tn), lambda j, i: (i, j)), pl.BlockSpec((2, tn), lambda j, i: (0, j))]

    return _mm_call(functools.partial(_mm_residual_kernel, tm=tm, ctx=ctx), a, b, (x, gate), specs, F32,
                    lambda tm_, tn: _nbytes((tm_, tn), F32) + _nbytes((8, tn), F32), name)


def matmul_headnorm_rope(a, b, gain, cc, ss, name="matmul_headnorm_rope"):
    m, k = a.shape
    n = b.shape[1]
    tm, tn, _ = _mm_tiles(m, k, n)
    blocks = (_nbytes((tm, k), a.dtype) + _nbytes((k, tn), b.dtype) + _nbytes((tm, tn), BF16)
              + _nbytes((8, tn), F32) + 2 * _nbytes((tm, HEAD_DIM), F32) + _nbytes((tm, tn), F32))
    return pl.pallas_call(
        functools.partial(_mm_headnorm_rope_kernel, tn=tn),
        out_shape=jax.ShapeDtypeStruct((m, n), BF16),
        grid=(n // tn, m // tm),
        in_specs=[
            pl.BlockSpec((tm, k), lambda j, i: (i, 0)),
            pl.BlockSpec((k, tn), lambda j, i: (0, j)),
            pl.BlockSpec((1, tn), lambda j, i: (0, j)),
            pl.BlockSpec((tm, HEAD_DIM), lambda j, i: (i, 0)),
            pl.BlockSpec((tm, HEAD_DIM), lambda j, i: (i, 0)),
        ],
        out_specs=pl.BlockSpec((tm, tn), lambda j, i: (i, j)),
        compiler_params=_params(("arbitrary", "arbitrary"), blocks, 2 * _nbytes((tm, tn), F32)),
        name=name,
    )(a, b, gain.reshape(1, n), cc, ss)


SCORE_BOUND_DIRECT = 64.0


def _flash_finalize(o_ref, l_rows, acc_ref, lam_ref, subln_ref, *, n_sub, lam_init):
    if lam_init is None:
        for s in range(n_sub):
            o_ref[:, s * HEAD_DIM:(s + 1) * HEAD_DIM] = (acc_ref[s] / l_rows[s]).astype(o_ref.dtype)
        return
    lv = lam_ref[...]
    lam = (jnp.exp(jnp.sum(lv[0:1, :] * lv[1:2, :], axis=-1, keepdims=True))
           - jnp.exp(jnp.sum(lv[2:3, :] * lv[3:4, :], axis=-1, keepdims=True)) + lam_init)
    o = acc_ref[0] / l_rows[0] - lam * (acc_ref[1] / l_rows[1])
    o = o * lax.rsqrt(jnp.mean(o * o, axis=-1, keepdims=True) + NORM_EPS) * subln_ref[...]
    o_ref[...] = (o * (1.0 - lam_init)).astype(o_ref.dtype)


def _flash_direct_kernel(q_ref, kt_ref, v_ref, lam_ref, subln_ref, o_ref, l_ref, acc_ref, *, n_sub, tq, tc, ctx,
                         n_chunks, lam_init):
    l_ref[...] = jnp.zeros_like(l_ref)
    acc_ref[...] = jnp.zeros_like(acc_ref)

    def accumulate(chunk, width, v, r0, nr):
        for s in range(n_sub):
            cols = slice(s * HEAD_DIM, (s + 1) * HEAD_DIM)
            p = jnp.exp2(jnp.dot(q_ref[r0:r0 + nr, cols], kt_ref[chunk, cols, 0:width], preferred_element_type=F32))
            part = p[:, 0:V7X_LANES]
            for j in range(1, width // V7X_LANES):
                part = part + p[:, j * V7X_LANES:(j + 1) * V7X_LANES]
            l_ref[s, r0:r0 + nr, :] += part
            acc_ref[s, r0:r0 + nr, :] += jnp.dot(p.astype(v.dtype), v, preferred_element_type=F32)

    def all_keys(r0, nr):
        def body(c, carry):
            accumulate(c, tc, v_ref[pl.ds(pl.multiple_of(c * tc, tc), tc), :], r0, nr)
            return carry

        lax.fori_loop(0, n_chunks, body, 0)

    @pl.when(pl.program_id(1) > 0)
    def _():
        all_keys(0, tq)

    @pl.when(pl.program_id(1) == 0)
    def _():
        accumulate(0, ctx, v_ref[0:ctx, :], 0, ctx)
        if ctx < tq:
            all_keys(ctx, tq - ctx)

    l_rows = [jnp.sum(l_ref[s], axis=-1, keepdims=True) for s in range(n_sub)]
    _flash_finalize(o_ref, l_rows, acc_ref, lam_ref, subln_ref, n_sub=n_sub, lam_init=lam_init)


def _flash_direct_shared_kernel(qt_ref, k_ref, vt_ref, ot_ref, qs_ref, l_ref, acc_ref, *, n_sub, tq, tc, ctx, n_chunks):
    for g in range(n_sub):
        qs_ref[:, g * tq:(g + 1) * tq] = qt_ref[g * HEAD_DIM:(g + 1) * HEAD_DIM, :]
    l_ref[...] = jnp.zeros_like(l_ref)
    acc_ref[...] = jnp.zeros_like(acc_ref)

    def accumulate(k, vt):
        pt = jnp.exp2(jnp.dot(k, qs_ref[...], preferred_element_type=F32))
        l_ref[...] += jnp.sum(pt, axis=0, keepdims=True)
        acc_ref[...] += jnp.dot(vt, pt.astype(vt.dtype), preferred_element_type=F32)

    @pl.when(pl.program_id(1) > 0)
    def _():
        def body(c, carry):
            accumulate(k_ref[pl.ds(pl.multiple_of(c * tc, tc), tc), :], vt_ref[c])
            return carry

        lax.fori_loop(0, n_chunks, body, 0)

    @pl.when(pl.program_id(1) == 0)
    def _():
        accumulate(k_ref[0:ctx, :], vt_ref[0, :, 0:ctx])

    out = acc_ref[...] / l_ref[...]
    for g in range(n_sub):
        ot_ref[g * HEAD_DIM:(g + 1) * HEAD_DIM, :] = out[:, g * tq:(g + 1) * tq].astype(ot_ref.dtype)


def _flash_online_step(q_ref, kt_ref, v_ref, m_ref, l_ref, acc_ref, *, n_sub, per_sub_k, tk, ctx, masked):
    ki = pl.program_id(2)
    v = v_ref[...]
    for s in range(n_sub):
        q = q_ref[:, s * HEAD_DIM:(s + 1) * HEAD_DIM]
        kt = kt_ref[s * HEAD_DIM:(s + 1) * HEAD_DIM, :] if per_sub_k else kt_ref[...]
        sc = jnp.dot(q, kt, preferred_element_type=F32)
        if masked:
            col = ki * tk + lax.broadcasted_iota(jnp.int32, sc.shape, 1)
            sc = jnp.where(col < ctx, sc, -jnp.inf)
        m_prev = m_ref[s]
        m_new = jnp.maximum(m_prev, jnp.max(sc, axis=-1, keepdims=True))
        alpha = jnp.exp2(m_prev - m_new)
        p = jnp.exp2(sc - m_new)
        l_ref[s] = alpha * l_ref[s] + jnp.sum(p, axis=-1, keepdims=True)
        acc_ref[s] = alpha * acc_ref[s] + jnp.dot(p.astype(v.dtype), v, preferred_element_type=F32)
        m_ref[s] = m_new


def _flash_online_kernel(q_ref, kt_ref, v_ref, *rest, n_sub, per_sub_k, tq, tk, ctx, nkv, lam_init):
    if lam_init is None:
        lam_ref = subln_ref = None
        o_ref, m_ref, l_ref, acc_ref = rest
    else:
        lam_ref, subln_ref, o_ref, m_ref, l_ref, acc_ref = rest
    qi = pl.program_id(1)
    ki = pl.program_id(2)
    q_is_ctx = (qi + 1) * tq <= ctx
    step = functools.partial(_flash_online_step, q_ref, kt_ref, v_ref, m_ref, l_ref, acc_ref,
                             n_sub=n_sub, per_sub_k=per_sub_k, tk=tk, ctx=ctx)

    @pl.when(ki == 0)
    def _():
        m_ref[...] = jnp.full(m_ref.shape, -jnp.inf, F32)
        l_ref[...] = jnp.zeros_like(l_ref)
        acc_ref[...] = jnp.zeros_like(acc_ref)

    @pl.when(jnp.logical_not(q_is_ctx))
    def _():
        step(masked=False)

    @pl.when(jnp.logical_and(q_is_ctx, ki * tk < ctx))
    def _():
        step(masked=True)

    @pl.when(ki == nkv - 1)
    def _():
        _flash_finalize(o_ref, [l_ref[s] for s in range(n_sub)], acc_ref, lam_ref, subln_ref, n_sub=n_sub,
                        lam_init=lam_init)


def flash_attention(qk, v, ctx, score_bound, *, groups, n_sub, per_sub_k, lam=None, subln=None, lam_init=None,
                    name="flash"):
    t = qk.shape[0]
    g = groups
    qw = n_sub * HEAD_DIM
    kw = qk.shape[1] // g - qw
    dv = v.shape[1] // g
    ow = dv if lam_init is not None else qw
    tq = _tile(ctx, (256, 128, 64, 32, 16, 8))
    tk = _tile(t, (1280, 1024, 768, 512, 256, 128))
    assert t % tq == 0 and ctx % tq == 0 and ctx <= tk and ctx % V7X_LANES == 0
    assert per_sub_k == (lam_init is not None)
    nkv = t // tk
    n_ctx_q = ctx // tq
    extra_specs3 = extra_specs2 = []
    extra_inputs = []
    if lam_init is not None:
        extra_specs3 = [pl.BlockSpec(lam.shape, lambda h, qi, ki: (0, 0)),
                        pl.BlockSpec((1, dv), lambda h, qi, ki: (0, 0))]
        extra_specs2 = [pl.BlockSpec(lam.shape, lambda h, qi: (0, 0)), pl.BlockSpec((1, dv), lambda h, qi: (0, 0))]
        extra_inputs = [lam, subln.reshape(1, dv)]
    out_shape = jax.ShapeDtypeStruct((t, g * ow), BF16)
    tile_bytes = _nbytes((tq, qw), BF16) + _nbytes((tq, ow), BF16)
    temp_bytes = 4 * _nbytes((tq, tk), F32) + n_sub * _nbytes((tq, dv), F32)
    once = pl.Buffered(1)

    def keys_t(qk):
        return qk[:, g * qw:].T

    def direct_per_sub_k(qk, v):
        tqd = next(c for c in (1280, 768, 512, 256, tq) if t % c == 0 and c % ctx == 0)
        kt_chunks = keys_t(qk).reshape(g, kw, nkv, tk).transpose(0, 2, 1, 3)
        blocks = _nbytes((tqd, qw), BF16) + _nbytes((tqd, ow), BF16)
        resident = _nbytes((kw, t), BF16) + _nbytes((t, dv), BF16)
        scratch_bytes = n_sub * (_nbytes((tqd, V7X_LANES), F32) + _nbytes((tqd, dv), F32))
        return pl.pallas_call(
            functools.partial(_flash_direct_kernel, n_sub=n_sub, tq=tqd, tc=tk, ctx=ctx, n_chunks=nkv,
                              lam_init=lam_init),
            out_shape=out_shape,
            grid=(g, t // tqd),
            in_specs=[pl.BlockSpec((tqd, qw), lambda h, qi: (qi, h)),
                      pl.BlockSpec((None, nkv, kw, tk), lambda h, qi: (h, 0, 0, 0), pipeline_mode=once),
                      pl.BlockSpec((t, dv), lambda h, qi: (0, h), pipeline_mode=once)] + extra_specs2,
            out_specs=pl.BlockSpec((tqd, ow), lambda h, qi: (qi, h)),
            scratch_shapes=[pltpu.VMEM((n_sub, tqd, V7X_LANES), F32), pltpu.VMEM((n_sub, tqd, dv), F32)],
            compiler_params=pltpu.CompilerParams(
                dimension_semantics=("arbitrary", "arbitrary"),
                vmem_limit_bytes=_vmem_limit(
                    blocks, resident + scratch_bytes + 3 * _nbytes((tqd, tk), F32) + 3 * _nbytes((tqd, dv), F32))),
            name=name + "_direct",
        )(qk, kt_chunks, v, *extra_inputs)

    def direct_shared_k(qk, v):
        cols = n_sub * tq
        qt = qk[:, :g * qw].T
        vt_chunks = v.T.reshape(g, dv, nkv, tk).transpose(0, 2, 1, 3)
        k_col0 = g * qw // kw
        blocks = 2 * _nbytes((qw, tq), BF16)
        resident = _nbytes((t, kw), BF16) + _nbytes((dv, t), BF16)
        scratch_bytes = _nbytes((HEAD_DIM, cols), BF16) + _nbytes((8, cols), F32) + _nbytes((dv, cols), F32)
        out_t = pl.pallas_call(
            functools.partial(_flash_direct_shared_kernel, n_sub=n_sub, tq=tq, tc=tk, ctx=ctx, n_chunks=nkv),
            out_shape=jax.ShapeDtypeStruct((g * qw, t), BF16),
            grid=(g, t // tq),
            in_specs=[pl.BlockSpec((qw, tq), lambda h, qi: (h, qi)),
                      pl.BlockSpec((t, kw), lambda h, qi: (0, k_col0 + h), pipeline_mode=once),
                      pl.BlockSpec((None, nkv, dv, tk), lambda h, qi: (h, 0, 0, 0), pipeline_mode=once)],
            out_specs=pl.BlockSpec((qw, tq), lambda h, qi: (h, qi)),
            scratch_shapes=[pltpu.VMEM((HEAD_DIM, cols), BF16), pltpu.VMEM((1, cols), F32),
                            pltpu.VMEM((dv, cols), F32)],
            compiler_params=pltpu.CompilerParams(
                dimension_semantics=("arbitrary", "arbitrary"),
                vmem_limit_bytes=_vmem_limit(blocks, resident + scratch_bytes + 3 * _nbytes((tk, cols), F32))),
            name=name + "_direct",
        )(qt, qk, vt_chunks)
        return out_t.T

    def online(qk, v):
        kt = keys_t(qk)
        q = qk
        last_ctx_kv = (ctx - 1) // tk

        def kv_index(qi, ki):
            return jnp.where(qi < n_ctx_q, jnp.minimum(ki, last_ctx_kv), ki)

        blocks = tile_bytes + _nbytes((kw, tk), BF16) + _nbytes((tk, dv), BF16)
        return pl.pallas_call(
            functools.partial(_flash_online_kernel, n_sub=n_sub, per_sub_k=per_sub_k, tq=tq, tk=tk, ctx=ctx, nkv=nkv,
                              lam_init=lam_init),
            out_shape=out_shape,
            grid=(g, t // tq, nkv),
            in_specs=[pl.BlockSpec((tq, qw), lambda h, qi, ki: (qi, h)),
                      pl.BlockSpec((kw, tk), lambda h, qi, ki: (h, kv_index(qi, ki))),
                      pl.BlockSpec((tk, dv), lambda h, qi, ki: (kv_index(qi, ki), h))] + extra_specs3,
            out_specs=pl.BlockSpec((tq, ow), lambda h, qi, ki: (qi, h)),
            scratch_shapes=[pltpu.VMEM((n_sub, tq, 1), F32), pltpu.VMEM((n_sub, tq, 1), F32),
                            pltpu.VMEM((n_sub, tq, dv), F32)],
            compiler_params=pltpu.CompilerParams(
                dimension_semantics=("arbitrary", "arbitrary", "arbitrary"),
                vmem_limit_bytes=_vmem_limit(blocks, temp_bytes + 2 * n_sub * _nbytes((tq, V7X_LANES), F32))),
            name=name + "_online",
        )(q, kt, v, *extra_inputs)

    direct = direct_per_sub_k if per_sub_k else direct_shared_k
    return lax.cond(score_bound <= SCORE_BOUND_DIRECT, direct, online, qk, v)


def _seq_neighbors(buf_ref, cur, prev8, next8, rows, row0, ctx, total):
    buf_ref[0:V7X_SUBLANES, :] = prev8
    buf_ref[V7X_SUBLANES:V7X_SUBLANES + rows, :] = cur
    buf_ref[V7X_SUBLANES + rows:, :] = next8
    t = row0 + lax.broadcasted_iota(jnp.int32, (rows, 1), 0)
    has_prev = jnp.logical_and(t != 0, t != ctx)
    has_next = jnp.logical_and(t != ctx - 1, t != total - 1)
    before = jnp.where(has_prev, buf_ref[V7X_SUBLANES - 1:V7X_SUBLANES - 1 + rows, :], 0.0)
    after = jnp.where(has_next, buf_ref[V7X_SUBLANES + 1:V7X_SUBLANES + 1 + rows, :], 0.0)
    return before, after


def _ffn_in_kernel(h_ref, hp_ref, hn_ref, wg_ref, wu_ref, cw_ref, cb_ref, o_ref, abuf, gbuf, *, tm, ctx, total):
    abuf[0:BF16_ROWS, :] = hp_ref[...]
    abuf[BF16_ROWS:BF16_ROWS + tm, :] = h_ref[...]
    abuf[BF16_ROWS + tm:, :] = hn_ref[...]
    gbuf[...] = jnp.dot(abuf[...], wg_ref[...], preferred_element_type=F32)
    up = jnp.dot(h_ref[...], wu_ref[...], preferred_element_type=F32)
    t = pl.program_id(1) * tm + lax.broadcasted_iota(jnp.int32, (tm, 1), 0)
    has_prev = jnp.logical_and(t != 0, t != ctx)
    has_next = jnp.logical_and(t != ctx - 1, t != total - 1)
    before = jnp.where(has_prev, gbuf[BF16_ROWS - 1:BF16_ROWS - 1 + tm, :], 0.0)
    after = jnp.where(has_next, gbuf[BF16_ROWS + 1:BF16_ROWS + 1 + tm, :], 0.0)
    conv = (before * cw_ref[0:1, :] + gbuf[BF16_ROWS:BF16_ROWS + tm, :] * cw_ref[1:2, :] + after * cw_ref[2:3, :]
            + cb_ref[...])
    o_ref[...] = (jax.nn.silu(conv) * up).astype(o_ref.dtype)


def ffn_in_glu(h, w_gate, w_up, conv_w, conv_b, ctx):
    t, d = h.shape
    fp = w_gate.shape[1]
    tm, tn, _ = _mm_tiles(t, d, fp)
    hb = tm // BF16_ROWS
    last = t // BF16_ROWS - 1
    once = pl.Buffered(1)
    blocks = _nbytes((tm + 2 * BF16_ROWS, d), BF16) + _nbytes((tm, tn), BF16) + 4 * _nbytes((8, tn), F32)
    resident = 2 * _nbytes((d, tn), BF16)
    scratch = _nbytes((tm + 2 * BF16_ROWS, d), BF16) + _nbytes((tm + 2 * BF16_ROWS, tn), F32)
    return pl.pallas_call(
        functools.partial(_ffn_in_kernel, tm=tm, ctx=ctx, total=t),
        out_shape=jax.ShapeDtypeStruct((t, fp), BF16),
        grid=(fp // tn, t // tm),
        in_specs=[
            pl.BlockSpec((tm, d), lambda j, i: (i, 0)),
            pl.BlockSpec((BF16_ROWS, d), lambda j, i: (jnp.maximum(i * hb - 1, 0), 0)),
            pl.BlockSpec((BF16_ROWS, d), lambda j, i: (jnp.minimum((i + 1) * hb, last), 0)),
            pl.BlockSpec((d, tn), lambda j, i: (0, j), pipeline_mode=once),
            pl.BlockSpec((d, tn), lambda j, i: (0, j), pipeline_mode=once),
            pl.BlockSpec((3, tn), lambda j, i: (0, j)),
            pl.BlockSpec((1, tn), lambda j, i: (0, j)),
        ],
        out_specs=pl.BlockSpec((tm, tn), lambda j, i: (i, j)),
        scratch_shapes=[pltpu.VMEM((tm + 2 * BF16_ROWS, d), BF16), pltpu.VMEM((tm + 2 * BF16_ROWS, tn), F32)],
        compiler_params=pltpu.CompilerParams(
            dimension_semantics=("arbitrary", "arbitrary"),
            vmem_limit_bytes=_vmem_limit(blocks, resident + scratch + 5 * _nbytes((tm, tn), F32))),
        name="ffn_in_glu",
    )(h, h, h, w_gate, w_up, conv_w, conv_b.reshape(1, fp))


def _rw_mix_kernel(x_ref, xp_ref, xn_ref, g_ref, sh_ref, sc_ref, mix_ref, *rest, rows, ctx, total):
    outs, buf_ref = rest[:6], rest[6]
    row0 = pl.program_id(0) * rows
    g, sh, sc = g_ref[...], sh_ref[...], sc_ref[...]

    def nm(x, first_row):
        t = first_row + lax.broadcasted_iota(jnp.int32, (x.shape[0], 1), 0)
        return _norm_mod_rows(x, g, sh, sc, t < ctx)

    h = nm(x_ref[...], row0)
    hp = nm(xp_ref[...], row0 - V7X_SUBLANES)
    hn = nm(xn_ref[...], row0 + rows)
    before, after = _seq_neighbors(buf_ref, h, hp, hn, rows, row0, ctx, total)
    xx = 0.5 * (before + after) - h
    for n in range(6):
        outs[n][...] = (h + xx * mix_ref[n:n + 1, :]).astype(outs[n].dtype)


def rw_token_mix(x, g, shift, scale, mix, ctx):
    t, d = x.shape
    rows = _tile(t, (128, 64, 32, 16, 8))
    rb = rows // V7X_SUBLANES
    last8 = t // V7X_SUBLANES - 1
    blocks = _nbytes((rows + 16, d), F32) + 6 * _nbytes((rows, d), BF16) + 4 * _nbytes((8, d), F32)
    return pl.pallas_call(
        functools.partial(_rw_mix_kernel, rows=rows, ctx=ctx, total=t),
        out_shape=[jax.ShapeDtypeStruct((t, d), BF16)] * 6,
        grid=(t // rows,),
        in_specs=[
            pl.BlockSpec((rows, d), lambda i: (i, 0)),
            pl.BlockSpec((V7X_SUBLANES, d), lambda i: (jnp.maximum(i * rb - 1, 0), 0)),
            pl.BlockSpec((V7X_SUBLANES, d), lambda i: (jnp.minimum((i + 1) * rb, last8), 0)),
            pl.BlockSpec((1, d), lambda i: (0, 0)),
            pl.BlockSpec((2, d), lambda i: (0, 0)),
            pl.BlockSpec((2, d), lambda i: (0, 0)),
            pl.BlockSpec((6, d), lambda i: (0, 0)),
        ],
        out_specs=[pl.BlockSpec((rows, d), lambda i: (i, 0))] * 6,
        scratch_shapes=[pltpu.VMEM((rows + 2 * V7X_SUBLANES, d), F32)],
        compiler_params=pltpu.CompilerParams(
            dimension_semantics=("arbitrary",),
            vmem_limit_bytes=_vmem_limit(blocks, 6 * _nbytes((rows + 16, d), F32))),
        name="rw_token_mix",
    )(x, x, x, g.reshape(1, d), shift, scale, mix)


SCAN_STEPS = 32
PREP_UNROLL = 4


def _mirror_block(i, n_blocks, n_ctx_blocks):
    return jnp.where(i < n_ctx_blocks, n_ctx_blocks - 1 - i, n_blocks + n_ctx_blocks - 1 - i)


def _scan_specs(t, chans, lanes, ctx):
    assert ctx % SCAN_STEPS == 0 and t % SCAN_STEPS == 0
    nb, nbc = t // SCAN_STEPS, ctx // SCAN_STEPS
    here = pl.BlockSpec((SCAN_STEPS, chans, lanes), lambda i: (i, 0, 0))
    mirror = pl.BlockSpec((SCAN_STEPS, chans, lanes), lambda i: (_mirror_block(i, nb, nbc), 0, 0))
    return nb, here, mirror


def _rw_prep_kernel(ra, rb, ka, kb, va, vb, wla, wlb, ala, alb, w0_ref, a0_ref, kk_ref, ka_ref, rk_ref,
                    decay_o, a_o, b_o, kd_o, wr_o, v_o, bonus_o, br_o, kr_o):
    lanes = wla.shape[-1]
    half = lanes // 2
    is_fwd = lax.broadcasted_iota(jnp.int32, (1, lanes), 1) < half

    def body(s, decay_before):
        sb = SCAN_STEPS - 1 - s

        def pick(xa, xb):
            return jnp.where(is_fwd, xa[s], xb[sb])

        def join(xa, xb):
            fwd, bwd = xa[s], xb[sb]
            low = jnp.where(is_fwd, fwd, pltpu.roll(bwd, half, 1))
            high = jnp.where(is_fwd, pltpu.roll(fwd, half, 1), bwd)
            return jnp.concatenate([low, high], axis=0)

        r, k, v = join(ra, rb), join(ka, kb), join(va, vb)
        kk = k * kk_ref[...]
        kk = kk * lax.rsqrt(jnp.maximum(jnp.sum(kk * kk, axis=0, keepdims=True), 1e-24))
        w = jnp.exp(-jnp.exp(-jax.nn.softplus(-(w0_ref[...] + pick(wla, wlb))) - 0.5))
        iclr = jax.nn.sigmoid(a0_ref[...] + pick(ala, alb))
        kd = k * (1.0 + (iclr - 1.0) * ka_ref[...])
        b = kk * iclr
        decay = decay_before * w
        undo = 1.0 / decay
        a_o[s] = -kk * decay_before
        b_o[s] = b * undo
        kd_o[s] = kd * undo
        wr_o[s] = r * decay
        v_o[s] = v
        br_o[s] = jnp.sum(b * r, axis=0, keepdims=True)
        kr_o[s] = jnp.sum(kd * r, axis=0, keepdims=True)
        bonus_o[s] = jnp.sum(r * kd * rk_ref[...], axis=0, keepdims=True) * v
        return decay

    decay_o[...] = lax.fori_loop(0, SCAN_STEPS, body, jnp.ones(decay_o.shape, F32), unroll=PREP_UNROLL)


def _wkv_kernel(*refs, chans):
    prep_in, (y_ref, bonus_ref), scratch = refs[:15], refs[15:17], refs[17:]
    decay_ref, a_ref, b_ref, kd_ref, wr_ref, v_ref, br_ref, kr_ref, s_ref = scratch
    _rw_prep_kernel(*prep_in, decay_ref, a_ref, b_ref, kd_ref, wr_ref, v_ref, bonus_ref, br_ref, kr_ref)

    @pl.when(pl.program_id(0) == 0)
    def _():
        s_ref[...] = jnp.zeros_like(s_ref)

    def reduce_state(i):
        sa = jnp.zeros(s_ref.shape[1:], F32)
        u = jnp.zeros(s_ref.shape[1:], F32)
        for c in range(chans):
            sc = s_ref[c]
            sa = sa + sc * a_ref[i, c:c + 1, :]
            u = u + sc * wr_ref[i, c:c + 1, :]
        return sa, u

    def emit(i, sa, u, vv):
        y_ref[i] = u + sa * br_ref[i] + vv * kr_ref[i]

    def update(i, c, sa, vv):
        return s_ref[c] + sa * b_ref[i, c:c + 1, :] + vv * kd_ref[i, c:c + 1, :]

    def step(i, carry):
        sa, u = carry
        vv = v_ref[i]
        emit(i, sa, u, vv)
        sa_next = jnp.zeros(s_ref.shape[1:], F32)
        u_next = jnp.zeros(s_ref.shape[1:], F32)
        for c in range(chans):
            sn = update(i, c, sa, vv)
            s_ref[c] = sn
            sa_next = sa_next + sn * a_ref[i + 1, c:c + 1, :]
            u_next = u_next + sn * wr_ref[i + 1, c:c + 1, :]
        return sa_next, u_next

    last = SCAN_STEPS - 1
    sa, u = lax.fori_loop(0, last, step, reduce_state(0))
    vv = v_ref[last]
    emit(last, sa, u, vv)
    for c in range(chans):
        s_ref[c] = update(last, c, sa, vv) * decay_ref[c:c + 1, :]


def wkv_scan(r, k, v, wl, al, w0, a0, k_k, k_a, r_k, ctx):
    t, chans, lanes = wl.shape
    nb, here, mirror = _scan_specs(t, chans, lanes, ctx)
    _, here_h, mirror_h = _scan_specs(t, chans // 2, lanes, ctx)
    par = pl.BlockSpec((chans, lanes), lambda i: (0, 0))
    big = jax.ShapeDtypeStruct((t, chans, lanes), F32)
    step_block = pltpu.VMEM((SCAN_STEPS, chans, lanes), F32)
    step_row = pltpu.VMEM((SCAN_STEPS, 1, lanes), F32)
    blocks = 12 * _nbytes((SCAN_STEPS, chans, lanes), F32) + 5 * _nbytes((chans, lanes), F32)
    scratch_bytes = (5 * _nbytes((SCAN_STEPS, chans, lanes), F32) + 2 * _nbytes((SCAN_STEPS, 8, lanes), F32)
                     + _nbytes((chans, lanes), F32) + _nbytes((chans, chans, lanes), F32))
    return pl.pallas_call(
        functools.partial(_wkv_kernel, chans=chans),
        out_shape=[big, big],
        grid=(nb,),
        in_specs=[here_h, mirror_h] * 3 + [here, mirror] * 2 + [par] * 5,
        out_specs=[here, here],
        scratch_shapes=([pltpu.VMEM((chans, lanes), F32)] + [step_block] * 5 + [step_row] * 2
                        + [pltpu.VMEM((chans, chans, lanes), F32)]),
        compiler_params=_params(("arbitrary",), blocks, scratch_bytes + 16 * _nbytes((chans, lanes), F32)),
        name="wkv_scan",
    )(r, r, k, k, v, v, wl, wl, al, al, w0, a0, k_k, k_a, r_k)


def _rw_finish_kernel(ya, yb, ba, bb, gate_ref, lnw_ref, lnb_ref, o_ref):
    lanes = ya.shape[-1]
    half = lanes // 2
    packed = ya.shape[1] // 2
    is_low = lax.broadcasted_iota(jnp.int32, (1, lanes), 1) < half

    def body(s, carry):
        sb = SCAN_STEPS - 1 - s
        y = ya[s] + pltpu.roll(yb[sb], half, 1)
        bonus = ba[s] + pltpu.roll(bb[sb], half, 1)
        mu = jnp.mean(y, axis=0, keepdims=True)
        yc = y - mu
        var = jnp.mean(yc * yc, axis=0, keepdims=True)
        z = yc * lax.rsqrt(var + RW_LN_EPS) * lnw_ref[...] + lnb_ref[...] + bonus
        z_packed = jnp.where(is_low, z[:packed], pltpu.roll(z[packed:], half, 1))
        o_ref[s] = (z_packed * gate_ref[s]).astype(o_ref.dtype)
        return carry

    lax.fori_loop(0, SCAN_STEPS, body, 0, unroll=PREP_UNROLL)


def rw_finish(y, bonus, gate, ln_w, ln_b, ctx):
    t, chans, lanes = y.shape
    nb, here, mirror = _scan_specs(t, chans, lanes, ctx)
    half = pl.BlockSpec((SCAN_STEPS, chans // 2, lanes), lambda i: (i, 0, 0))
    par = pl.BlockSpec((chans, lanes), lambda i: (0, 0))
    blocks = 6 * _nbytes((SCAN_STEPS, chans, lanes), F32) + 2 * _nbytes((chans, lanes), F32)
    return pl.pallas_call(
        _rw_finish_kernel,
        out_shape=jax.ShapeDtypeStruct((t, chans // 2, lanes), BF16),
        grid=(nb,),
        in_specs=[here, mirror, here, mirror, half, par, par],
        out_specs=half,
        compiler_params=_params(("arbitrary",), blocks, 8 * _nbytes((chans, lanes), F32)),
        name="rw_finish",
    )(y, y, bonus, bonus, gate, ln_w, ln_b)


def _rope_tables(ctx, seq):
    pos = jnp.arange(seq)
    row = (pos // GRID_W).astype(F32)
    col = (pos % GRID_W).astype(F32)
    n_freq = HEAD_DIM // 4
    inv_freq = ROPE_THETA ** (-jnp.arange(n_freq, dtype=F32) / n_freq)
    ang = jnp.concatenate([row[:, None] * inv_freq, col[:, None] * inv_freq], axis=-1)
    cos = jnp.concatenate([jnp.ones((ctx, HEAD_DIM // 2), F32), jnp.cos(ang)], axis=0)
    sin = jnp.concatenate([jnp.zeros((ctx, HEAD_DIM // 2), F32), jnp.sin(ang)], axis=0)
    return jnp.concatenate([cos, cos], axis=-1), jnp.concatenate([-sin, sin], axis=-1)


def _channel_major(w):
    heads = w.shape[-1] // RW_HEAD_DIM
    return jnp.swapaxes(w.reshape(w.shape[:-1] + (heads, RW_HEAD_DIM)), -1, -2)


def _channel_packed(w):
    heads = w.shape[-1] // RW_HEAD_DIM
    w4 = w.reshape(w.shape[:-1] + (heads, 2, RW_HEAD_DIM // 2))
    return jnp.moveaxis(w4, -3, -1).swapaxes(-3, -2).reshape(w.shape)


def _per_direction(w2):
    cm = _channel_major(w2)
    zero = jnp.zeros_like(cm[0])
    top = jnp.concatenate([cm[0], zero], axis=-1)
    bottom = jnp.concatenate([zero, cm[1]], axis=-1)
    return jnp.concatenate([top, bottom], axis=0).reshape(2 * w2.shape[1], -1)


def _score_gain(q_g, k_g):
    q_gain = q_g * (HEAD_DIM ** -0.5 * math.log2(math.e))
    bound = HEAD_DIM * jnp.max(jnp.abs(q_gain)) * jnp.max(jnp.abs(k_g))
    return q_gain, bound


def _diff_lambda_init(layer_idx):
    return 0.8 - 0.6 * math.exp(-0.3 * layer_idx)


def _diff_attention(h, cc, ss, wqkv, wo, q_g, k_g, lam_vecs, subln_g, lam_init, x, gate, ctx):
    d = h.shape[1]
    n_heads = d // HEAD_DIM
    q_gain, bound = _score_gain(q_g, k_g)
    gain = jnp.concatenate([jnp.tile(q_gain, n_heads), jnp.tile(k_g, n_heads)])
    qk = matmul_headnorm_rope(h, wqkv[:, :2 * d].astype(BF16), gain, cc, ss, name="da_qk_proj")
    v = matmul(h, wqkv[:, 2 * d:].astype(BF16), BF16, name="da_v_proj")
    o = flash_attention(qk, v, ctx, bound, groups=n_heads // 2, n_sub=2, per_sub_k=True, lam=lam_vecs,
                        subln=subln_g, lam_init=lam_init, name="da_flash")
    return matmul_residual(o, wo.astype(BF16), x, gate, ctx, name="da_out_proj")


def _gqa_attention(h, cc, ss, wqkv, wo, q_g, k_g, x, gate, ctx):
    d = h.shape[1]
    kv_w = d // GA_GROUP
    q_gain, bound = _score_gain(q_g, k_g)
    gain = jnp.concatenate([jnp.tile(q_gain, d // HEAD_DIM), jnp.tile(k_g, kv_w // HEAD_DIM)])
    qk = matmul_headnorm_rope(h, wqkv[:, :d + kv_w].astype(BF16), gain, cc, ss, name="ga_qk_proj")
    v = matmul(h, wqkv[:, d + kv_w:].astype(BF16), BF16, name="ga_v_proj")
    o = flash_attention(qk, v, ctx, bound, groups=kv_w // HEAD_DIM, n_sub=GA_GROUP, per_sub_k=False,
                        name="ga_flash")
    return matmul_residual(o, wo.astype(BF16), x, gate, ctx, name="ga_out_proj")


def _pad_cols(w, n):
    return jnp.pad(w, ((0, 0), (0, n - w.shape[1])))


def _rwkv7(x, g, shift, scale, mix, wrkv, wo, w0, w1, w2, a0, a1, a2, g1, g2, k_k, k_a, r_k, ln_w, ln_b, gate,
           ctx):
    t, d = x.shape
    heads = d // RW_HEAD_DIM
    scan_shape = (t, RW_HEAD_DIM, 2 * heads)
    packed_shape = (t, RW_HEAD_DIM // 2, 2 * heads)
    xr, xw, xk, xv, xa, xg = rw_token_mix(x, g, shift, scale, mix, ctx)

    def shared(xin, w, name):
        return matmul(xin, _channel_packed(w).astype(BF16), F32, name=name).reshape(packed_shape)

    r = shared(xr, wrkv[0], "rw_r_proj")
    k = shared(xk, wrkv[1], "rw_k_proj")
    v = shared(xv, wrkv[2], "rw_v_proj")
    hw = matmul(xw, jnp.concatenate([w1[0], w1[1]], axis=1).astype(BF16), BF16, act="tanh", name="rw_w_lora_in")
    ha = matmul(xa, jnp.concatenate([a1[0], a1[1]], axis=1).astype(BF16), BF16, name="rw_a_lora_in")
    wl = matmul(hw, _per_direction(w2).astype(BF16), F32, name="rw_w_lora_out").reshape(scan_shape)
    al = matmul(ha, _per_direction(a2).astype(BF16), F32, name="rw_a_lora_out").reshape(scan_shape)
    lg = -(-g1.shape[1] // V7X_LANES) * V7X_LANES
    hg = matmul(xg, _pad_cols(g1, lg).astype(BF16), BF16, act="sigmoid", name="rw_g_lora_in")
    g2p = jnp.pad(_channel_packed(g2), ((0, lg - g2.shape[0]), (0, 0))).astype(BF16)
    out_gate = matmul(hg, g2p, F32, name="rw_g_lora_out").reshape(packed_shape)

    def dir_param(p):
        return jnp.concatenate([_channel_major(p[0]), _channel_major(p[1])], axis=-1)

    def shared_param(p):
        return jnp.concatenate([_channel_major(p)] * 2, axis=-1)

    y, bonus = wkv_scan(r, k, v, wl, al, dir_param(w0), dir_param(a0), shared_param(k_k), shared_param(k_a),
                        shared_param(r_k.reshape(d)), ctx)
    z = rw_finish(y, bonus, out_gate, shared_param(ln_w), shared_param(ln_b), ctx).reshape(t, d)
    wo_packed = _channel_packed(wo.T).T
    return matmul_residual(z, wo_packed.astype(BF16), x, gate, ctx, name="rw_out_proj")


def _conv_glu(h, w_in, conv_w, conv_b, w_out, x, gate, ctx):
    f = conv_w.shape[1]
    fp = -(-f // FFN_PAD) * FFN_PAD
    act = ffn_in_glu(h, _pad_cols(w_in[:, :f], fp).astype(BF16), _pad_cols(w_in[:, f:], fp).astype(BF16),
                     _pad_cols(conv_w, fp), jnp.pad(conv_b, (0, fp - f)), ctx)
    w_out_p = jnp.pad(w_out, ((0, fp - f), (0, 0))).astype(BF16)
    return matmul_residual(act, w_out_p, x, gate, ctx, name="ffn_out_proj")


def kernel(x, c, ctx, c_ctx, ada_down, ada_up, ada_b, norm_g, ffn_in, ffn_conv, ffn_conv_b, ffn_out, da_wqkv, da_wo, da_q_g, da_k_g, da_lambda, da_subln_g, rw_mix, rw_wrkv, rw_wo, rw_w0, rw_w1, rw_w2, rw_a0, rw_a1, rw_a2, rw_g1, rw_g2, rw_k_k, rw_k_a, rw_r_k, rw_ln_w, rw_ln_b, ga_wqkv, ga_wo, ga_q_g, ga_k_g):
    assert x.shape[0] == 1
    seq, d = x.shape[1], x.shape[2]
    n_ctx = ctx.shape[1]
    depth = ada_down.shape[0]
    stream = jnp.concatenate([ctx[0], x[0]], axis=0)
    cond = jnp.zeros((V7X_SUBLANES, d), F32).at[0].set(c_ctx).at[1].set(c[0])
    mod = ada_modulation(cond, ada_down, ada_up, ada_b)[:, :2].reshape(depth, 2, N_MOD, d)
    cc, ss = _rope_tables(n_ctx, seq)
    for i in range(depth):
        kind, j = i % N_MIXERS, i // N_MIXERS
        m = [mod[i, :, n, :] for n in range(N_MOD)]
        if kind == 1:
            stream = _rwkv7(stream, norm_g[i, 0], m[0], m[1], rw_mix[j], rw_wrkv[j], rw_wo[j], rw_w0[j], rw_w1[j],
                            rw_w2[j], rw_a0[j], rw_a1[j], rw_a2[j], rw_g1[j], rw_g2[j], rw_k_k[j], rw_k_a[j],
                            rw_r_k[j], rw_ln_w[j], rw_ln_b[j], m[2], n_ctx)
        else:
            h = norm_mod(stream, norm_g[i, 0], m[0], m[1], n_ctx)
            if kind == 0:
                stream = _diff_attention(h, cc, ss, da_wqkv[j], da_wo[j], da_q_g[j], da_k_g[j], da_lambda[j],
                                         da_subln_g[j], _diff_lambda_init(i), stream, m[2], n_ctx)
            else:
                stream = _gqa_attention(h, cc, ss, ga_wqkv[j], ga_wo[j], ga_q_g[j], ga_k_g[j], stream, m[2], n_ctx)
        h2 = norm_mod(stream, norm_g[i, 1], m[3], m[4], n_ctx)
        stream = _conv_glu(h2, ffn_in[i], ffn_conv[i], ffn_conv_b[i], ffn_out[i], stream, m[5], n_ctx)
    return stream[n_ctx:][None]
```

```python
import functools
import math

import jax
import jax.numpy as jnp
from jax import lax
from jax.experimental import pallas as pl
from jax.experimental.pallas import tpu as pltpu

F32 = jnp.float32
BF16 = jnp.bfloat16

N_MOD = 6
N_MIXERS = 3
GRID_W = 64
ROPE_THETA = 10000.0
NORM_EPS = 1e-6
HEAD_DIM = 128
GA_GROUP = 4
RW_HEAD_DIM = 64
RW_LN_EPS = 64e-5

V7X_LANES = 128
V7X_SUBLANES = 8
BF16_ROWS = 16
QK_ROW_SLABS = 4
V7X_VMEM_BYTES = 64 * 1024 * 1024
V7X_VMEM_RESERVE = 6 * 1024 * 1024
FFN_PAD = 1024


def _vmem_limit(block_bytes, temp_bytes=0):
    return int(min(2 * block_bytes + temp_bytes + V7X_VMEM_RESERVE, V7X_VMEM_BYTES - V7X_VMEM_RESERVE))


def _params(semantics, block_bytes, temp_bytes=0):
    return pltpu.CompilerParams(dimension_semantics=semantics,
                                vmem_limit_bytes=_vmem_limit(block_bytes, temp_bytes))


def _tile(n, prefs):
    for t in prefs:
        if n % t == 0:
            return t
    return n


def _nbytes(shape, dtype):
    return math.prod(shape) * jnp.dtype(dtype).itemsize


def _row_ids(i, rows, shape):
    return i * rows + lax.broadcasted_iota(jnp.int32, shape, 0)


def _ada_kernel(cond_ref, down_ref, up_ref, b_ref, o_ref):
    s = jax.nn.silu(cond_ref[...])
    t = jnp.dot(s, down_ref[...], preferred_element_type=F32, precision=lax.Precision.HIGHEST)
    m = jnp.dot(t, up_ref[...], preferred_element_type=F32, precision=lax.Precision.HIGHEST)
    o_ref[...] = m + b_ref[...]


def ada_modulation(cond, ada_down, ada_up, ada_b):
    depth, d, r = ada_down.shape
    n = ada_up.shape[-1]
    tn = _tile(n, (4096, 2048, 1024, 512))
    blocks = _nbytes((8, d), F32) + _nbytes((d, r), F32) + _nbytes((r, tn), F32) + 2 * _nbytes((8, tn), F32)
    return pl.pallas_call(
        _ada_kernel,
        out_shape=jax.ShapeDtypeStruct((depth, 8, n), F32),
        grid=(depth, n // tn),
        in_specs=[
            pl.BlockSpec((8, d), lambda l, j: (0, 0)),
            pl.BlockSpec((None, d, r), lambda l, j: (l, 0, 0)),
            pl.BlockSpec((None, r, tn), lambda l, j: (l, 0, j)),
            pl.BlockSpec((None, 1, tn), lambda l, j: (l, 0, j)),
        ],
        out_specs=pl.BlockSpec((None, 8, tn), lambda l, j: (l, 0, j)),
        compiler_params=_params(("arbitrary", "arbitrary"), blocks),
        name="ada_modulation",
    )(cond, ada_down, ada_up, ada_b.reshape(depth, 1, n))


def _norm_mod_rows(x, g, shift, scale, is_ctx):
    y = x * lax.rsqrt(jnp.mean(x * x, axis=-1, keepdims=True) + NORM_EPS) * g
    sc = jnp.where(is_ctx, scale[0:1, :], scale[1:2, :])
    sh = jnp.where(is_ctx, shift[0:1, :], shift[1:2, :])
    return y * (1.0 + sc) + sh


def _norm_mod_kernel(x_ref, g_ref, sh_ref, sc_ref, o_ref, *, rows, ctx):
    is_ctx = _row_ids(pl.program_id(0), rows, (rows, 1)) < ctx
    o_ref[...] = _norm_mod_rows(x_ref[...], g_ref[...], sh_ref[...], sc_ref[...], is_ctx).astype(o_ref.dtype)


def norm_mod(x, g, shift, scale, ctx):
    t, d = x.shape
    rows = _tile(t, (256, 128, 64, 32, 16, 8))
    blocks = _nbytes((rows, d), F32) + _nbytes((rows, d), BF16) + 5 * _nbytes((8, d), F32)
    return pl.pallas_call(
        functools.partial(_norm_mod_kernel, rows=rows, ctx=ctx),
        out_shape=jax.ShapeDtypeStruct((t, d), BF16),
        grid=(t // rows,),
        in_specs=[
            pl.BlockSpec((rows, d), lambda i: (i, 0)),
            pl.BlockSpec((1, d), lambda i: (0, 0)),
            pl.BlockSpec((2, d), lambda i: (0, 0)),
            pl.BlockSpec((2, d), lambda i: (0, 0)),
        ],
        out_specs=pl.BlockSpec((rows, d), lambda i: (i, 0)),
        compiler_params=_params(("arbitrary",), blocks, 4 * _nbytes((rows, d), F32)),
        name="norm_mod",
    )(x, g.reshape(1, d), shift, scale)


def _mm_plain_kernel(a_ref, b_ref, o_ref, *, act):
    acc = jnp.dot(a_ref[...], b_ref[...], preferred_element_type=F32)
    if act == "tanh":
        acc = jnp.tanh(acc)
    elif act == "sigmoid":
        acc = jax.nn.sigmoid(acc)
    o_ref[...] = acc.astype(o_ref.dtype)


def _mm_residual_kernel(a_ref, b_ref, x_ref, g_ref, o_ref, *, tm, ctx):
    acc = jnp.dot(a_ref[...], b_ref[...], preferred_element_type=F32)
    is_ctx = _row_ids(pl.program_id(1), tm, (tm, 1)) < ctx
    gate = jnp.where(is_ctx, g_ref[0:1, :], g_ref[1:2, :])
    o_ref[...] = x_ref[...] + gate * acc


def _mm_headnorm_rope_kernel(a_ref, b_ref, g_ref, cc_ref, ss_ref, o_ref, *, tn):
    tm = a_ref.shape[0]
    n_slabs = QK_ROW_SLABS if tm % (QK_ROW_SLABS * BF16_ROWS) == 0 else 1
    rows = tm // n_slabs
    for r in range(n_slabs):
        rs = slice(r * rows, (r + 1) * rows)
        cc = cc_ref[rs, :]
        ss = ss_ref[rs, :]
        acc = jnp.dot(a_ref[rs, :], b_ref[...], preferred_element_type=F32)
        for h in range(tn // HEAD_DIM):
            sl = slice(h * HEAD_DIM, (h + 1) * HEAD_DIM)
            z = acc[:, sl]
            y = z * lax.rsqrt(jnp.mean(z * z, axis=-1, keepdims=True) + NORM_EPS) * g_ref[:, sl]
            o_ref[rs, sl] = (y * cc + pltpu.roll(y, HEAD_DIM // 2, 1) * ss).astype(o_ref.dtype)


MM_WEIGHT_TILE_BYTES = 12 * 1024 * 1024
MM_DOUBLE_BUFFER_BYTES = 8 * 1024 * 1024


def _mm_tiles(m, k, n):
    tm = _tile(m, (640, 512, 384, 256, 128, 64, 32, 16, 8))
    tn = next(c for c in (1024, 512, 256, 128, n) if n % c == 0 and _nbytes((k, c), BF16) <= MM_WEIGHT_TILE_BYTES)
    return tm, tn, _nbytes((k, tn), BF16) > MM_DOUBLE_BUFFER_BYTES


def _mm_call(kernel, a, b, extra_inputs, extra_specs, out_dtype, extra_bytes, name):
    m, k = a.shape
    n = b.shape[1]
    tm, tn, deep = _mm_tiles(m, k, n)
    weights = _nbytes((k, tn), b.dtype)
    blocks = _nbytes((tm, k), a.dtype) + _nbytes((tm, tn), out_dtype) + extra_bytes(tm, tn)
    b_spec = pl.BlockSpec((k, tn), lambda j, i: (0, j), **({"pipeline_mode": pl.Buffered(1)} if deep else {}))
    return pl.pallas_call(
        kernel,
        out_shape=jax.ShapeDtypeStruct((m, n), out_dtype),
        grid=(n // tn, m // tm),
        in_specs=[pl.BlockSpec((tm, k), lambda j, i: (i, 0)), b_spec] + extra_specs(tm, tn),
        out_specs=pl.BlockSpec((tm, tn), lambda j, i: (i, j)),
        compiler_params=pltpu.CompilerParams(
            dimension_semantics=("arbitrary", "arbitrary"),
            vmem_limit_bytes=_vmem_limit(blocks, (1 if deep else 2) * weights + 2 * _nbytes((tm, tn), F32))),
        name=name,
    )(a, b, *extra_inputs)


def matmul(a, b, out_dtype, act=None, name="matmul"):
    return _mm_call(functools.partial(_mm_plain_kernel, act=act), a, b, (), lambda tm, tn: [],
                    out_dtype, lambda tm, tn: 0, name)


def matmul_residual(a, b, x, gate, ctx, name="matmul_residual"):
    tm = _mm_tiles(a.shape[0], a.shape[1], b.shape[1])[0]

    def specs(tm_, tn):
        return [pl.BlockSpec((tm_, tn), lambda j, i: (i, j)), pl.BlockSpec((2, tn), lambda j, i: (0, j))]

    return _mm_call(functools.partial(_mm_residual_kernel, tm=tm, ctx=ctx), a, b, (x, gate), specs, F32,
                    lambda tm_, tn: _nbytes((tm_, tn), F32) + _nbytes((8, tn), F32), name)


def matmul_headnorm_rope(a, b, gain, cc, ss, name="matmul_headnorm_rope"):
    m, k = a.shape
    n = b.shape[1]
    tm, tn, _ = _mm_tiles(m, k, n)
    blocks = (_nbytes((tm, k), a.dtype) + _nbytes((k, tn), b.dtype) + _nbytes((tm, tn), BF16)
              + _nbytes((8, tn), F32) + 2 * _nbytes((tm, HEAD_DIM), F32) + _nbytes((tm, tn), F32))
    return pl.pallas_call(
        functools.partial(_mm_headnorm_rope_kernel, tn=tn),
        out_shape=jax.ShapeDtypeStruct((m, n), BF16),
        grid=(n // tn, m // tm),
        in_specs=[
            pl.BlockSpec((tm, k), lambda j, i: (i, 0)),
            pl.BlockSpec((k, tn), lambda j, i: (0, j)),
            pl.BlockSpec((1, tn), lambda j, i: (0, j)),
            pl.BlockSpec((tm, HEAD_DIM), lambda j, i: (i, 0)),
            pl.BlockSpec((tm, HEAD_DIM), lambda j, i: (i, 0)),
        ],
        out_specs=pl.BlockSpec((tm, tn), lambda j, i: (i, j)),
        compiler_params=_params(("arbitrary", "arbitrary"), blocks, 2 * _nbytes((tm, tn), F32)),
        name=name,
    )(a, b, gain.reshape(1, n), cc, ss)


SCORE_BOUND_DIRECT = 64.0


def _flash_finalize(o_ref, l_rows, acc_ref, lam_ref, subln_ref, *, n_sub, lam_init):
    if lam_init is None:
        for s in range(n_sub):
            o_ref[:, s * HEAD_DIM:(s + 1) * HEAD_DIM] = (acc_ref[s] / l_rows[s]).astype(o_ref.dtype)
        return
    lv = lam_ref[...]
    lam = (jnp.exp(jnp.sum(lv[0:1, :] * lv[1:2, :], axis=-1, keepdims=True))
           - jnp.exp(jnp.sum(lv[2:3, :] * lv[3:4, :], axis=-1, keepdims=True)) + lam_init)
    o = acc_ref[0] / l_rows[0] - lam * (acc_ref[1] / l_rows[1])
    o = o * lax.rsqrt(jnp.mean(o * o, axis=-1, keepdims=True) + NORM_EPS) * subln_ref[...]
    o_ref[...] = (o * (1.0 - lam_init)).astype(o_ref.dtype)


def _chunk_loop(n_chunks, visit):
    lead = n_chunks % 2
    for c in range(lead):
        visit(c)

    def pair(j, carry):
        visit(lead + 2 * j)
        visit(lead + 2 * j + 1)
        return carry

    lax.fori_loop(0, n_chunks // 2, pair, 0)


def _flash_direct_kernel(q_ref, kt_ref, v_ref, lam_ref, subln_ref, o_ref, l_ref, acc_ref, *, n_sub, tq, tc, ctx,
                         n_chunks, lam_init):
    l_ref[...] = jnp.zeros_like(l_ref)
    acc_ref[...] = jnp.zeros_like(acc_ref)

    def accumulate(chunk, width, v, r0, nr):
        for s in range(n_sub):
            cols = slice(s * HEAD_DIM, (s + 1) * HEAD_DIM)
            p = jnp.exp2(jnp.dot(q_ref[r0:r0 + nr, cols], kt_ref[chunk, cols, 0:width], preferred_element_type=F32))
            part = p[:, 0:V7X_LANES]
            for j in range(1, width // V7X_LANES):
                part = part + p[:, j * V7X_LANES:(j + 1) * V7X_LANES]
            l_ref[s, r0:r0 + nr, :] += part
            acc_ref[s, r0:r0 + nr, :] += jnp.dot(p.astype(v.dtype), v, preferred_element_type=F32)

    def all_keys(r0, nr):
        _chunk_loop(n_chunks, lambda c: accumulate(c, tc, v_ref[pl.ds(pl.multiple_of(c * tc, tc), tc), :], r0, nr))

    @pl.when(pl.program_id(1) > 0)
    def _():
        all_keys(0, tq)

    @pl.when(pl.program_id(1) == 0)
    def _():
        accumulate(0, ctx, v_ref[0:ctx, :], 0, ctx)
        if ctx < tq:
            all_keys(ctx, tq - ctx)

    l_rows = [jnp.sum(l_ref[s], axis=-1, keepdims=True) for s in range(n_sub)]
    _flash_finalize(o_ref, l_rows, acc_ref, lam_ref, subln_ref, n_sub=n_sub, lam_init=lam_init)


def _flash_direct_shared_kernel(qt_ref, k_ref, vt_ref, ot_ref, qs_ref, l_ref, acc_ref, *, n_sub, tq, tc, ctx, n_chunks):
    for g in range(n_sub):
        qs_ref[:, g * tq:(g + 1) * tq] = qt_ref[g * HEAD_DIM:(g + 1) * HEAD_DIM, :]
    l_ref[...] = jnp.zeros_like(l_ref)
    acc_ref[...] = jnp.zeros_like(acc_ref)

    def accumulate(k, vt):
        pt = jnp.exp2(jnp.dot(k, qs_ref[...], preferred_element_type=F32))
        l_ref[...] += jnp.sum(pt, axis=0, keepdims=True)
        acc_ref[...] += jnp.dot(vt, pt.astype(vt.dtype), preferred_element_type=F32)

    @pl.when(pl.program_id(1) > 0)
    def _():
        _chunk_loop(n_chunks, lambda c: accumulate(k_ref[pl.ds(pl.multiple_of(c * tc, tc), tc), :], vt_ref[c]))

    @pl.when(pl.program_id(1) == 0)
    def _():
        accumulate(k_ref[0:ctx, :], vt_ref[0, :, 0:ctx])

    out = acc_ref[...] / l_ref[...]
    for g in range(n_sub):
        ot_ref[g * HEAD_DIM:(g + 1) * HEAD_DIM, :] = out[:, g * tq:(g + 1) * tq].astype(ot_ref.dtype)


def _flash_online_step(q_ref, kt_ref, v_ref, m_ref, l_ref, acc_ref, *, n_sub, per_sub_k, tk, ctx, masked):
    ki = pl.program_id(2)
    v = v_ref[...]
    for s in range(n_sub):
        q = q_ref[:, s * HEAD_DIM:(s + 1) * HEAD_DIM]
        kt = kt_ref[s * HEAD_DIM:(s + 1) * HEAD_DIM, :] if per_sub_k else kt_ref[...]
        sc = jnp.dot(q, kt, preferred_element_type=F32)
        if masked:
            col = ki * tk + lax.broadcasted_iota(jnp.int32, sc.shape, 1)
            sc = jnp.where(col < ctx, sc, -jnp.inf)
        m_prev = m_ref[s]
        m_new = jnp.maximum(m_prev, jnp.max(sc, axis=-1, keepdims=True))
        alpha = jnp.exp2(m_prev - m_new)
        p = jnp.exp2(sc - m_new)
        l_ref[s] = alpha * l_ref[s] + jnp.sum(p, axis=-1, keepdims=True)
        acc_ref[s] = alpha * acc_ref[s] + jnp.dot(p.astype(v.dtype), v, preferred_element_type=F32)
        m_ref[s] = m_new


def _flash_online_kernel(q_ref, kt_ref, v_ref, *rest, n_sub, per_sub_k, tq, tk, ctx, nkv, lam_init):
    if lam_init is None:
        lam_ref = subln_ref = None
        o_ref, m_ref, l_ref, acc_ref = rest
    else:
        lam_ref, subln_ref, o_ref, m_ref, l_ref, acc_ref = rest
    qi = pl.program_id(1)
    ki = pl.program_id(2)
    q_is_ctx = (qi + 1) * tq <= ctx
    step = functools.partial(_flash_online_step, q_ref, kt_ref, v_ref, m_ref, l_ref, acc_ref,
                             n_sub=n_sub, per_sub_k=per_sub_k, tk=tk, ctx=ctx)

    @pl.when(ki == 0)
    def _():
        m_ref[...] = jnp.full(m_ref.shape, -jnp.inf, F32)
        l_ref[...] = jnp.zeros_like(l_ref)
        acc_ref[...] = jnp.zeros_like(acc_ref)

    @pl.when(jnp.logical_not(q_is_ctx))
    def _():
        step(masked=False)

    @pl.when(jnp.logical_and(q_is_ctx, ki * tk < ctx))
    def _():
        step(masked=True)

    @pl.when(ki == nkv - 1)
    def _():
        _flash_finalize(o_ref, [l_ref[s] for s in range(n_sub)], acc_ref, lam_ref, subln_ref, n_sub=n_sub,
                        lam_init=lam_init)


def flash_attention(qk, v, ctx, score_bound, *, groups, n_sub, per_sub_k, lam=None, subln=None, lam_init=None,
                    name="flash"):
    t = qk.shape[0]
    g = groups
    qw = n_sub * HEAD_DIM
    kw = qk.shape[1] // g - qw
    dv = v.shape[1] // g
    ow = dv if lam_init is not None else qw
    tq = _tile(ctx, (256, 128, 64, 32, 16, 8))
    tk = _tile(t, (1280, 1024, 768, 512, 256, 128))
    assert t % tq == 0 and ctx % tq == 0 and ctx <= tk and ctx % V7X_LANES == 0
    assert per_sub_k == (lam_init is not None)
    nkv = t // tk
    n_ctx_q = ctx // tq
    extra_specs3 = extra_specs2 = []
    extra_inputs = []
    if lam_init is not None:
        extra_specs3 = [pl.BlockSpec(lam.shape, lambda h, qi, ki: (0, 0)),
                        pl.BlockSpec((1, dv), lambda h, qi, ki: (0, 0))]
        extra_specs2 = [pl.BlockSpec(lam.shape, lambda h, qi: (0, 0)), pl.BlockSpec((1, dv), lambda h, qi: (0, 0))]
        extra_inputs = [lam, subln.reshape(1, dv)]
    out_shape = jax.ShapeDtypeStruct((t, g * ow), BF16)
    tile_bytes = _nbytes((tq, qw), BF16) + _nbytes((tq, ow), BF16)
    temp_bytes = 4 * _nbytes((tq, tk), F32) + n_sub * _nbytes((tq, dv), F32)
    once = pl.Buffered(1)

    def keys_t(qk):
        return qk[:, g * qw:].T

    def direct_per_sub_k(qk, v):
        tqd = next(c for c in (1280, 768, 512, 256, tq) if t % c == 0 and c % ctx == 0)
        kt_chunks = keys_t(qk).reshape(g, kw, nkv, tk).transpose(0, 2, 1, 3)
        blocks = _nbytes((tqd, qw), BF16) + _nbytes((tqd, ow), BF16)
        resident = _nbytes((kw, t), BF16) + _nbytes((t, dv), BF16)
        scratch_bytes = n_sub * (_nbytes((tqd, V7X_LANES), F32) + _nbytes((tqd, dv), F32))
        return pl.pallas_call(
            functools.partial(_flash_direct_kernel, n_sub=n_sub, tq=tqd, tc=tk, ctx=ctx, n_chunks=nkv,
                              lam_init=lam_init),
            out_shape=out_shape,
            grid=(g, t // tqd),
            in_specs=[pl.BlockSpec((tqd, qw), lambda h, qi: (qi, h)),
                      pl.BlockSpec((None, nkv, kw, tk), lambda h, qi: (h, 0, 0, 0), pipeline_mode=once),
                      pl.BlockSpec((t, dv), lambda h, qi: (0, h), pipeline_mode=once)] + extra_specs2,
            out_specs=pl.BlockSpec((tqd, ow), lambda h, qi: (qi, h)),
            scratch_shapes=[pltpu.VMEM((n_sub, tqd, V7X_LANES), F32), pltpu.VMEM((n_sub, tqd, dv), F32)],
            compiler_params=pltpu.CompilerParams(
                dimension_semantics=("arbitrary", "arbitrary"),
                vmem_limit_bytes=_vmem_limit(
                    blocks, resident + scratch_bytes + 3 * _nbytes((tqd, tk), F32) + 3 * _nbytes((tqd, dv), F32))),
            name=name + "_direct",
        )(qk, kt_chunks, v, *extra_inputs)

    def direct_shared_k(qk, v):
        cols = n_sub * tq
        qt = qk[:, :g * qw].T
        vt_chunks = v.T.reshape(g, dv, nkv, tk).transpose(0, 2, 1, 3)
        k_col0 = g * qw // kw
        blocks = 2 * _nbytes((qw, tq), BF16)
        resident = _nbytes((t, kw), BF16) + _nbytes((dv, t), BF16)
        scratch_bytes = _nbytes((HEAD_DIM, cols), BF16) + _nbytes((8, cols), F32) + _nbytes((dv, cols), F32)
        out_t = pl.pallas_call(
            functools.partial(_flash_direct_shared_kernel, n_sub=n_sub, tq=tq, tc=tk, ctx=ctx, n_chunks=nkv),
            out_shape=jax.ShapeDtypeStruct((g * qw, t), BF16),
            grid=(g, t // tq),
            in_specs=[pl.BlockSpec((qw, tq), lambda h, qi: (h, qi)),
                      pl.BlockSpec((t, kw), lambda h, qi: (0, k_col0 + h), pipeline_mode=once),
                      pl.BlockSpec((None, nkv, dv, tk), lambda h, qi: (h, 0, 0, 0), pipeline_mode=once)],
            out_specs=pl.BlockSpec((qw, tq), lambda h, qi: (h, qi)),
            scratch_shapes=[pltpu.VMEM((HEAD_DIM, cols), BF16), pltpu.VMEM((1, cols), F32),
                            pltpu.VMEM((dv, cols), F32)],
            compiler_params=pltpu.CompilerParams(
                dimension_semantics=("arbitrary", "arbitrary"),
                vmem_limit_bytes=_vmem_limit(blocks, resident + scratch_bytes + 3 * _nbytes((tk, cols), F32))),
            name=name + "_direct",
        )(qt, qk, vt_chunks)
        return out_t.T

    def online(qk, v):
        kt = keys_t(qk)
        q = qk
        last_ctx_kv = (ctx - 1) // tk

        def kv_index(qi, ki):
            return jnp.where(qi < n_ctx_q, jnp.minimum(ki, last_ctx_kv), ki)

        blocks = tile_bytes + _nbytes((kw, tk), BF16) + _nbytes((tk, dv), BF16)
        return pl.pallas_call(
            functools.partial(_flash_online_kernel, n_sub=n_sub, per_sub_k=per_sub_k, tq=tq, tk=tk, ctx=ctx, nkv=nkv,
                              lam_init=lam_init),
            out_shape=out_shape,
            grid=(g, t // tq, nkv),
            in_specs=[pl.BlockSpec((tq, qw), lambda h, qi, ki: (qi, h)),
                      pl.BlockSpec((kw, tk), lambda h, qi, ki: (h, kv_index(qi, ki))),
                      pl.BlockSpec((tk, dv), lambda h, qi, ki: (kv_index(qi, ki), h))] + extra_specs3,
            out_specs=pl.BlockSpec((tq, ow), lambda h, qi, ki: (qi, h)),
            scratch_shapes=[pltpu.VMEM((n_sub, tq, 1), F32), pltpu.VMEM((n_sub, tq, 1), F32),
                            pltpu.VMEM((n_sub, tq, dv), F32)],
            compiler_params=pltpu.CompilerParams(
                dimension_semantics=("arbitrary", "arbitrary", "arbitrary"),
                vmem_limit_bytes=_vmem_limit(blocks, temp_bytes + 2 * n_sub * _nbytes((tq, V7X_LANES), F32))),
            name=name + "_online",
        )(q, kt, v, *extra_inputs)

    direct = direct_per_sub_k if per_sub_k else direct_shared_k
    return lax.cond(score_bound <= SCORE_BOUND_DIRECT, direct, online, qk, v)


def _seq_neighbors(buf_ref, cur, prev8, next8, rows, row0, ctx, total):
    buf_ref[0:V7X_SUBLANES, :] = prev8
    buf_ref[V7X_SUBLANES:V7X_SUBLANES + rows, :] = cur
    buf_ref[V7X_SUBLANES + rows:, :] = next8
    t = row0 + lax.broadcasted_iota(jnp.int32, (rows, 1), 0)
    has_prev = jnp.logical_and(t != 0, t != ctx)
    has_next = jnp.logical_and(t != ctx - 1, t != total - 1)
    before = jnp.where(has_prev, buf_ref[V7X_SUBLANES - 1:V7X_SUBLANES - 1 + rows, :], 0.0)
    after = jnp.where(has_next, buf_ref[V7X_SUBLANES + 1:V7X_SUBLANES + 1 + rows, :], 0.0)
    return before, after


def _ffn_in_kernel(h_ref, hp_ref, hn_ref, wg_ref, wu_ref, cw_ref, cb_ref, o_ref, abuf, gbuf, *, tm, ctx, total):
    abuf[0:BF16_ROWS, :] = hp_ref[...]
    abuf[BF16_ROWS:BF16_ROWS + tm, :] = h_ref[...]
    abuf[BF16_ROWS + tm:, :] = hn_ref[...]
    gbuf[...] = jnp.dot(abuf[...], wg_ref[...], preferred_element_type=F32)
    up = jnp.dot(h_ref[...], wu_ref[...], preferred_element_type=F32)
    t = pl.program_id(1) * tm + lax.broadcasted_iota(jnp.int32, (tm, 1), 0)
    has_prev = jnp.logical_and(t != 0, t != ctx)
    has_next = jnp.logical_and(t != ctx - 1, t != total - 1)
    before = jnp.where(has_prev, gbuf[BF16_ROWS - 1:BF16_ROWS - 1 + tm, :], 0.0)
    after = jnp.where(has_next, gbuf[BF16_ROWS + 1:BF16_ROWS + 1 + tm, :], 0.0)
    conv = (before * cw_ref[0:1, :] + gbuf[BF16_ROWS:BF16_ROWS + tm, :] * cw_ref[1:2, :] + after * cw_ref[2:3, :]
            + cb_ref[...])
    o_ref[...] = (jax.nn.silu(conv) * up).astype(o_ref.dtype)


def ffn_in_glu(h, w_gate, w_up, conv_w, conv_b, ctx):
    t, d = h.shape
    fp = w_gate.shape[1]
    tm, tn, _ = _mm_tiles(t, d, fp)
    hb = tm // BF16_ROWS
    last = t // BF16_ROWS - 1
    once = pl.Buffered(1)
    blocks = _nbytes((tm + 2 * BF16_ROWS, d), BF16) + _nbytes((tm, tn), BF16) + 4 * _nbytes((8, tn), F32)
    resident = 2 * _nbytes((d, tn), BF16)
    scratch = _nbytes((tm + 2 * BF16_ROWS, d), BF16) + _nbytes((tm + 2 * BF16_ROWS, tn), F32)
    return pl.pallas_call(
        functools.partial(_ffn_in_kernel, tm=tm, ctx=ctx, total=t),
        out_shape=jax.ShapeDtypeStruct((t, fp), BF16),
        grid=(fp // tn, t // tm),
        in_specs=[
            pl.BlockSpec((tm, d), lambda j, i: (i, 0)),
            pl.BlockSpec((BF16_ROWS, d), lambda j, i: (jnp.maximum(i * hb - 1, 0), 0)),
            pl.BlockSpec((BF16_ROWS, d), lambda j, i: (jnp.minimum((i + 1) * hb, last), 0)),
            pl.BlockSpec((d, tn), lambda j, i: (0, j), pipeline_mode=once),
            pl.BlockSpec((d, tn), lambda j, i: (0, j), pipeline_mode=once),
            pl.BlockSpec((3, tn), lambda j, i: (0, j)),
            pl.BlockSpec((1, tn), lambda j, i: (0, j)),
        ],
        out_specs=pl.BlockSpec((tm, tn), lambda j, i: (i, j)),
        scratch_shapes=[pltpu.VMEM((tm + 2 * BF16_ROWS, d), BF16), pltpu.VMEM((tm + 2 * BF16_ROWS, tn), F32)],
        compiler_params=pltpu.CompilerParams(
            dimension_semantics=("arbitrary", "arbitrary"),
            vmem_limit_bytes=_vmem_limit(blocks, resident + scratch + 5 * _nbytes((tm, tn), F32))),
        name="ffn_in_glu",
    )(h, h, h, w_gate, w_up, conv_w, conv_b.reshape(1, fp))


def _rw_mix_kernel(x_ref, xp_ref, xn_ref, g_ref, sh_ref, sc_ref, mix_ref, *rest, rows, ctx, total):
    outs, buf_ref = rest[:6], rest[6]
    row0 = pl.program_id(0) * rows
    g, sh, sc = g_ref[...], sh_ref[...], sc_ref[...]

    def nm(x, first_row):
        t = first_row + lax.broadcasted_iota(jnp.int32, (x.shape[0], 1), 0)
        return _norm_mod_rows(x, g, sh, sc, t < ctx)

    h = nm(x_ref[...], row0)
    hp = nm(xp_ref[...], row0 - V7X_SUBLANES)
    hn = nm(xn_ref[...], row0 + rows)
    before, after = _seq_neighbors(buf_ref, h, hp, hn, rows, row0, ctx, total)
    xx = 0.5 * (before + after) - h
    for n in range(6):
        outs[n][...] = (h + xx * mix_ref[n:n + 1, :]).astype(outs[n].dtype)


def rw_token_mix(x, g, shift, scale, mix, ctx):
    t, d = x.shape
    rows = _tile(t, (128, 64, 32, 16, 8))
    rb = rows // V7X_SUBLANES
    last8 = t // V7X_SUBLANES - 1
    blocks = _nbytes((rows + 16, d), F32) + 6 * _nbytes((rows, d), BF16) + 4 * _nbytes((8, d), F32)
    return pl.pallas_call(
        functools.partial(_rw_mix_kernel, rows=rows, ctx=ctx, total=t),
        out_shape=[jax.ShapeDtypeStruct((t, d), BF16)] * 6,
        grid=(t // rows,),
        in_specs=[
            pl.BlockSpec((rows, d), lambda i: (i, 0)),
            pl.BlockSpec((V7X_SUBLANES, d), lambda i: (jnp.maximum(i * rb - 1, 0), 0)),
            pl.BlockSpec((V7X_SUBLANES, d), lambda i: (jnp.minimum((i + 1) * rb, last8), 0)),
            pl.BlockSpec((1, d), lambda i: (0, 0)),
            pl.BlockSpec((2, d), lambda i: (0, 0)),
            pl.BlockSpec((2, d), lambda i: (0, 0)),
            pl.BlockSpec((6, d), lambda i: (0, 0)),
        ],
        out_specs=[pl.BlockSpec((rows, d), lambda i: (i, 0))] * 6,
        scratch_shapes=[pltpu.VMEM((rows + 2 * V7X_SUBLANES, d), F32)],
        compiler_params=pltpu.CompilerParams(
            dimension_semantics=("arbitrary",),
            vmem_limit_bytes=_vmem_limit(blocks, 6 * _nbytes((rows + 16, d), F32))),
        name="rw_token_mix",
    )(x, x, x, g.reshape(1, d), shift, scale, mix)


SCAN_STEPS = 32
PREP_UNROLL = 4


def _mirror_block(i, n_blocks, n_ctx_blocks):
    return jnp.where(i < n_ctx_blocks, n_ctx_blocks - 1 - i, n_blocks + n_ctx_blocks - 1 - i)


def _scan_specs(t, chans, lanes, ctx):
    assert ctx % SCAN_STEPS == 0 and t % SCAN_STEPS == 0
    nb, nbc = t // SCAN_STEPS, ctx // SCAN_STEPS
    here = pl.BlockSpec((SCAN_STEPS, chans, lanes), lambda i: (i, 0, 0))
    mirror = pl.BlockSpec((SCAN_STEPS, chans, lanes), lambda i: (_mirror_block(i, nb, nbc), 0, 0))
    return nb, here, mirror


def _rw_prep_kernel(ra, rb, ka, kb, va, vb, wla, wlb, ala, alb, w0_ref, a0_ref, kk_ref, ka_ref, rk_ref,
                    decay_o, a_o, b_o, kd_o, wr_o, v_o, bonus_o, br_o, kr_o):
    lanes = wla.shape[-1]
    half = lanes // 2
    is_fwd = lax.broadcasted_iota(jnp.int32, (1, lanes), 1) < half

    def body(s, decay_before):
        sb = SCAN_STEPS - 1 - s

        def pick(xa, xb):
            return jnp.where(is_fwd, xa[s], xb[sb])

        def join(xa, xb):
            fwd, bwd = xa[s], xb[sb]
            low = jnp.where(is_fwd, fwd, pltpu.roll(bwd, half, 1))
            high = jnp.where(is_fwd, pltpu.roll(fwd, half, 1), bwd)
            return jnp.concatenate([low, high], axis=0)

        r, k, v = join(ra, rb), join(ka, kb), join(va, vb)
        kk = k * kk_ref[...]
        kk = kk * lax.rsqrt(jnp.maximum(jnp.sum(kk * kk, axis=0, keepdims=True), 1e-24))
        w = jnp.exp(-jnp.exp(-jax.nn.softplus(-(w0_ref[...] + pick(wla, wlb))) - 0.5))
        iclr = jax.nn.sigmoid(a0_ref[...] + pick(ala, alb))
        kd = k * (1.0 + (iclr - 1.0) * ka_ref[...])
        b = kk * iclr
        decay = decay_before * w
        undo = 1.0 / decay
        a_o[s] = -kk * decay_before
        b_o[s] = b * undo
        kd_o[s] = kd * undo
        wr_o[s] = r * decay
        v_o[s] = v
        br_o[s] = jnp.sum(b * r, axis=0, keepdims=True)
        kr_o[s] = jnp.sum(kd * r, axis=0, keepdims=True)
        bonus_o[s] = jnp.sum(r * kd * rk_ref[...], axis=0, keepdims=True) * v
        return decay

    decay_o[...] = lax.fori_loop(0, SCAN_STEPS, body, jnp.ones(decay_o.shape, F32), unroll=PREP_UNROLL)


def _wkv_kernel(*refs, chans):
    prep_in, (y_ref, bonus_ref), scratch = refs[:15], refs[15:17], refs[17:]
    decay_ref, a_ref, b_ref, kd_ref, wr_ref, v_ref, br_ref, kr_ref, s_ref = scratch
    _rw_prep_kernel(*prep_in, decay_ref, a_ref, b_ref, kd_ref, wr_ref, v_ref, bonus_ref, br_ref, kr_ref)

    @pl.when(pl.program_id(0) == 0)
    def _():
        s_ref[...] = jnp.zeros_like(s_ref)

    def reduce_state(i):
        sa = jnp.zeros(s_ref.shape[1:], F32)
        u = jnp.zeros(s_ref.shape[1:], F32)
        for c in range(chans):
            sc = s_ref[c]
            sa = sa + sc * a_ref[i, c:c + 1, :]
            u = u + sc * wr_ref[i, c:c + 1, :]
        return sa, u

    def emit(i, sa, u, vv):
        y_ref[i] = u + sa * br_ref[i] + vv * kr_ref[i]

    def update(i, c, sa, vv):
        return s_ref[c] + sa * b_ref[i, c:c + 1, :] + vv * kd_ref[i, c:c + 1, :]

    def step(i, carry):
        sa, u = carry
        vv = v_ref[i]
        emit(i, sa, u, vv)
        sa_next = jnp.zeros(s_ref.shape[1:], F32)
        u_next = jnp.zeros(s_ref.shape[1:], F32)
        for c in range(chans):
            sn = update(i, c, sa, vv)
            s_ref[c] = sn
            sa_next = sa_next + sn * a_ref[i + 1, c:c + 1, :]
            u_next = u_next + sn * wr_ref[i + 1, c:c + 1, :]
        return sa_next, u_next

    last = SCAN_STEPS - 1
    sa, u = lax.fori_loop(0, last, step, reduce_state(0))
    vv = v_ref[last]
    emit(last, sa, u, vv)
    for c in range(chans):
        s_ref[c] = update(last, c, sa, vv) * decay_ref[c:c + 1, :]


def wkv_scan(r, k, v, wl, al, w0, a0, k_k, k_a, r_k, ctx):
    t, chans, lanes = wl.shape
    nb, here, mirror = _scan_specs(t, chans, lanes, ctx)
    _, here_h, mirror_h = _scan_specs(t, chans // 2, lanes, ctx)
    par = pl.BlockSpec((chans, lanes), lambda i: (0, 0))
    big = jax.ShapeDtypeStruct((t, chans, lanes), F32)
    step_block = pltpu.VMEM((SCAN_STEPS, chans, lanes), F32)
    step_row = pltpu.VMEM((SCAN_STEPS, 1, lanes), F32)
    blocks = 12 * _nbytes((SCAN_STEPS, chans, lanes), F32) + 5 * _nbytes((chans, lanes), F32)
    scratch_bytes = (5 * _nbytes((SCAN_STEPS, chans, lanes), F32) + 2 * _nbytes((SCAN_STEPS, 8, lanes), F32)
                     + _nbytes((chans, lanes), F32) + _nbytes((chans, chans, lanes), F32))
    return pl.pallas_call(
        functools.partial(_wkv_kernel, chans=chans),
        out_shape=[big, big],
        grid=(nb,),
        in_specs=[here_h, mirror_h] * 3 + [here, mirror] * 2 + [par] * 5,
        out_specs=[here, here],
        scratch_shapes=([pltpu.VMEM((chans, lanes), F32)] + [step_block] * 5 + [step_row] * 2
                        + [pltpu.VMEM((chans, chans, lanes), F32)]),
        compiler_params=_params(("arbitrary",), blocks, scratch_bytes + 16 * _nbytes((chans, lanes), F32)),
        name="wkv_scan",
    )(r, r, k, k, v, v, wl, wl, al, al, w0, a0, k_k, k_a, r_k)


def _rw_finish_kernel(ya, yb, ba, bb, gate_ref, lnw_ref, lnb_ref, o_ref):
    lanes = ya.shape[-1]
    half = lanes // 2
    packed = ya.shape[1] // 2
    is_low = lax.broadcasted_iota(jnp.int32, (1, lanes), 1) < half

    def body(s, carry):
        sb = SCAN_STEPS - 1 - s
        y = ya[s] + pltpu.roll(yb[sb], half, 1)
        bonus = ba[s] + pltpu.roll(bb[sb], half, 1)
        mu = jnp.mean(y, axis=0, keepdims=True)
        yc = y - mu
        var = jnp.mean(yc * yc, axis=0, keepdims=True)
        z = yc * lax.rsqrt(var + RW_LN_EPS) * lnw_ref[...] + lnb_ref[...] + bonus
        z_packed = jnp.where(is_low, z[:packed], pltpu.roll(z[packed:], half, 1))
        o_ref[s] = (z_packed * gate_ref[s]).astype(o_ref.dtype)
        return carry

    lax.fori_loop(0, SCAN_STEPS, body, 0, unroll=PREP_UNROLL)


def rw_finish(y, bonus, gate, ln_w, ln_b, ctx):
    t, chans, lanes = y.shape
    nb, here, mirror = _scan_specs(t, chans, lanes, ctx)
    half = pl.BlockSpec((SCAN_STEPS, chans // 2, lanes), lambda i: (i, 0, 0))
    par = pl.BlockSpec((chans, lanes), lambda i: (0, 0))
    blocks = 6 * _nbytes((SCAN_STEPS, chans, lanes), F32) + 2 * _nbytes((chans, lanes), F32)
    return pl.pallas_call(
        _rw_finish_kernel,
        out_shape=jax.ShapeDtypeStruct((t, chans // 2, lanes), BF16),
        grid=(nb,),
        in_specs=[here, mirror, here, mirror, half, par, par],
        out_specs=half,
        compiler_params=_params(("arbitrary",), blocks, 8 * _nbytes((chans, lanes), F32)),
        name="rw_finish",
    )(y, y, bonus, bonus, gate, ln_w, ln_b)


def _rope_tables(ctx, seq):
    pos = jnp.arange(seq)
    row = (pos // GRID_W).astype(F32)
    col = (pos % GRID_W).astype(F32)
    n_freq = HEAD_DIM // 4
    inv_freq = ROPE_THETA ** (-jnp.arange(n_freq, dtype=F32) / n_freq)
    ang = jnp.concatenate([row[:, None] * inv_freq, col[:, None] * inv_freq], axis=-1)
    cos = jnp.concatenate([jnp.ones((ctx, HEAD_DIM // 2), F32), jnp.cos(ang)], axis=0)
    sin = jnp.concatenate([jnp.zeros((ctx, HEAD_DIM // 2), F32), jnp.sin(ang)], axis=0)
    return jnp.concatenate([cos, cos], axis=-1), jnp.concatenate([-sin, sin], axis=-1)


def _channel_major(w):
    heads = w.shape[-1] // RW_HEAD_DIM
    return jnp.swapaxes(w.reshape(w.shape[:-1] + (heads, RW_HEAD_DIM)), -1, -2)


def _channel_packed(w):
    heads = w.shape[-1] // RW_HEAD_DIM
    w4 = w.reshape(w.shape[:-1] + (heads, 2, RW_HEAD_DIM // 2))
    return jnp.moveaxis(w4, -3, -1).swapaxes(-3, -2).reshape(w.shape)


def _per_direction(w2):
    cm = _channel_major(w2)
    zero = jnp.zeros_like(cm[0])
    top = jnp.concatenate([cm[0], zero], axis=-1)
    bottom = jnp.concatenate([zero, cm[1]], axis=-1)
    return jnp.concatenate([top, bottom], axis=0).reshape(2 * w2.shape[1], -1)


def _score_gain(q_g, k_g):
    q_gain = q_g * (HEAD_DIM ** -0.5 * math.log2(math.e))
    bound = HEAD_DIM * jnp.max(jnp.abs(q_gain)) * jnp.max(jnp.abs(k_g))
    return q_gain, bound


def _diff_lambda_init(layer_idx):
    return 0.8 - 0.6 * math.exp(-0.3 * layer_idx)


def _diff_attention(h, cc, ss, wqkv, wo, q_g, k_g, lam_vecs, subln_g, lam_init, x, gate, ctx):
    d = h.shape[1]
    n_heads = d // HEAD_DIM
    q_gain, bound = _score_gain(q_g, k_g)
    gain = jnp.concatenate([jnp.tile(q_gain, n_heads), jnp.tile(k_g, n_heads)])
    qk = matmul_headnorm_rope(h, wqkv[:, :2 * d].astype(BF16), gain, cc, ss, name="da_qk_proj")
    v = matmul(h, wqkv[:, 2 * d:].astype(BF16), BF16, name="da_v_proj")
    o = flash_attention(qk, v, ctx, bound, groups=n_heads // 2, n_sub=2, per_sub_k=True, lam=lam_vecs,
                        subln=subln_g, lam_init=lam_init, name="da_flash")
    return matmul_residual(o, wo.astype(BF16), x, gate, ctx, name="da_out_proj")


def _gqa_attention(h, cc, ss, wqkv, wo, q_g, k_g, x, gate, ctx):
    d = h.shape[1]
    kv_w = d // GA_GROUP
    q_gain, bound = _score_gain(q_g, k_g)
    gain = jnp.concatenate([jnp.tile(q_gain, d // HEAD_DIM), jnp.tile(k_g, kv_w // HEAD_DIM)])
    qk = matmul_headnorm_rope(h, wqkv[:, :d + kv_w].astype(BF16), gain, cc, ss, name="ga_qk_proj")
    v = matmul(h, wqkv[:, d + kv_w:].astype(BF16), BF16, name="ga_v_proj")
    o = flash_attention(qk, v, ctx, bound, groups=kv_w // HEAD_DIM, n_sub=GA_GROUP, per_sub_k=False,
                        name="ga_flash")
    return matmul_residual(o, wo.astype(BF16), x, gate, ctx, name="ga_out_proj")


def _pad_cols(w, n):
    return jnp.pad(w, ((0, 0), (0, n - w.shape[1])))


def _rwkv7(x, g, shift, scale, mix, wrkv, wo, w0, w1, w2, a0, a1, a2, g1, g2, k_k, k_a, r_k, ln_w, ln_b, gate,
           ctx):
    t, d = x.shape
    heads = d // RW_HEAD_DIM
    scan_shape = (t, RW_HEAD_DIM, 2 * heads)
    packed_shape = (t, RW_HEAD_DIM // 2, 2 * heads)
    xr, xw, xk, xv, xa, xg = rw_token_mix(x, g, shift, scale, mix, ctx)

    def shared(xin, w, name):
        return matmul(xin, _channel_packed(w).astype(BF16), F32, name=name).reshape(packed_shape)

    r = shared(xr, wrkv[0], "rw_r_proj")
    k = shared(xk, wrkv[1], "rw_k_proj")
    v = shared(xv, wrkv[2], "rw_v_proj")
    hw = matmul(xw, jnp.concatenate([w1[0], w1[1]], axis=1).astype(BF16), BF16, act="tanh", name="rw_w_lora_in")
    ha = matmul(xa, jnp.concatenate([a1[0], a1[1]], axis=1).astype(BF16), BF16, name="rw_a_lora_in")
    wl = matmul(hw, _per_direction(w2).astype(BF16), F32, name="rw_w_lora_out").reshape(scan_shape)
    al = matmul(ha, _per_direction(a2).astype(BF16), F32, name="rw_a_lora_out").reshape(scan_shape)
    lg = -(-g1.shape[1] // V7X_LANES) * V7X_LANES
    hg = matmul(xg, _pad_cols(g1, lg).astype(BF16), BF16, act="sigmoid", name="rw_g_lora_in")
    g2p = jnp.pad(_channel_packed(g2), ((0, lg - g2.shape[0]), (0, 0))).astype(BF16)
    out_gate = matmul(hg, g2p, F32, name="rw_g_lora_out").reshape(packed_shape)

    def dir_param(p):
        return jnp.concatenate([_channel_major(p[0]), _channel_major(p[1])], axis=-1)

    def shared_param(p):
        return jnp.concatenate([_channel_major(p)] * 2, axis=-1)

    y, bonus = wkv_scan(r, k, v, wl, al, dir_param(w0), dir_param(a0), shared_param(k_k), shared_param(k_a),
                        shared_param(r_k.reshape(d)), ctx)
    z = rw_finish(y, bonus, out_gate, shared_param(ln_w), shared_param(ln_b), ctx).reshape(t, d)
    wo_packed = _channel_packed(wo.T).T
    return matmul_residual(z, wo_packed.astype(BF16), x, gate, ctx, name="rw_out_proj")


def _conv_glu(h, w_in, conv_w, conv_b, w_out, x, gate, ctx):
    f = conv_w.shape[1]
    fp = -(-f // FFN_PAD) * FFN_PAD
    act = ffn_in_glu(h, _pad_cols(w_in[:, :f], fp).astype(BF16), _pad_cols(w_in[:, f:], fp).astype(BF16),
                     _pad_cols(conv_w, fp), jnp.pad(conv_b, (0, fp - f)), ctx)
    w_out_p = jnp.pad(w_out, ((0, fp - f), (0, 0))).astype(BF16)
    return matmul_residual(act, w_out_p, x, gate, ctx, name="ffn_out_proj")


def kernel(x, c, ctx, c_ctx, ada_down, ada_up, ada_b, norm_g, ffn_in, ffn_conv, ffn_conv_b, ffn_out, da_wqkv, da_wo, da_q_g, da_k_g, da_lambda, da_subln_g, rw_mix, rw_wrkv, rw_wo, rw_w0, rw_w1, rw_w2, rw_a0, rw_a1, rw_a2, rw_g1, rw_g2, rw_k_k, rw_k_a, rw_r_k, rw_ln_w, rw_ln_b, ga_wqkv, ga_wo, ga_q_g, ga_k_g):
    assert x.shape[0] == 1
    seq, d = x.shape[1], x.shape[2]
    n_ctx = ctx.shape[1]
    depth = ada_down.shape[0]
    stream = jnp.concatenate([ctx[0], x[0]], axis=0)
    cond = jnp.zeros((V7X_SUBLANES, d), F32).at[0].set(c_ctx).at[1].set(c[0])
    mod = ada_modulation(cond, ada_down, ada_up, ada_b)[:, :2].reshape(depth, 2, N_MOD, d)
    cc, ss = _rope_tables(n_ctx, seq)
    for i in range(depth):
        kind, j = i % N_MIXERS, i // N_MIXERS
        m = [mod[i, :, n, :] for n in range(N_MOD)]
        if kind == 1:
            stream = _rwkv7(stream, norm_g[i, 0], m[0], m[1], rw_mix[j], rw_wrkv[j], rw_wo[j], rw_w0[j], rw_w1[j],
                            rw_w2[j], rw_a0[j], rw_a1[j], rw_a2[j], rw_g1[j], rw_g2[j], rw_k_k[j], rw_k_a[j],
                            rw_r_k[j], rw_ln_w[j], rw_ln_b[j], m[2], n_ctx)
        else:
            h = norm_mod(stream, norm_g[i, 0], m[0], m[1], n_ctx)
            if kind == 0:
                stream = _diff_attention(h, cc, ss, da_wqkv[j], da_wo[j], da_q_g[j], da_k_g[j], da_lambda[j],
                                         da_subln_g[j], _diff_lambda_init(i), stream, m[2], n_ctx)
            else:
                stream = _gqa_attention(h, cc, ss, ga_wqkv[j], ga_wo[j], ga_q_g[j], ga_k_g[j], stream, m[2], n_ctx)
        h2 = norm_mod(stream, norm_g[i, 1], m[3], m[4], n_ctx)
        stream = _conv_glu(h2, ffn_in[i], ffn_conv[i], ffn_conv_b[i], ffn_out[i], stream, m[5], n_ctx)
    return stream[n_ctx:][None]
```

```python
import functools
import math

import jax
import jax.numpy as jnp
from jax import lax
from jax.experimental import pallas as pl
from jax.experimental.pallas import tpu as pltpu

F32 = jnp.float32
BF16 = jnp.bfloat16

N_MOD = 6
N_MIXERS = 3
GRID_W = 64
ROPE_THETA = 10000.0
NORM_EPS = 1e-6
HEAD_DIM = 128
GA_GROUP = 4
RW_HEAD_DIM = 64
RW_LN_EPS = 64e-5

V7X_LANES = 128
V7X_SUBLANES = 8
BF16_ROWS = 16
QK_ROW_SLABS = 4
V7X_VMEM_BYTES = 64 * 1024 * 1024
V7X_VMEM_RESERVE = 6 * 1024 * 1024
FFN_PAD = 1024


def _vmem_limit(block_bytes, temp_bytes=0):
    return int(min(2 * block_bytes + temp_bytes + V7X_VMEM_RESERVE, V7X_VMEM_BYTES - V7X_VMEM_RESERVE))


def _params(semantics, block_bytes, temp_bytes=0):
    return pltpu.CompilerParams(dimension_semantics=semantics,
                                vmem_limit_bytes=_vmem_limit(block_bytes, temp_bytes))


def _tile(n, prefs):
    for t in prefs:
        if n % t == 0:
            return t
    return n


def _nbytes(shape, dtype):
    return math.prod(shape) * jnp.dtype(dtype).itemsize


def _row_ids(i, rows, shape):
    return i * rows + lax.broadcasted_iota(jnp.int32, shape, 0)


def _ada_kernel(cond_ref, down_ref, up_ref, b_ref, o_ref):
    s = jax.nn.silu(cond_ref[...])
    t = jnp.dot(s, down_ref[...], preferred_element_type=F32, precision=lax.Precision.HIGHEST)
    m = jnp.dot(t, up_ref[...], preferred_element_type=F32, precision=lax.Precision.HIGHEST)
    o_ref[...] = m + b_ref[...]


def ada_modulation(cond, ada_down, ada_up, ada_b):
    depth, d, r = ada_down.shape
    n = ada_up.shape[-1]
    tn = _tile(n, (4096, 2048, 1024, 512))
    blocks = _nbytes((8, d), F32) + _nbytes((d, r), F32) + _nbytes((r, tn), F32) + 2 * _nbytes((8, tn), F32)
    return pl.pallas_call(
        _ada_kernel,
        out_shape=jax.ShapeDtypeStruct((depth, 8, n), F32),
        grid=(depth, n // tn),
        in_specs=[
            pl.BlockSpec((8, d), lambda l, j: (0, 0)),
            pl.BlockSpec((None, d, r), lambda l, j: (l, 0, 0)),
            pl.BlockSpec((None, r, tn), lambda l, j: (l, 0, j)),
            pl.BlockSpec((None, 1, tn), lambda l, j: (l, 0, j)),
        ],
        out_specs=pl.BlockSpec((None, 8, tn), lambda l, j: (l, 0, j)),
        compiler_params=_params(("arbitrary", "arbitrary"), blocks),
        name="ada_modulation",
    )(cond, ada_down, ada_up, ada_b.reshape(depth, 1, n))


def _norm_mod_rows(x, g, shift, scale, is_ctx):
    y = x * lax.rsqrt(jnp.mean(x * x, axis=-1, keepdims=True) + NORM_EPS) * g
    sc = jnp.where(is_ctx, scale[0:1, :], scale[1:2, :])
    sh = jnp.where(is_ctx, shift[0:1, :], shift[1:2, :])
    return y * (1.0 + sc) + sh


def _norm_mod_stream(x_ref, inv_ref, n_rows, first_stream_row, g_ref, sh_ref, sc_ref, ctx, emit):
    d = x_ref.shape[1]
    lane_blocks = [slice(j * V7X_LANES, (j + 1) * V7X_LANES) for j in range(d // V7X_LANES)]

    def row_stats(c, carry):
        rows = pl.ds(pl.multiple_of(c * BF16_ROWS, BF16_ROWS), BF16_ROWS)
        ssq = jnp.zeros((BF16_ROWS, V7X_LANES), F32)
        for sl in lane_blocks:
            xb = x_ref[rows, sl]
            ssq = ssq + xb * xb
        inv = lax.rsqrt(jnp.sum(ssq, axis=-1, keepdims=True) * (1.0 / d) + NORM_EPS)
        inv_ref[rows, :] = jnp.broadcast_to(inv, (BF16_ROWS, V7X_LANES))
        return carry

    n_chunks = n_rows // BF16_ROWS
    lax.fori_loop(0, n_chunks, row_stats, 0, unroll=math.gcd(n_chunks, PREP_UNROLL))
    is_ctx = first_stream_row < ctx
    for sl in lane_blocks:
        gain = g_ref[:, sl] * (1.0 + jnp.where(is_ctx, sc_ref[0:1, sl], sc_ref[1:2, sl]))
        shift = jnp.where(is_ctx, sh_ref[0:1, sl], sh_ref[1:2, sl])
        emit(sl, x_ref[0:n_rows, sl] * inv_ref[...] * gain + shift)


def _norm_mod_kernel(x_ref, g_ref, sh_ref, sc_ref, o_ref, inv_ref, *, rows, ctx):
    def emit(sl, value):
        o_ref[:, sl] = value.astype(o_ref.dtype)

    _norm_mod_stream(x_ref, inv_ref, rows, pl.program_id(0) * rows, g_ref, sh_ref, sc_ref, ctx, emit)


def norm_mod(x, g, shift, scale, ctx):
    t, d = x.shape
    rows = _tile(ctx, (256, 128, 64, 32, 16))
    assert t % rows == 0 and ctx % rows == 0 and rows % BF16_ROWS == 0
    blocks = _nbytes((rows, d), F32) + _nbytes((rows, d), BF16) + 5 * _nbytes((8, d), F32)
    return pl.pallas_call(
        functools.partial(_norm_mod_kernel, rows=rows, ctx=ctx),
        out_shape=jax.ShapeDtypeStruct((t, d), BF16),
        grid=(t // rows,),
        in_specs=[
            pl.BlockSpec((rows, d), lambda i: (i, 0)),
            pl.BlockSpec((1, d), lambda i: (0, 0)),
            pl.BlockSpec((2, d), lambda i: (0, 0)),
            pl.BlockSpec((2, d), lambda i: (0, 0)),
        ],
        out_specs=pl.BlockSpec((rows, d), lambda i: (i, 0)),
        scratch_shapes=[pltpu.VMEM((rows, V7X_LANES), F32)],
        compiler_params=_params(("arbitrary",), blocks, _nbytes((rows, d), F32)),
        name="norm_mod",
    )(x, g.reshape(1, d), shift, scale)


def _mm_plain_kernel(a_ref, b_ref, o_ref, *, act):
    acc = jnp.dot(a_ref[...], b_ref[...], preferred_element_type=F32)
    if act == "tanh":
        acc = jnp.tanh(acc)
    elif act == "sigmoid":
        acc = jax.nn.sigmoid(acc)
    o_ref[...] = acc.astype(o_ref.dtype)


def _mm_residual_kernel(a_ref, b_ref, x_ref, g_ref, o_ref, *, tm, ctx):
    acc = jnp.dot(a_ref[...], b_ref[...], preferred_element_type=F32)
    is_ctx = _row_ids(pl.program_id(1), tm, (tm, 1)) < ctx
    gate = jnp.where(is_ctx, g_ref[0:1, :], g_ref[1:2, :])
    o_ref[...] = x_ref[...] + gate * acc


def _mm_headnorm_rope_kernel(a_ref, b_ref, g_ref, cc_ref, ss_ref, o_ref, *, tn):
    tm = a_ref.shape[0]
    n_slabs = QK_ROW_SLABS if tm % (QK_ROW_SLABS * BF16_ROWS) == 0 else 1
    rows = tm // n_slabs
    for r in range(n_slabs):
        rs = slice(r * rows, (r + 1) * rows)
        cc = cc_ref[rs, :]
        ss = ss_ref[rs, :]
        acc = jnp.dot(a_ref[rs, :], b_ref[...], preferred_element_type=F32)
        for h in range(tn // HEAD_DIM):
            sl = slice(h * HEAD_DIM, (h + 1) * HEAD_DIM)
            z = acc[:, sl]
            y = z * lax.rsqrt(jnp.mean(z * z, axis=-1, keepdims=True) + NORM_EPS) * g_ref[:, sl]
            o_ref[rs, sl] = (y * cc + pltpu.roll(y, HEAD_DIM // 2, 1) * ss).astype(o_ref.dtype)


MM_WEIGHT_TILE_BYTES = 12 * 1024 * 1024
MM_DOUBLE_BUFFER_BYTES = 8 * 1024 * 1024


def _mm_tiles(m, k, n):
    tm = _tile(m, (640, 512, 384, 256, 128, 64, 32, 16, 8))
    tn = next(c for c in (1024, 512, 256, 128, n) if n % c == 0 and _nbytes((k, c), BF16) <= MM_WEIGHT_TILE_BYTES)
    return tm, tn, _nbytes((k, tn), BF16) > MM_DOUBLE_BUFFER_BYTES


def _mm_call(kernel, a, b, extra_inputs, extra_specs, out_dtype, extra_bytes, name):
    m, k = a.shape
    n = b.shape[1]
    tm, tn, deep = _mm_tiles(m, k, n)
    weights = _nbytes((k, tn), b.dtype)
    blocks = _nbytes((tm, k), a.dtype) + _nbytes((tm, tn), out_dtype) + extra_bytes(tm, tn)
    b_spec = pl.BlockSpec((k, tn), lambda j, i: (0, j), **({"pipeline_mode": pl.Buffered(1)} if deep else {}))
    return pl.pallas_call(
        kernel,
        out_shape=jax.ShapeDtypeStruct((m, n), out_dtype),
        grid=(n // tn, m // tm),
        in_specs=[pl.BlockSpec((tm, k), lambda j, i: (i, 0)), b_spec] + extra_specs(tm, tn),
        out_specs=pl.BlockSpec((tm, tn), lambda j, i: (i, j)),
        compiler_params=pltpu.CompilerParams(
            dimension_semantics=("arbitrary", "arbitrary"),
            vmem_limit_bytes=_vmem_limit(blocks, (1 if deep else 2) * weights + 2 * _nbytes((tm, tn), F32))),
        name=name,
    )(a, b, *extra_inputs)


def matmul(a, b, out_dtype, act=None, name="matmul"):
    return _mm_call(functools.partial(_mm_plain_kernel, act=act), a, b, (), lambda tm, tn: [],
                    out_dtype, lambda tm, tn: 0, name)


def matmul_residual(a, b, x, gate, ctx, name="matmul_residual"):
    tm = _mm_tiles(a.shape[0], a.shape[1], b.shape[1])[0]

    def specs(tm_, tn):
        return [pl.BlockSpec((tm_, tn), lambda j, i: (i, j)), pl.BlockSpec((2, tn), lambda j, i: (0, j))]

    return _mm_call(functools.partial(_mm_residual_kernel, tm=tm, ctx=ctx), a, b, (x, gate), specs, F32,
                    lambda tm_, tn: _nbytes((tm_, tn), F32) + _nbytes((8, tn), F32), name)


def matmul_headnorm_rope(a, b, gain, cc, ss, name="matmul_headnorm_rope"):
    m, k = a.shape
    n = b.shape[1]
    tm, tn, _ = _mm_tiles(m, k, n)
    blocks = (_nbytes((tm, k), a.dtype) + _nbytes((k, tn), b.dtype) + _nbytes((tm, tn), BF16)
              + _nbytes((8, tn), F32) + 2 * _nbytes((tm, HEAD_DIM), F32) + _nbytes((tm, tn), F32))
    return pl.pallas_call(
        functools.partial(_mm_headnorm_rope_kernel, tn=tn),
        out_shape=jax.ShapeDtypeStruct((m, n), BF16),
        grid=(n // tn, m // tm),
        in_specs=[
            pl.BlockSpec((tm, k), lambda j, i: (i, 0)),
            pl.BlockSpec((k, tn), lambda j, i: (0, j)),
            pl.BlockSpec((1, tn), lambda j, i: (0, j)),
            pl.BlockSpec((tm, HEAD_DIM), lambda j, i: (i, 0)),
            pl.BlockSpec((tm, HEAD_DIM), lambda j, i: (i, 0)),
        ],
        out_specs=pl.BlockSpec((tm, tn), lambda j, i: (i, j)),
        compiler_params=_params(("arbitrary", "arbitrary"), blocks, 2 * _nbytes((tm, tn), F32)),
        name=name,
    )(a, b, gain.reshape(1, n), cc, ss)


SCORE_BOUND_DIRECT = 64.0


def _flash_finalize(o_ref, l_rows, acc_ref, lam_ref, subln_ref, *, n_sub, lam_init):
    if lam_init is None:
        for s in range(n_sub):
            o_ref[:, s * HEAD_DIM:(s + 1) * HEAD_DIM] = (acc_ref[s] / l_rows[s]).astype(o_ref.dtype)
        return
    lv = lam_ref[...]
    lam = (jnp.exp(jnp.sum(lv[0:1, :] * lv[1:2, :], axis=-1, keepdims=True))
           - jnp.exp(jnp.sum(lv[2:3, :] * lv[3:4, :], axis=-1, keepdims=True)) + lam_init)
    o = acc_ref[0] / l_rows[0] - lam * (acc_ref[1] / l_rows[1])
    o = o * lax.rsqrt(jnp.mean(o * o, axis=-1, keepdims=True) + NORM_EPS) * subln_ref[...]
    o_ref[...] = (o * (1.0 - lam_init)).astype(o_ref.dtype)


def _chunk_loop(n_chunks, visit):
    lead = n_chunks % 2
    for c in range(lead):
        visit(c)

    def pair(j, carry):
        visit(lead + 2 * j)
        visit(lead + 2 * j + 1)
        return carry

    lax.fori_loop(0, n_chunks // 2, pair, 0)


def _flash_direct_kernel(q_ref, kt_ref, v_ref, lam_ref, subln_ref, o_ref, l_ref, acc_ref, *, n_sub, tq, tc, ctx,
                         n_chunks, lam_init):
    l_ref[...] = jnp.zeros_like(l_ref)
    acc_ref[...] = jnp.zeros_like(acc_ref)

    def accumulate(chunk, width, v, r0, nr):
        for s in range(n_sub):
            cols = slice(s * HEAD_DIM, (s + 1) * HEAD_DIM)
            p = jnp.exp2(jnp.dot(q_ref[r0:r0 + nr, cols], kt_ref[chunk, cols, 0:width], preferred_element_type=F32))
            part = p[:, 0:V7X_LANES]
            for j in range(1, width // V7X_LANES):
                part = part + p[:, j * V7X_LANES:(j + 1) * V7X_LANES]
            l_ref[s, r0:r0 + nr, :] += part
            acc_ref[s, r0:r0 + nr, :] += jnp.dot(p.astype(v.dtype), v, preferred_element_type=F32)

    def all_keys(r0, nr):
        _chunk_loop(n_chunks, lambda c: accumulate(c, tc, v_ref[pl.ds(pl.multiple_of(c * tc, tc), tc), :], r0, nr))

    @pl.when(pl.program_id(1) > 0)
    def _():
        all_keys(0, tq)

    @pl.when(pl.program_id(1) == 0)
    def _():
        accumulate(0, ctx, v_ref[0:ctx, :], 0, ctx)
        if ctx < tq:
            all_keys(ctx, tq - ctx)

    l_rows = [jnp.sum(l_ref[s], axis=-1, keepdims=True) for s in range(n_sub)]
    _flash_finalize(o_ref, l_rows, acc_ref, lam_ref, subln_ref, n_sub=n_sub, lam_init=lam_init)


def _flash_direct_shared_kernel(qt_ref, k_ref, vt_ref, ot_ref, qs_ref, l_ref, acc_ref, *, n_sub, tq, tc, ctx, n_chunks):
    for g in range(n_sub):
        qs_ref[:, g * tq:(g + 1) * tq] = qt_ref[g * HEAD_DIM:(g + 1) * HEAD_DIM, :]
    l_ref[...] = jnp.zeros_like(l_ref)
    acc_ref[...] = jnp.zeros_like(acc_ref)

    def accumulate(k, vt):
        pt = jnp.exp2(jnp.dot(k, qs_ref[...], preferred_element_type=F32))
        l_ref[...] += jnp.sum(pt, axis=0, keepdims=True)
        acc_ref[...] += jnp.dot(vt, pt.astype(vt.dtype), preferred_element_type=F32)

    @pl.when(pl.program_id(1) > 0)
    def _():
        _chunk_loop(n_chunks, lambda c: accumulate(k_ref[pl.ds(pl.multiple_of(c * tc, tc), tc), :], vt_ref[c]))

    @pl.when(pl.program_id(1) == 0)
    def _():
        accumulate(k_ref[0:ctx, :], vt_ref[0, :, 0:ctx])

    out = acc_ref[...] / l_ref[...]
    for g in range(n_sub):
        ot_ref[g * HEAD_DIM:(g + 1) * HEAD_DIM, :] = out[:, g * tq:(g + 1) * tq].astype(ot_ref.dtype)


def _flash_online_step(q_ref, kt_ref, v_ref, m_ref, l_ref, acc_ref, *, n_sub, per_sub_k, tk, ctx, masked):
    ki = pl.program_id(2)
    v = v_ref[...]
    for s in range(n_sub):
        q = q_ref[:, s * HEAD_DIM:(s + 1) * HEAD_DIM]
        kt = kt_ref[s * HEAD_DIM:(s + 1) * HEAD_DIM, :] if per_sub_k else kt_ref[...]
        sc = jnp.dot(q, kt, preferred_element_type=F32)
        if masked:
            col = ki * tk + lax.broadcasted_iota(jnp.int32, sc.shape, 1)
            sc = jnp.where(col < ctx, sc, -jnp.inf)
        m_prev = m_ref[s]
        m_new = jnp.maximum(m_prev, jnp.max(sc, axis=-1, keepdims=True))
        alpha = jnp.exp2(m_prev - m_new)
        p = jnp.exp2(sc - m_new)
        l_ref[s] = alpha * l_ref[s] + jnp.sum(p, axis=-1, keepdims=True)
        acc_ref[s] = alpha * acc_ref[s] + jnp.dot(p.astype(v.dtype), v, preferred_element_type=F32)
        m_ref[s] = m_new


def _flash_online_kernel(q_ref, kt_ref, v_ref, *rest, n_sub, per_sub_k, tq, tk, ctx, nkv, lam_init):
    if lam_init is None:
        lam_ref = subln_ref = None
        o_ref, m_ref, l_ref, acc_ref = rest
    else:
        lam_ref, subln_ref, o_ref, m_ref, l_ref, acc_ref = rest
    qi = pl.program_id(1)
    ki = pl.program_id(2)
    q_is_ctx = (qi + 1) * tq <= ctx
    step = functools.partial(_flash_online_step, q_ref, kt_ref, v_ref, m_ref, l_ref, acc_ref,
                             n_sub=n_sub, per_sub_k=per_sub_k, tk=tk, ctx=ctx)

    @pl.when(ki == 0)
    def _():
        m_ref[...] = jnp.full(m_ref.shape, -jnp.inf, F32)
        l_ref[...] = jnp.zeros_like(l_ref)
        acc_ref[...] = jnp.zeros_like(acc_ref)

    @pl.when(jnp.logical_not(q_is_ctx))
    def _():
        step(masked=False)

    @pl.when(jnp.logical_and(q_is_ctx, ki * tk < ctx))
    def _():
        step(masked=True)

    @pl.when(ki == nkv - 1)
    def _():
        _flash_finalize(o_ref, [l_ref[s] for s in range(n_sub)], acc_ref, lam_ref, subln_ref, n_sub=n_sub,
                        lam_init=lam_init)


def flash_attention(qk, v, ctx, score_bound, *, groups, n_sub, per_sub_k, lam=None, subln=None, lam_init=None,
                    name="flash"):
    t = qk.shape[0]
    g = groups
    qw = n_sub * HEAD_DIM
    kw = qk.shape[1] // g - qw
    dv = v.shape[1] // g
    ow = dv if lam_init is not None else qw
    tq = _tile(ctx, (256, 128, 64, 32, 16, 8))
    tk = _tile(t, (1280, 1024, 768, 512, 256, 128))
    assert t % tq == 0 and ctx % tq == 0 and ctx <= tk and ctx % V7X_LANES == 0
    assert per_sub_k == (lam_init is not None)
    nkv = t // tk
    n_ctx_q = ctx // tq
    extra_specs3 = extra_specs2 = []
    extra_inputs = []
    if lam_init is not None:
        extra_specs3 = [pl.BlockSpec(lam.shape, lambda h, qi, ki: (0, 0)),
                        pl.BlockSpec((1, dv), lambda h, qi, ki: (0, 0))]
        extra_specs2 = [pl.BlockSpec(lam.shape, lambda h, qi: (0, 0)), pl.BlockSpec((1, dv), lambda h, qi: (0, 0))]
        extra_inputs = [lam, subln.reshape(1, dv)]
    out_shape = jax.ShapeDtypeStruct((t, g * ow), BF16)
    tile_bytes = _nbytes((tq, qw), BF16) + _nbytes((tq, ow), BF16)
    temp_bytes = 4 * _nbytes((tq, tk), F32) + n_sub * _nbytes((tq, dv), F32)
    once = pl.Buffered(1)

    def keys_t(qk):
        return qk[:, g * qw:].T

    def direct_per_sub_k(qk, v):
        tqd = next(c for c in (1280, 768, 512, 256, tq) if t % c == 0 and c % ctx == 0)
        kt_chunks = keys_t(qk).reshape(g, kw, nkv, tk).transpose(0, 2, 1, 3)
        blocks = _nbytes((tqd, qw), BF16) + _nbytes((tqd, ow), BF16)
        resident = _nbytes((kw, t), BF16) + _nbytes((t, dv), BF16)
        scratch_bytes = n_sub * (_nbytes((tqd, V7X_LANES), F32) + _nbytes((tqd, dv), F32))
        return pl.pallas_call(
            functools.partial(_flash_direct_kernel, n_sub=n_sub, tq=tqd, tc=tk, ctx=ctx, n_chunks=nkv,
                              lam_init=lam_init),
            out_shape=out_shape,
            grid=(g, t // tqd),
            in_specs=[pl.BlockSpec((tqd, qw), lambda h, qi: (qi, h)),
                      pl.BlockSpec((None, nkv, kw, tk), lambda h, qi: (h, 0, 0, 0), pipeline_mode=once),
                      pl.BlockSpec((t, dv), lambda h, qi: (0, h), pipeline_mode=once)] + extra_specs2,
            out_specs=pl.BlockSpec((tqd, ow), lambda h, qi: (qi, h)),
            scratch_shapes=[pltpu.VMEM((n_sub, tqd, V7X_LANES), F32), pltpu.VMEM((n_sub, tqd, dv), F32)],
            compiler_params=pltpu.CompilerParams(
                dimension_semantics=("arbitrary", "arbitrary"),
                vmem_limit_bytes=_vmem_limit(
                    blocks, resident + scratch_bytes + 3 * _nbytes((tqd, tk), F32) + 3 * _nbytes((tqd, dv), F32))),
            name=name + "_direct",
        )(qk, kt_chunks, v, *extra_inputs)

    def direct_shared_k(qk, v):
        cols = n_sub * tq
        qt = qk[:, :g * qw].T
        vt_chunks = v.T.reshape(g, dv, nkv, tk).transpose(0, 2, 1, 3)
        k_col0 = g * qw // kw
        blocks = 2 * _nbytes((qw, tq), BF16)
        resident = _nbytes((t, kw), BF16) + _nbytes((dv, t), BF16)
        scratch_bytes = _nbytes((HEAD_DIM, cols), BF16) + _nbytes((8, cols), F32) + _nbytes((dv, cols), F32)
        out_t = pl.pallas_call(
            functools.partial(_flash_direct_shared_kernel, n_sub=n_sub, tq=tq, tc=tk, ctx=ctx, n_chunks=nkv),
            out_shape=jax.ShapeDtypeStruct((g * qw, t), BF16),
            grid=(g, t // tq),
            in_specs=[pl.BlockSpec((qw, tq), lambda h, qi: (h, qi)),
                      pl.BlockSpec((t, kw), lambda h, qi: (0, k_col0 + h), pipeline_mode=once),
                      pl.BlockSpec((None, nkv, dv, tk), lambda h, qi: (h, 0, 0, 0), pipeline_mode=once)],
            out_specs=pl.BlockSpec((qw, tq), lambda h, qi: (h, qi)),
            scratch_shapes=[pltpu.VMEM((HEAD_DIM, cols), BF16), pltpu.VMEM((1, cols), F32),
                            pltpu.VMEM((dv, cols), F32)],
            compiler_params=pltpu.CompilerParams(
                dimension_semantics=("arbitrary", "arbitrary"),
                vmem_limit_bytes=_vmem_limit(blocks, resident + scratch_bytes + 3 * _nbytes((tk, cols), F32))),
            name=name + "_direct",
        )(qt, qk, vt_chunks)
        return out_t.T

    def online(qk, v):
        kt = keys_t(qk)
        q = qk
        last_ctx_kv = (ctx - 1) // tk

        def kv_index(qi, ki):
            return jnp.where(qi < n_ctx_q, jnp.minimum(ki, last_ctx_kv), ki)

        blocks = tile_bytes + _nbytes((kw, tk), BF16) + _nbytes((tk, dv), BF16)
        return pl.pallas_call(
            functools.partial(_flash_online_kernel, n_sub=n_sub, per_sub_k=per_sub_k, tq=tq, tk=tk, ctx=ctx, nkv=nkv,
                              lam_init=lam_init),
            out_shape=out_shape,
            grid=(g, t // tq, nkv),
            in_specs=[pl.BlockSpec((tq, qw), lambda h, qi, ki: (qi, h)),
                      pl.BlockSpec((kw, tk), lambda h, qi, ki: (h, kv_index(qi, ki))),
                      pl.BlockSpec((tk, dv), lambda h, qi, ki: (kv_index(qi, ki), h))] + extra_specs3,
            out_specs=pl.BlockSpec((tq, ow), lambda h, qi, ki: (qi, h)),
            scratch_shapes=[pltpu.VMEM((n_sub, tq, 1), F32), pltpu.VMEM((n_sub, tq, 1), F32),
                            pltpu.VMEM((n_sub, tq, dv), F32)],
            compiler_params=pltpu.CompilerParams(
                dimension_semantics=("arbitrary", "arbitrary", "arbitrary"),
                vmem_limit_bytes=_vmem_limit(blocks, temp_bytes + 2 * n_sub * _nbytes((tq, V7X_LANES), F32))),
            name=name + "_online",
        )(q, kt, v, *extra_inputs)

    direct = direct_per_sub_k if per_sub_k else direct_shared_k
    return lax.cond(score_bound <= SCORE_BOUND_DIRECT, direct, online, qk, v)


def _ffn_in_kernel(h_ref, hp_ref, hn_ref, wg_ref, wu_ref, cw_ref, cb_ref, o_ref, abuf, gbuf, *, tm, ctx, total):
    abuf[0:BF16_ROWS, :] = hp_ref[...]
    abuf[BF16_ROWS:BF16_ROWS + tm, :] = h_ref[...]
    abuf[BF16_ROWS + tm:, :] = hn_ref[...]
    gbuf[...] = jnp.dot(abuf[...], wg_ref[...], preferred_element_type=F32)
    up = jnp.dot(h_ref[...], wu_ref[...], preferred_element_type=F32)
    t = pl.program_id(1) * tm + lax.broadcasted_iota(jnp.int32, (tm, 1), 0)
    has_prev = jnp.logical_and(t != 0, t != ctx)
    has_next = jnp.logical_and(t != ctx - 1, t != total - 1)
    before = jnp.where(has_prev, gbuf[BF16_ROWS - 1:BF16_ROWS - 1 + tm, :], 0.0)
    after = jnp.where(has_next, gbuf[BF16_ROWS + 1:BF16_ROWS + 1 + tm, :], 0.0)
    conv = (before * cw_ref[0:1, :] + gbuf[BF16_ROWS:BF16_ROWS + tm, :] * cw_ref[1:2, :] + after * cw_ref[2:3, :]
            + cb_ref[...])
    o_ref[...] = (jax.nn.silu(conv) * up).astype(o_ref.dtype)


def ffn_in_glu(h, w_gate, w_up, conv_w, conv_b, ctx):
    t, d = h.shape
    fp = w_gate.shape[1]
    tm, tn, _ = _mm_tiles(t, d, fp)
    hb = tm // BF16_ROWS
    last = t // BF16_ROWS - 1
    once = pl.Buffered(1)
    blocks = _nbytes((tm + 2 * BF16_ROWS, d), BF16) + _nbytes((tm, tn), BF16) + 4 * _nbytes((8, tn), F32)
    resident = 2 * _nbytes((d, tn), BF16)
    scratch = _nbytes((tm + 2 * BF16_ROWS, d), BF16) + _nbytes((tm + 2 * BF16_ROWS, tn), F32)
    return pl.pallas_call(
        functools.partial(_ffn_in_kernel, tm=tm, ctx=ctx, total=t),
        out_shape=jax.ShapeDtypeStruct((t, fp), BF16),
        grid=(fp // tn, t // tm),
        in_specs=[
            pl.BlockSpec((tm, d), lambda j, i: (i, 0)),
            pl.BlockSpec((BF16_ROWS, d), lambda j, i: (jnp.maximum(i * hb - 1, 0), 0)),
            pl.BlockSpec((BF16_ROWS, d), lambda j, i: (jnp.minimum((i + 1) * hb, last), 0)),
            pl.BlockSpec((d, tn), lambda j, i: (0, j), pipeline_mode=once),
            pl.BlockSpec((d, tn), lambda j, i: (0, j), pipeline_mode=once),
            pl.BlockSpec((3, tn), lambda j, i: (0, j)),
            pl.BlockSpec((1, tn), lambda j, i: (0, j)),
        ],
        out_specs=pl.BlockSpec((tm, tn), lambda j, i: (i, j)),
        scratch_shapes=[pltpu.VMEM((tm + 2 * BF16_ROWS, d), BF16), pltpu.VMEM((tm + 2 * BF16_ROWS, tn), F32)],
        compiler_params=pltpu.CompilerParams(
            dimension_semantics=("arbitrary", "arbitrary"),
            vmem_limit_bytes=_vmem_limit(blocks, resident + scratch + 5 * _nbytes((tm, tn), F32))),
        name="ffn_in_glu",
    )(h, h, h, w_gate, w_up, conv_w, conv_b.reshape(1, fp))


def _rw_mix_kernel(x_ref, xp_ref, xn_ref, g_ref, sh_ref, sc_ref, mix_ref, *rest, rows, ctx, total):
    outs, buf_ref, inv_ref = rest[:6], rest[6], rest[7]
    row0 = pl.program_id(0) * rows
    pad = V7X_SUBLANES

    def nm(x, first_row):
        t = first_row + lax.broadcasted_iota(jnp.int32, (x.shape[0], 1), 0)
        return _norm_mod_rows(x, g_ref[...], sh_ref[...], sc_ref[...], t < ctx)

    buf_ref[0:pad, :] = nm(xp_ref[...], row0 - pad)
    buf_ref[pad + rows:, :] = nm(xn_ref[...], row0 + rows)

    def emit(sl, value):
        buf_ref[pad:pad + rows, sl] = value

    _norm_mod_stream(x_ref, inv_ref, rows, row0, g_ref, sh_ref, sc_ref, ctx, emit)
    t = row0 + lax.broadcasted_iota(jnp.int32, (rows, 1), 0)
    has_prev = jnp.logical_and(t != 0, t != ctx)
    has_next = jnp.logical_and(t != ctx - 1, t != total - 1)
    for j in range(x_ref.shape[1] // V7X_LANES):
        sl = slice(j * V7X_LANES, (j + 1) * V7X_LANES)
        h = buf_ref[pad:pad + rows, sl]
        before = jnp.where(has_prev, buf_ref[pad - 1:pad - 1 + rows, sl], 0.0)
        after = jnp.where(has_next, buf_ref[pad + 1:pad + 1 + rows, sl], 0.0)
        xx = 0.5 * (before + after) - h
        for n in range(6):
            outs[n][:, sl] = (h + xx * mix_ref[n:n + 1, sl]).astype(outs[n].dtype)


def rw_token_mix(x, g, shift, scale, mix, ctx):
    t, d = x.shape
    rows = _tile(ctx, (128, 64, 32, 16))
    assert t % rows == 0 and ctx % rows == 0 and rows % BF16_ROWS == 0
    rb = rows // V7X_SUBLANES
    last8 = t // V7X_SUBLANES - 1
    blocks = _nbytes((rows + 16, d), F32) + 6 * _nbytes((rows, d), BF16) + 4 * _nbytes((8, d), F32)
    return pl.pallas_call(
        functools.partial(_rw_mix_kernel, rows=rows, ctx=ctx, total=t),
        out_shape=[jax.ShapeDtypeStruct((t, d), BF16)] * 6,
        grid=(t // rows,),
        in_specs=[
            pl.BlockSpec((rows, d), lambda i: (i, 0)),
            pl.BlockSpec((V7X_SUBLANES, d), lambda i: (jnp.maximum(i * rb - 1, 0), 0)),
            pl.BlockSpec((V7X_SUBLANES, d), lambda i: (jnp.minimum((i + 1) * rb, last8), 0)),
            pl.BlockSpec((1, d), lambda i: (0, 0)),
            pl.BlockSpec((2, d), lambda i: (0, 0)),
            pl.BlockSpec((2, d), lambda i: (0, 0)),
            pl.BlockSpec((6, d), lambda i: (0, 0)),
        ],
        out_specs=[pl.BlockSpec((rows, d), lambda i: (i, 0))] * 6,
        scratch_shapes=[pltpu.VMEM((rows + 2 * V7X_SUBLANES, d), F32), pltpu.VMEM((rows, V7X_LANES), F32)],
        compiler_params=pltpu.CompilerParams(
            dimension_semantics=("arbitrary",),
            vmem_limit_bytes=_vmem_limit(blocks, 3 * _nbytes((rows + 16, d), F32))),
        name="rw_token_mix",
    )(x, x, x, g.reshape(1, d), shift, scale, mix)


SCAN_STEPS = 32
PREP_UNROLL = 4


def _mirror_block(i, n_blocks, n_ctx_blocks):
    return jnp.where(i < n_ctx_blocks, n_ctx_blocks - 1 - i, n_blocks + n_ctx_blocks - 1 - i)


def _scan_specs(t, chans, lanes, ctx):
    assert ctx % SCAN_STEPS == 0 and t % SCAN_STEPS == 0
    nb, nbc = t // SCAN_STEPS, ctx // SCAN_STEPS
    here = pl.BlockSpec((SCAN_STEPS, chans, lanes), lambda i: (i, 0, 0))
    mirror = pl.BlockSpec((SCAN_STEPS, chans, lanes), lambda i: (_mirror_block(i, nb, nbc), 0, 0))
    return nb, here, mirror


def _rw_prep_kernel(ra, rb, ka, kb, va, vb, wla, wlb, ala, alb, w0_ref, a0_ref, kk_ref, ka_ref, rk_ref,
                    decay_o, a_o, b_o, kd_o, wr_o, v_o, bonus_o, br_o, kr_o):
    lanes = wla.shape[-1]
    half = lanes // 2
    is_fwd = lax.broadcasted_iota(jnp.int32, (1, lanes), 1) < half

    def body(s, decay_before):
        sb = SCAN_STEPS - 1 - s

        def pick(xa, xb):
            return jnp.where(is_fwd, xa[s], xb[sb])

        def join(xa, xb):
            fwd, bwd = xa[s], xb[sb]
            low = jnp.where(is_fwd, fwd, pltpu.roll(bwd, half, 1))
            high = jnp.where(is_fwd, pltpu.roll(fwd, half, 1), bwd)
            return jnp.concatenate([low, high], axis=0)

        r, k, v = join(ra, rb), join(ka, kb), join(va, vb)
        kk = k * kk_ref[...]
        kk = kk * lax.rsqrt(jnp.maximum(jnp.sum(kk * kk, axis=0, keepdims=True), 1e-24))
        w = jnp.exp(-math.exp(-0.5) * jax.nn.sigmoid(w0_ref[...] + pick(wla, wlb)))
        iclr = jax.nn.sigmoid(a0_ref[...] + pick(ala, alb))
        kd = k * (1.0 + (iclr - 1.0) * ka_ref[...])
        b = kk * iclr
        decay = decay_before * w
        undo = 1.0 / decay
        a_o[s] = -kk * decay_before
        b_o[s] = b * undo
        kd_o[s] = kd * undo
        wr_o[s] = r * decay
        v_o[s] = v
        br_o[s] = jnp.sum(b * r, axis=0, keepdims=True)
        kr_o[s] = jnp.sum(kd * r, axis=0, keepdims=True)
        bonus_o[s] = jnp.sum(r * kd * rk_ref[...], axis=0, keepdims=True) * v
        return decay

    decay_o[...] = lax.fori_loop(0, SCAN_STEPS, body, jnp.ones(decay_o.shape, F32), unroll=PREP_UNROLL)


def _wkv_kernel(*refs, chans):
    prep_in, (y_ref, bonus_ref), scratch = refs[:15], refs[15:17], refs[17:]
    decay_ref, a_ref, b_ref, kd_ref, wr_ref, v_ref, br_ref, kr_ref, s_ref = scratch
    _rw_prep_kernel(*prep_in, decay_ref, a_ref, b_ref, kd_ref, wr_ref, v_ref, bonus_ref, br_ref, kr_ref)

    @pl.when(pl.program_id(0) == 0)
    def _():
        s_ref[...] = jnp.zeros_like(s_ref)

    def reduce_state(i):
        sa = jnp.zeros(s_ref.shape[1:], F32)
        u = jnp.zeros(s_ref.shape[1:], F32)
        for c in range(chans):
            sc = s_ref[c]
            sa = sa + sc * a_ref[i, c:c + 1, :]
            u = u + sc * wr_ref[i, c:c + 1, :]
        return sa, u

    def emit(i, sa, u, vv):
        y_ref[i] = u + sa * br_ref[i] + vv * kr_ref[i]

    def update(i, c, sa, vv):
        return s_ref[c] + sa * b_ref[i, c:c + 1, :] + vv * kd_ref[i, c:c + 1, :]

    def step(i, carry):
        sa, u = carry
        vv = v_ref[i]
        emit(i, sa, u, vv)
        sa_next = jnp.zeros(s_ref.shape[1:], F32)
        u_next = jnp.zeros(s_ref.shape[1:], F32)
        for c in range(chans):
            sn = update(i, c, sa, vv)
            s_ref[c] = sn
            sa_next = sa_next + sn * a_ref[i + 1, c:c + 1, :]
            u_next = u_next + sn * wr_ref[i + 1, c:c + 1, :]
        return sa_next, u_next

    last = SCAN_STEPS - 1
    sa, u = lax.fori_loop(0, last, step, reduce_state(0))
    vv = v_ref[last]
    emit(last, sa, u, vv)
    for c in range(chans):
        s_ref[c] = update(last, c, sa, vv) * decay_ref[c:c + 1, :]


def wkv_scan(r, k, v, wl, al, w0, a0, k_k, k_a, r_k, ctx):
    t, chans, lanes = wl.shape
    nb, here, mirror = _scan_specs(t, chans, lanes, ctx)
    _, here_h, mirror_h = _scan_specs(t, chans // 2, lanes, ctx)
    par = pl.BlockSpec((chans, lanes), lambda i: (0, 0))
    big = jax.ShapeDtypeStruct((t, chans, lanes), F32)
    step_block = pltpu.VMEM((SCAN_STEPS, chans, lanes), F32)
    step_row = pltpu.VMEM((SCAN_STEPS, 1, lanes), F32)
    blocks = 12 * _nbytes((SCAN_STEPS, chans, lanes), F32) + 5 * _nbytes((chans, lanes), F32)
    scratch_bytes = (5 * _nbytes((SCAN_STEPS, chans, lanes), F32) + 2 * _nbytes((SCAN_STEPS, 8, lanes), F32)
                     + _nbytes((chans, lanes), F32) + _nbytes((chans, chans, lanes), F32))
    return pl.pallas_call(
        functools.partial(_wkv_kernel, chans=chans),
        out_shape=[big, big],
        grid=(nb,),
        in_specs=[here_h, mirror_h] * 3 + [here, mirror] * 2 + [par] * 5,
        out_specs=[here, here],
        scratch_shapes=([pltpu.VMEM((chans, lanes), F32)] + [step_block] * 5 + [step_row] * 2
                        + [pltpu.VMEM((chans, chans, lanes), F32)]),
        compiler_params=_params(("arbitrary",), blocks, scratch_bytes + 16 * _nbytes((chans, lanes), F32)),
        name="wkv_scan",
    )(r, r, k, k, v, v, wl, wl, al, al, w0, a0, k_k, k_a, r_k)


def _rw_finish_kernel(ya, yb, ba, bb, gate_ref, lnw_ref, lnb_ref, o_ref):
    lanes = ya.shape[-1]
    half = lanes // 2
    packed = ya.shape[1] // 2
    is_low = lax.broadcasted_iota(jnp.int32, (1, lanes), 1) < half

    def body(s, carry):
        sb = SCAN_STEPS - 1 - s
        y = ya[s] + pltpu.roll(yb[sb], half, 1)
        bonus = ba[s] + pltpu.roll(bb[sb], half, 1)
        mu = jnp.mean(y, axis=0, keepdims=True)
        yc = y - mu
        var = jnp.mean(yc * yc, axis=0, keepdims=True)
        z = yc * lax.rsqrt(var + RW_LN_EPS) * lnw_ref[...] + lnb_ref[...] + bonus
        z_packed = jnp.where(is_low, z[:packed], pltpu.roll(z[packed:], half, 1))
        o_ref[s] = (z_packed * gate_ref[s]).astype(o_ref.dtype)
        return carry

    lax.fori_loop(0, SCAN_STEPS, body, 0, unroll=PREP_UNROLL)


def rw_finish(y, bonus, gate, ln_w, ln_b, ctx):
    t, chans, lanes = y.shape
    nb, here, mirror = _scan_specs(t, chans, lanes, ctx)
    half = pl.BlockSpec((SCAN_STEPS, chans // 2, lanes), lambda i: (i, 0, 0))
    par = pl.BlockSpec((chans, lanes), lambda i: (0, 0))
    blocks = 6 * _nbytes((SCAN_STEPS, chans, lanes), F32) + 2 * _nbytes((chans, lanes), F32)
    return pl.pallas_call(
        _rw_finish_kernel,
        out_shape=jax.ShapeDtypeStruct((t, chans // 2, lanes), BF16),
        grid=(nb,),
        in_specs=[here, mirror, here, mirror, half, par, par],
        out_specs=half,
        compiler_params=_params(("arbitrary",), blocks, 8 * _nbytes((chans, lanes), F32)),
        name="rw_finish",
    )(y, y, bonus, bonus, gate, ln_w, ln_b)


def _rope_tables(ctx, seq):
    pos = jnp.arange(seq)
    row = (pos // GRID_W).astype(F32)
    col = (pos % GRID_W).astype(F32)
    n_freq = HEAD_DIM // 4
    inv_freq = ROPE_THETA ** (-jnp.arange(n_freq, dtype=F32) / n_freq)
    ang = jnp.concatenate([row[:, None] * inv_freq, col[:, None] * inv_freq], axis=-1)
    cos = jnp.concatenate([jnp.ones((ctx, HEAD_DIM // 2), F32), jnp.cos(ang)], axis=0)
    sin = jnp.concatenate([jnp.zeros((ctx, HEAD_DIM // 2), F32), jnp.sin(ang)], axis=0)
    return jnp.concatenate([cos, cos], axis=-1), jnp.concatenate([-sin, sin], axis=-1)


def _channel_major(w):
    heads = w.shape[-1] // RW_HEAD_DIM
    return jnp.swapaxes(w.reshape(w.shape[:-1] + (heads, RW_HEAD_DIM)), -1, -2)


def _channel_packed(w):
    heads = w.shape[-1] // RW_HEAD_DIM
    w4 = w.reshape(w.shape[:-1] + (heads, 2, RW_HEAD_DIM // 2))
    return jnp.moveaxis(w4, -3, -1).swapaxes(-3, -2).reshape(w.shape)


def _per_direction(w2):
    cm = _channel_major(w2)
    zero = jnp.zeros_like(cm[0])
    top = jnp.concatenate([cm[0], zero], axis=-1)
    bottom = jnp.concatenate([zero, cm[1]], axis=-1)
    return jnp.concatenate([top, bottom], axis=0).reshape(2 * w2.shape[1], -1)


def _score_gain(q_g, k_g):
    q_gain = q_g * (HEAD_DIM ** -0.5 * math.log2(math.e))
    bound = HEAD_DIM * jnp.max(jnp.abs(q_gain)) * jnp.max(jnp.abs(k_g))
    return q_gain, bound


def _diff_lambda_init(layer_idx):
    return 0.8 - 0.6 * math.exp(-0.3 * layer_idx)


def _diff_attention(h, cc, ss, wqkv, wo, q_g, k_g, lam_vecs, subln_g, lam_init, x, gate, ctx):
    d = h.shape[1]
    n_heads = d // HEAD_DIM
    q_gain, bound = _score_gain(q_g, k_g)
    gain = jnp.concatenate([jnp.tile(q_gain, n_heads), jnp.tile(k_g, n_heads)])
    qk = matmul_headnorm_rope(h, wqkv[:, :2 * d].astype(BF16), gain, cc, ss, name="da_qk_proj")
    v = matmul(h, wqkv[:, 2 * d:].astype(BF16), BF16, name="da_v_proj")
    o = flash_attention(qk, v, ctx, bound, groups=n_heads // 2, n_sub=2, per_sub_k=True, lam=lam_vecs,
                        subln=subln_g, lam_init=lam_init, name="da_flash")
    return matmul_residual(o, wo.astype(BF16), x, gate, ctx, name="da_out_proj")


def _gqa_attention(h, cc, ss, wqkv, wo, q_g, k_g, x, gate, ctx):
    d = h.shape[1]
    kv_w = d // GA_GROUP
    q_gain, bound = _score_gain(q_g, k_g)
    gain = jnp.concatenate([jnp.tile(q_gain, d // HEAD_DIM), jnp.tile(k_g, kv_w // HEAD_DIM)])
    qk = matmul_headnorm_rope(h, wqkv[:, :d + kv_w].astype(BF16), gain, cc, ss, name="ga_qk_proj")
    v = matmul(h, wqkv[:, d + kv_w:].astype(BF16), BF16, name="ga_v_proj")
    o = flash_attention(qk, v, ctx, bound, groups=kv_w // HEAD_DIM, n_sub=GA_GROUP, per_sub_k=False,
                        name="ga_flash")
    return matmul_residual(o, wo.astype(BF16), x, gate, ctx, name="ga_out_proj")


def _pad_cols(w, n):
    return jnp.pad(w, ((0, 0), (0, n - w.shape[1])))


def _rwkv7(x, g, shift, scale, mix, wrkv, wo, w0, w1, w2, a0, a1, a2, g1, g2, k_k, k_a, r_k, ln_w, ln_b, gate,
           ctx):
    t, d = x.shape
    heads = d // RW_HEAD_DIM
    scan_shape = (t, RW_HEAD_DIM, 2 * heads)
    packed_shape = (t, RW_HEAD_DIM // 2, 2 * heads)
    xr, xw, xk, xv, xa, xg = rw_token_mix(x, g, shift, scale, mix, ctx)

    def shared(xin, w, name):
        return matmul(xin, _channel_packed(w).astype(BF16), F32, name=name).reshape(packed_shape)

    r = shared(xr, wrkv[0], "rw_r_proj")
    k = shared(xk, wrkv[1], "rw_k_proj")
    v = shared(xv, wrkv[2], "rw_v_proj")
    hw = matmul(xw, jnp.concatenate([w1[0], w1[1]], axis=1).astype(BF16), BF16, act="tanh", name="rw_w_lora_in")
    ha = matmul(xa, jnp.concatenate([a1[0], a1[1]], axis=1).astype(BF16), BF16, name="rw_a_lora_in")
    wl = matmul(hw, _per_direction(w2).astype(BF16), F32, name="rw_w_lora_out").reshape(scan_shape)
    al = matmul(ha, _per_direction(a2).astype(BF16), F32, name="rw_a_lora_out").reshape(scan_shape)
    lg = -(-g1.shape[1] // V7X_LANES) * V7X_LANES
    hg = matmul(xg, _pad_cols(g1, lg).astype(BF16), BF16, act="sigmoid", name="rw_g_lora_in")
    g2p = jnp.pad(_channel_packed(g2), ((0, lg - g2.shape[0]), (0, 0))).astype(BF16)
    out_gate = matmul(hg, g2p, F32, name="rw_g_lora_out").reshape(packed_shape)

    def dir_param(p):
        return jnp.concatenate([_channel_major(p[0]), _channel_major(p[1])], axis=-1)

    def shared_param(p):
        return jnp.concatenate([_channel_major(p)] * 2, axis=-1)

    y, bonus = wkv_scan(r, k, v, wl, al, dir_param(w0), dir_param(a0), shared_param(k_k), shared_param(k_a),
                        shared_param(r_k.reshape(d)), ctx)
    z = rw_finish(y, bonus, out_gate, shared_param(ln_w), shared_param(ln_b), ctx).reshape(t, d)
    wo_packed = _channel_packed(wo.T).T
    return matmul_residual(z, wo_packed.astype(BF16), x, gate, ctx, name="rw_out_proj")


def _conv_glu(h, w_in, conv_w, conv_b, w_out, x, gate, ctx):
    f = conv_w.shape[1]
    fp = -(-f // FFN_PAD) * FFN_PAD
    act = ffn_in_glu(h, _pad_cols(w_in[:, :f], fp).astype(BF16), _pad_cols(w_in[:, f:], fp).astype(BF16),
                     _pad_cols(conv_w, fp), jnp.pad(conv_b, (0, fp - f)), ctx)
    w_out_p = jnp.pad(w_out, ((0, fp - f), (0, 0))).astype(BF16)
    return matmul_residual(act, w_out_p, x, gate, ctx, name="ffn_out_proj")


def kernel(x, c, ctx, c_ctx, ada_down, ada_up, ada_b, norm_g, ffn_in, ffn_conv, ffn_conv_b, ffn_out, da_wqkv, da_wo, da_q_g, da_k_g, da_lambda, da_subln_g, rw_mix, rw_wrkv, rw_wo, rw_w0, rw_w1, rw_w2, rw_a0, rw_a1, rw_a2, rw_g1, rw_g2, rw_k_k, rw_k_a, rw_r_k, rw_ln_w, rw_ln_b, ga_wqkv, ga_wo, ga_q_g, ga_k_g):
    assert x.shape[0] == 1
    seq, d = x.shape[1], x.shape[2]
    n_ctx = ctx.shape[1]
    depth = ada_down.shape[0]
    stream = jnp.concatenate([ctx[0], x[0]], axis=0)
    cond = jnp.zeros((V7X_SUBLANES, d), F32).at[0].set(c_ctx).at[1].set(c[0])
    mod = ada_modulation(cond, ada_down, ada_up, ada_b)[:, :2].reshape(depth, 2, N_MOD, d)
    cc, ss = _rope_tables(n_ctx, seq)
    for i in range(depth):
        kind, j = i % N_MIXERS, i // N_MIXERS
        m = [mod[i, :, n, :] for n in range(N_MOD)]
        if kind == 1:
            stream = _rwkv7(stream, norm_g[i, 0], m[0], m[1], rw_mix[j], rw_wrkv[j], rw_wo[j], rw_w0[j], rw_w1[j],
                            rw_w2[j], rw_a0[j], rw_a1[j], rw_a2[j], rw_g1[j], rw_g2[j], rw_k_k[j], rw_k_a[j],
                            rw_r_k[j], rw_ln_w[j], rw_ln_b[j], m[2], n_ctx)
        else:
            h = norm_mod(stream, norm_g[i, 0], m[0], m[1], n_ctx)
            if kind == 0:
                stream = _diff_attention(h, cc, ss, da_wqkv[j], da_wo[j], da_q_g[j], da_k_g[j], da_lambda[j],
                                         da_subln_g[j], _diff_lambda_init(i), stream, m[2], n_ctx)
            else:
                stream = _gqa_attention(h, cc, ss, ga_wqkv[j], ga_wo[j], ga_q_g[j], ga_k_g[j], stream, m[2], n_ctx)
        h2 = norm_mod(stream, norm_g[i, 1], m[3], m[4], n_ctx)
        stream = _conv_glu(h2, ffn_in[i], ffn_conv[i], ffn_conv_b[i], ffn_out[i], stream, m[5], n_ctx)
    return stream[n_ctx:][None]
```

```python
import functools
import math

import jax
import jax.numpy as jnp
from jax import lax
from jax.experimental import pallas as pl
from jax.experimental.pallas import tpu as pltpu

F32 = jnp.float32
BF16 = jnp.bfloat16

N_MOD = 6
N_MIXERS = 3
GRID_W = 64
ROPE_THETA = 10000.0
NORM_EPS = 1e-6
HEAD_DIM = 128
GA_GROUP = 4
RW_HEAD_DIM = 64
RW_LN_EPS = 64e-5

V7X_LANES = 128
V7X_SUBLANES = 8
BF16_ROWS = 16
QK_ROW_SLABS = 4
V7X_VMEM_BYTES = 64 * 1024 * 1024
V7X_VMEM_RESERVE = 6 * 1024 * 1024
FFN_PAD = 1024


def _vmem_limit(block_bytes, temp_bytes=0):
    return int(min(2 * block_bytes + temp_bytes + V7X_VMEM_RESERVE, V7X_VMEM_BYTES - V7X_VMEM_RESERVE))


def _params(semantics, block_bytes, temp_bytes=0):
    return pltpu.CompilerParams(dimension_semantics=semantics,
                                vmem_limit_bytes=_vmem_limit(block_bytes, temp_bytes))


def _tile(n, prefs):
    for t in prefs:
        if n % t == 0:
            return t
    return n


def _nbytes(shape, dtype):
    return math.prod(shape) * jnp.dtype(dtype).itemsize


def _row_ids(i, rows, shape):
    return i * rows + lax.broadcasted_iota(jnp.int32, shape, 0)


def _ada_kernel(cond_ref, down_ref, up_ref, b_ref, o_ref):
    s = jax.nn.silu(cond_ref[...])
    t = jnp.dot(s, down_ref[...], preferred_element_type=F32, precision=lax.Precision.HIGHEST)
    m = jnp.dot(t, up_ref[...], preferred_element_type=F32, precision=lax.Precision.HIGHEST)
    o_ref[...] = m + b_ref[...]


def ada_modulation(cond, ada_down, ada_up, ada_b):
    depth, d, r = ada_down.shape
    n = ada_up.shape[-1]
    tn = _tile(n, (4096, 2048, 1024, 512))
    blocks = _nbytes((8, d), F32) + _nbytes((d, r), F32) + _nbytes((r, tn), F32) + 2 * _nbytes((8, tn), F32)
    return pl.pallas_call(
        _ada_kernel,
        out_shape=jax.ShapeDtypeStruct((depth, 8, n), F32),
        grid=(depth, n // tn),
        in_specs=[
            pl.BlockSpec((8, d), lambda l, j: (0, 0)),
            pl.BlockSpec((None, d, r), lambda l, j: (l, 0, 0)),
            pl.BlockSpec((None, r, tn), lambda l, j: (l, 0, j)),
            pl.BlockSpec((None, 1, tn), lambda l, j: (l, 0, j)),
        ],
        out_specs=pl.BlockSpec((None, 8, tn), lambda l, j: (l, 0, j)),
        compiler_params=_params(("arbitrary", "arbitrary"), blocks),
        name="ada_modulation",
    )(cond, ada_down, ada_up, ada_b.reshape(depth, 1, n))


def _norm_mod_rows(x, g, shift, scale, is_ctx):
    y = x * lax.rsqrt(jnp.mean(x * x, axis=-1, keepdims=True) + NORM_EPS) * g
    sc = jnp.where(is_ctx, scale[0:1, :], scale[1:2, :])
    sh = jnp.where(is_ctx, shift[0:1, :], shift[1:2, :])
    return y * (1.0 + sc) + sh


def _norm_mod_stream(x_ref, inv_ref, n_rows, first_stream_row, g_ref, sh_ref, sc_ref, ctx, emit):
    d = x_ref.shape[1]
    lane_blocks = [slice(j * V7X_LANES, (j + 1) * V7X_LANES) for j in range(d // V7X_LANES)]

    def row_stats(c, carry):
        rows = pl.ds(pl.multiple_of(c * BF16_ROWS, BF16_ROWS), BF16_ROWS)
        ssq = jnp.zeros((BF16_ROWS, V7X_LANES), F32)
        for sl in lane_blocks:
            xb = x_ref[rows, sl]
            ssq = ssq + xb * xb
        inv = lax.rsqrt(jnp.sum(ssq, axis=-1, keepdims=True) * (1.0 / d) + NORM_EPS)
        inv_ref[rows, :] = jnp.broadcast_to(inv, (BF16_ROWS, V7X_LANES))
        return carry

    n_chunks = n_rows // BF16_ROWS
    lax.fori_loop(0, n_chunks, row_stats, 0, unroll=math.gcd(n_chunks, PREP_UNROLL))
    is_ctx = first_stream_row < ctx
    for sl in lane_blocks:
        gain = g_ref[:, sl] * (1.0 + jnp.where(is_ctx, sc_ref[0:1, sl], sc_ref[1:2, sl]))
        shift = jnp.where(is_ctx, sh_ref[0:1, sl], sh_ref[1:2, sl])
        emit(sl, x_ref[0:n_rows, sl] * inv_ref[...] * gain + shift)


def _norm_mod_kernel(x_ref, g_ref, sh_ref, sc_ref, o_ref, inv_ref, *, rows, ctx):
    def emit(sl, value):
        o_ref[:, sl] = value.astype(o_ref.dtype)

    _norm_mod_stream(x_ref, inv_ref, rows, pl.program_id(0) * rows, g_ref, sh_ref, sc_ref, ctx, emit)


def norm_mod(x, g, shift, scale, ctx):
    t, d = x.shape
    rows = _tile(ctx, (256, 128, 64, 32, 16))
    assert t % rows == 0 and ctx % rows == 0 and rows % BF16_ROWS == 0
    blocks = _nbytes((rows, d), F32) + _nbytes((rows, d), BF16) + 5 * _nbytes((8, d), F32)
    return pl.pallas_call(
        functools.partial(_norm_mod_kernel, rows=rows, ctx=ctx),
        out_shape=jax.ShapeDtypeStruct((t, d), BF16),
        grid=(t // rows,),
        in_specs=[
            pl.BlockSpec((rows, d), lambda i: (i, 0)),
            pl.BlockSpec((1, d), lambda i: (0, 0)),
            pl.BlockSpec((2, d), lambda i: (0, 0)),
            pl.BlockSpec((2, d), lambda i: (0, 0)),
        ],
        out_specs=pl.BlockSpec((rows, d), lambda i: (i, 0)),
        scratch_shapes=[pltpu.VMEM((rows, V7X_LANES), F32)],
        compiler_params=_params(("arbitrary",), blocks, _nbytes((rows, d), F32)),
        name="norm_mod",
    )(x, g.reshape(1, d), shift, scale)


def _mm_plain_kernel(a_ref, b_ref, o_ref, *, act):
    acc = jnp.dot(a_ref[...], b_ref[...], preferred_element_type=F32)
    if act == "tanh":
        acc = jnp.tanh(acc)
    elif act == "sigmoid":
        acc = jax.nn.sigmoid(acc)
    o_ref[...] = acc.astype(o_ref.dtype)


def _mm_residual_kernel(a_ref, b_ref, x_ref, g_ref, o_ref, *, tm, ctx):
    acc = jnp.dot(a_ref[...], b_ref[...], preferred_element_type=F32)
    is_ctx = _row_ids(pl.program_id(1), tm, (tm, 1)) < ctx
    gate = jnp.where(is_ctx, g_ref[0:1, :], g_ref[1:2, :])
    o_ref[...] = x_ref[...] + gate * acc


def _mm_headnorm_rope_kernel(a_ref, b_ref, g_ref, cc_ref, ss_ref, o_ref, *, tn):
    tm = a_ref.shape[0]
    n_slabs = QK_ROW_SLABS if tm % (QK_ROW_SLABS * BF16_ROWS) == 0 else 1
    rows = tm // n_slabs
    for r in range(n_slabs):
        rs = slice(r * rows, (r + 1) * rows)
        cc = cc_ref[rs, :]
        ss = ss_ref[rs, :]
        acc = jnp.dot(a_ref[rs, :], b_ref[...], preferred_element_type=F32)
        for h in range(tn // HEAD_DIM):
            sl = slice(h * HEAD_DIM, (h + 1) * HEAD_DIM)
            z = acc[:, sl]
            y = z * lax.rsqrt(jnp.mean(z * z, axis=-1, keepdims=True) + NORM_EPS) * g_ref[:, sl]
            o_ref[rs, sl] = (y * cc + pltpu.roll(y, HEAD_DIM // 2, 1) * ss).astype(o_ref.dtype)


MM_WEIGHT_TILE_BYTES = 12 * 1024 * 1024
MM_DOUBLE_BUFFER_BYTES = 8 * 1024 * 1024


def _mm_tiles(m, k, n):
    tm = _tile(m, (640, 512, 384, 256, 128, 64, 32, 16, 8))
    tn = next(c for c in (1024, 512, 256, 128, n) if n % c == 0 and _nbytes((k, c), BF16) <= MM_WEIGHT_TILE_BYTES)
    return tm, tn, _nbytes((k, tn), BF16) > MM_DOUBLE_BUFFER_BYTES


def _mm_call(kernel, a, b, extra_inputs, extra_specs, out_dtype, extra_bytes, name):
    m, k = a.shape
    n = b.shape[1]
    tm, tn, deep = _mm_tiles(m, k, n)
    weights = _nbytes((k, tn), b.dtype)
    blocks = _nbytes((tm, k), a.dtype) + _nbytes((tm, tn), out_dtype) + extra_bytes(tm, tn)
    b_spec = pl.BlockSpec((k, tn), lambda j, i: (0, j), **({"pipeline_mode": pl.Buffered(1)} if deep else {}))
    return pl.pallas_call(
        kernel,
        out_shape=jax.ShapeDtypeStruct((m, n), out_dtype),
        grid=(n // tn, m // tm),
        in_specs=[pl.BlockSpec((tm, k), lambda j, i: (i, 0)), b_spec] + extra_specs(tm, tn),
        out_specs=pl.BlockSpec((tm, tn), lambda j, i: (i, j)),
        compiler_params=pltpu.CompilerParams(
            dimension_semantics=("arbitrary", "arbitrary"),
            vmem_limit_bytes=_vmem_limit(blocks, (1 if deep else 2) * weights + 2 * _nbytes((tm, tn), F32))),
        name=name,
    )(a, b, *extra_inputs)


def matmul(a, b, out_dtype, act=None, name="matmul"):
    return _mm_call(functools.partial(_mm_plain_kernel, act=act), a, b, (), lambda tm, tn: [],
                    out_dtype, lambda tm, tn: 0, name)


def matmul_residual(a, b, x, gate, ctx, name="matmul_residual"):
    tm = _mm_tiles(a.shape[0], a.shape[1], b.shape[1])[0]

    def specs(tm_, tn):
        return [pl.BlockSpec((tm_, tn), lambda j, i: (i, j)), pl.BlockSpec((2, tn), lambda j, i: (0, j))]

    return _mm_call(functools.partial(_mm_residual_kernel, tm=tm, ctx=ctx), a, b, (x, gate), specs, F32,
                    lambda tm_, tn: _nbytes((tm_, tn), F32) + _nbytes((8, tn), F32), name)


def matmul_headnorm_rope(a, b, gain, cc, ss, name="matmul_headnorm_rope"):
    m, k = a.shape
    n = b.shape[1]
    tm, tn, _ = _mm_tiles(m, k, n)
    blocks = (_nbytes((tm, k), a.dtype) + _nbytes((k, tn), b.dtype) + _nbytes((tm, tn), BF16)
              + _nbytes((8, tn), F32) + 2 * _nbytes((tm, HEAD_DIM), F32) + _nbytes((tm, tn), F32))
    return pl.pallas_call(
        functools.partial(_mm_headnorm_rope_kernel, tn=tn),
        out_shape=jax.ShapeDtypeStruct((m, n), BF16),
        grid=(n // tn, m // tm),
        in_specs=[
            pl.BlockSpec((tm, k), lambda j, i: (i, 0)),
            pl.BlockSpec((k, tn), lambda j, i: (0, j)),
            pl.BlockSpec((1, tn), lambda j, i: (0, j)),
            pl.BlockSpec((tm, HEAD_DIM), lambda j, i: (i, 0)),
            pl.BlockSpec((tm, HEAD_DIM), lambda j, i: (i, 0)),
        ],
        out_specs=pl.BlockSpec((tm, tn), lambda j, i: (i, j)),
        compiler_params=_params(("arbitrary", "arbitrary"), blocks, 2 * _nbytes((tm, tn), F32)),
        name=name,
    )(a, b, gain.reshape(1, n), cc, ss)


SCORE_BOUND_DIRECT = 64.0
FLASH_CHUNKS_PER_TRIP = 3


def _flash_finalize(o_ref, l_rows, acc_ref, lam_ref, subln_ref, *, n_sub, lam_init):
    if lam_init is None:
        for s in range(n_sub):
            o_ref[:, s * HEAD_DIM:(s + 1) * HEAD_DIM] = (acc_ref[s] / l_rows[s]).astype(o_ref.dtype)
        return
    lv = lam_ref[...]
    lam = (jnp.exp(jnp.sum(lv[0:1, :] * lv[1:2, :], axis=-1, keepdims=True))
           - jnp.exp(jnp.sum(lv[2:3, :] * lv[3:4, :], axis=-1, keepdims=True)) + lam_init)
    o = acc_ref[0] / l_rows[0] - lam * (acc_ref[1] / l_rows[1])
    o = o * lax.rsqrt(jnp.mean(o * o, axis=-1, keepdims=True) + NORM_EPS) * subln_ref[...]
    o_ref[...] = (o * (1.0 - lam_init)).astype(o_ref.dtype)


def _chunk_loop(n_chunks, visit):
    per = FLASH_CHUNKS_PER_TRIP
    lead = n_chunks % per
    for c in range(lead):
        visit(c)

    def trip(j, carry):
        for u in range(per):
            visit(lead + per * j + u)
        return carry

    lax.fori_loop(0, n_chunks // per, trip, 0)


def _flash_direct_kernel(q_ref, kt_ref, v_ref, lam_ref, subln_ref, o_ref, l_ref, acc_ref, *, n_sub, tq, tc, ctx,
                         n_chunks, lam_init):
    l_ref[...] = jnp.zeros_like(l_ref)
    acc_ref[...] = jnp.zeros_like(acc_ref)

    def accumulate(chunk, width, v, r0, nr):
        for s in range(n_sub):
            cols = slice(s * HEAD_DIM, (s + 1) * HEAD_DIM)
            p = jnp.exp2(jnp.dot(q_ref[r0:r0 + nr, cols], kt_ref[chunk, cols, 0:width], preferred_element_type=F32))
            part = p[:, 0:V7X_LANES]
            for j in range(1, width // V7X_LANES):
                part = part + p[:, j * V7X_LANES:(j + 1) * V7X_LANES]
            l_ref[s, r0:r0 + nr, :] += part
            acc_ref[s, r0:r0 + nr, :] += jnp.dot(p.astype(v.dtype), v, preferred_element_type=F32)

    def all_keys(r0, nr):
        _chunk_loop(n_chunks, lambda c: accumulate(c, tc, v_ref[pl.ds(pl.multiple_of(c * tc, tc), tc), :], r0, nr))

    @pl.when(pl.program_id(1) > 0)
    def _():
        all_keys(0, tq)

    @pl.when(pl.program_id(1) == 0)
    def _():
        accumulate(0, ctx, v_ref[0:ctx, :], 0, ctx)
        if ctx < tq:
            all_keys(ctx, tq - ctx)

    l_rows = [jnp.sum(l_ref[s], axis=-1, keepdims=True) for s in range(n_sub)]
    _flash_finalize(o_ref, l_rows, acc_ref, lam_ref, subln_ref, n_sub=n_sub, lam_init=lam_init)


def _flash_direct_shared_kernel(qt_ref, k_ref, vt_ref, ot_ref, qs_ref, l_ref, acc_ref, *, n_sub, tq, tc, ctx, n_chunks):
    for g in range(n_sub):
        qs_ref[:, g * tq:(g + 1) * tq] = qt_ref[g * HEAD_DIM:(g + 1) * HEAD_DIM, :]
    l_ref[...] = jnp.zeros_like(l_ref)
    acc_ref[...] = jnp.zeros_like(acc_ref)

    def accumulate(k, vt):
        pt = jnp.exp2(jnp.dot(k, qs_ref[...], preferred_element_type=F32))
        l_ref[...] += jnp.sum(pt, axis=0, keepdims=True)
        acc_ref[...] += jnp.dot(vt, pt.astype(vt.dtype), preferred_element_type=F32)

    @pl.when(pl.program_id(1) > 0)
    def _():
        _chunk_loop(n_chunks, lambda c: accumulate(k_ref[pl.ds(pl.multiple_of(c * tc, tc), tc), :], vt_ref[c]))

    @pl.when(pl.program_id(1) == 0)
    def _():
        accumulate(k_ref[0:ctx, :], vt_ref[0, :, 0:ctx])

    out = acc_ref[...] / l_ref[...]
    for g in range(n_sub):
        ot_ref[g * HEAD_DIM:(g + 1) * HEAD_DIM, :] = out[:, g * tq:(g + 1) * tq].astype(ot_ref.dtype)


def _flash_online_step(q_ref, kt_ref, v_ref, m_ref, l_ref, acc_ref, *, n_sub, per_sub_k, tk, ctx, masked):
    ki = pl.program_id(2)
    v = v_ref[...]
    for s in range(n_sub):
        q = q_ref[:, s * HEAD_DIM:(s + 1) * HEAD_DIM]
        kt = kt_ref[s * HEAD_DIM:(s + 1) * HEAD_DIM, :] if per_sub_k else kt_ref[...]
        sc = jnp.dot(q, kt, preferred_element_type=F32)
        if masked:
            col = ki * tk + lax.broadcasted_iota(jnp.int32, sc.shape, 1)
            sc = jnp.where(col < ctx, sc, -jnp.inf)
        m_prev = m_ref[s]
        m_new = jnp.maximum(m_prev, jnp.max(sc, axis=-1, keepdims=True))
        alpha = jnp.exp2(m_prev - m_new)
        p = jnp.exp2(sc - m_new)
        l_ref[s] = alpha * l_ref[s] + jnp.sum(p, axis=-1, keepdims=True)
        acc_ref[s] = alpha * acc_ref[s] + jnp.dot(p.astype(v.dtype), v, preferred_element_type=F32)
        m_ref[s] = m_new


def _flash_online_kernel(q_ref, kt_ref, v_ref, *rest, n_sub, per_sub_k, tq, tk, ctx, nkv, lam_init):
    if lam_init is None:
        lam_ref = subln_ref = None
        o_ref, m_ref, l_ref, acc_ref = rest
    else:
        lam_ref, subln_ref, o_ref, m_ref, l_ref, acc_ref = rest
    qi = pl.program_id(1)
    ki = pl.program_id(2)
    q_is_ctx = (qi + 1) * tq <= ctx
    step = functools.partial(_flash_online_step, q_ref, kt_ref, v_ref, m_ref, l_ref, acc_ref,
                             n_sub=n_sub, per_sub_k=per_sub_k, tk=tk, ctx=ctx)

    @pl.when(ki == 0)
    def _():
        m_ref[...] = jnp.full(m_ref.shape, -jnp.inf, F32)
        l_ref[...] = jnp.zeros_like(l_ref)
        acc_ref[...] = jnp.zeros_like(acc_ref)

    @pl.when(jnp.logical_not(q_is_ctx))
    def _():
        step(masked=False)

    @pl.when(jnp.logical_and(q_is_ctx, ki * tk < ctx))
    def _():
        step(masked=True)

    @pl.when(ki == nkv - 1)
    def _():
        _flash_finalize(o_ref, [l_ref[s] for s in range(n_sub)], acc_ref, lam_ref, subln_ref, n_sub=n_sub,
                        lam_init=lam_init)


def flash_attention(qk, v, ctx, score_bound, *, groups, n_sub, per_sub_k, lam=None, subln=None, lam_init=None,
                    name="flash"):
    t = qk.shape[0]
    g = groups
    qw = n_sub * HEAD_DIM
    kw = qk.shape[1] // g - qw
    dv = v.shape[1] // g
    ow = dv if lam_init is not None else qw
    tq = _tile(ctx, (256, 128, 64, 32, 16, 8))
    tk = _tile(t, (1280, 1024, 768, 512, 256, 128))
    assert t % tq == 0 and ctx % tq == 0 and ctx <= tk and ctx % V7X_LANES == 0
    assert per_sub_k == (lam_init is not None)
    nkv = t // tk
    n_ctx_q = ctx // tq
    extra_specs3 = extra_specs2 = []
    extra_inputs = []
    if lam_init is not None:
        extra_specs3 = [pl.BlockSpec(lam.shape, lambda h, qi, ki: (0, 0)),
                        pl.BlockSpec((1, dv), lambda h, qi, ki: (0, 0))]
        extra_specs2 = [pl.BlockSpec(lam.shape, lambda h, qi: (0, 0)), pl.BlockSpec((1, dv), lambda h, qi: (0, 0))]
        extra_inputs = [lam, subln.reshape(1, dv)]
    out_shape = jax.ShapeDtypeStruct((t, g * ow), BF16)
    tile_bytes = _nbytes((tq, qw), BF16) + _nbytes((tq, ow), BF16)
    temp_bytes = 4 * _nbytes((tq, tk), F32) + n_sub * _nbytes((tq, dv), F32)
    once = pl.Buffered(1)

    def keys_t(qk):
        return qk[:, g * qw:].T

    def direct_per_sub_k(qk, v):
        tqd = next(c for c in (1280, 768, 512, 256, tq) if t % c == 0 and c % ctx == 0)
        kt_chunks = keys_t(qk).reshape(g, kw, nkv, tk).transpose(0, 2, 1, 3)
        blocks = _nbytes((tqd, qw), BF16) + _nbytes((tqd, ow), BF16)
        resident = _nbytes((kw, t), BF16) + _nbytes((t, dv), BF16)
        scratch_bytes = n_sub * (_nbytes((tqd, V7X_LANES), F32) + _nbytes((tqd, dv), F32))
        return pl.pallas_call(
            functools.partial(_flash_direct_kernel, n_sub=n_sub, tq=tqd, tc=tk, ctx=ctx, n_chunks=nkv,
                              lam_init=lam_init),
            out_shape=out_shape,
            grid=(g, t // tqd),
            in_specs=[pl.BlockSpec((tqd, qw), lambda h, qi: (qi, h)),
                      pl.BlockSpec((None, nkv, kw, tk), lambda h, qi: (h, 0, 0, 0), pipeline_mode=once),
                      pl.BlockSpec((t, dv), lambda h, qi: (0, h), pipeline_mode=once)] + extra_specs2,
            out_specs=pl.BlockSpec((tqd, ow), lambda h, qi: (qi, h)),
            scratch_shapes=[pltpu.VMEM((n_sub, tqd, V7X_LANES), F32), pltpu.VMEM((n_sub, tqd, dv), F32)],
            compiler_params=pltpu.CompilerParams(
                dimension_semantics=("arbitrary", "arbitrary"),
                vmem_limit_bytes=_vmem_limit(
                    blocks, resident + scratch_bytes + 3 * _nbytes((tqd, tk), F32) + 3 * _nbytes((tqd, dv), F32))),
            name=name + "_direct",
        )(qk, kt_chunks, v, *extra_inputs)

    def direct_shared_k(qk, v):
        cols = n_sub * tq
        qt = qk[:, :g * qw].T
        vt_chunks = v.T.reshape(g, dv, nkv, tk).transpose(0, 2, 1, 3)
        k_col0 = g * qw // kw
        blocks = 2 * _nbytes((qw, tq), BF16)
        resident = _nbytes((t, kw), BF16) + _nbytes((dv, t), BF16)
        scratch_bytes = _nbytes((HEAD_DIM, cols), BF16) + _nbytes((8, cols), F32) + _nbytes((dv, cols), F32)
        out_t = pl.pallas_call(
            functools.partial(_flash_direct_shared_kernel, n_sub=n_sub, tq=tq, tc=tk, ctx=ctx, n_chunks=nkv),
            out_shape=jax.ShapeDtypeStruct((g * qw, t), BF16),
            grid=(g, t // tq),
            in_specs=[pl.BlockSpec((qw, tq), lambda h, qi: (h, qi)),
                      pl.BlockSpec((t, kw), lambda h, qi: (0, k_col0 + h), pipeline_mode=once),
                      pl.BlockSpec((None, nkv, dv, tk), lambda h, qi: (h, 0, 0, 0), pipeline_mode=once)],
            out_specs=pl.BlockSpec((qw, tq), lambda h, qi: (h, qi)),
            scratch_shapes=[pltpu.VMEM((HEAD_DIM, cols), BF16), pltpu.VMEM((1, cols), F32),
                            pltpu.VMEM((dv, cols), F32)],
            compiler_params=pltpu.CompilerParams(
                dimension_semantics=("arbitrary", "arbitrary"),
                vmem_limit_bytes=_vmem_limit(blocks, resident + scratch_bytes + 3 * _nbytes((tk, cols), F32))),
            name=name + "_direct",
        )(qt, qk, vt_chunks)
        return out_t.T

    def online(qk, v):
        kt = keys_t(qk)
        q = qk
        last_ctx_kv = (ctx - 1) // tk

        def kv_index(qi, ki):
            return jnp.where(qi < n_ctx_q, jnp.minimum(ki, last_ctx_kv), ki)

        blocks = tile_bytes + _nbytes((kw, tk), BF16) + _nbytes((tk, dv), BF16)
        return pl.pallas_call(
            functools.partial(_flash_online_kernel, n_sub=n_sub, per_sub_k=per_sub_k, tq=tq, tk=tk, ctx=ctx, nkv=nkv,
                              lam_init=lam_init),
            out_shape=out_shape,
            grid=(g, t // tq, nkv),
            in_specs=[pl.BlockSpec((tq, qw), lambda h, qi, ki: (qi, h)),
                      pl.BlockSpec((kw, tk), lambda h, qi, ki: (h, kv_index(qi, ki))),
                      pl.BlockSpec((tk, dv), lambda h, qi, ki: (kv_index(qi, ki), h))] + extra_specs3,
            out_specs=pl.BlockSpec((tq, ow), lambda h, qi, ki: (qi, h)),
            scratch_shapes=[pltpu.VMEM((n_sub, tq, 1), F32), pltpu.VMEM((n_sub, tq, 1), F32),
                            pltpu.VMEM((n_sub, tq, dv), F32)],
            compiler_params=pltpu.CompilerParams(
                dimension_semantics=("arbitrary", "arbitrary", "arbitrary"),
                vmem_limit_bytes=_vmem_limit(blocks, temp_bytes + 2 * n_sub * _nbytes((tq, V7X_LANES), F32))),
            name=name + "_online",
        )(q, kt, v, *extra_inputs)

    direct = direct_per_sub_k if per_sub_k else direct_shared_k
    return lax.cond(score_bound <= SCORE_BOUND_DIRECT, direct, online, qk, v)


def _ffn_in_kernel(h_ref, hp_ref, hn_ref, wg_ref, wu_ref, cw_ref, cb_ref, o_ref, abuf, gbuf, *, tm, ctx, total):
    abuf[0:BF16_ROWS, :] = hp_ref[...]
    abuf[BF16_ROWS:BF16_ROWS + tm, :] = h_ref[...]
    abuf[BF16_ROWS + tm:, :] = hn_ref[...]
    gbuf[...] = jnp.dot(abuf[...], wg_ref[...], preferred_element_type=F32)
    up = jnp.dot(h_ref[...], wu_ref[...], preferred_element_type=F32)
    t = pl.program_id(1) * tm + lax.broadcasted_iota(jnp.int32, (tm, 1), 0)
    has_prev = jnp.logical_and(t != 0, t != ctx)
    has_next = jnp.logical_and(t != ctx - 1, t != total - 1)
    before = jnp.where(has_prev, gbuf[BF16_ROWS - 1:BF16_ROWS - 1 + tm, :], 0.0)
    after = jnp.where(has_next, gbuf[BF16_ROWS + 1:BF16_ROWS + 1 + tm, :], 0.0)
    conv = (before * cw_ref[0:1, :] + gbuf[BF16_ROWS:BF16_ROWS + tm, :] * cw_ref[1:2, :] + after * cw_ref[2:3, :]
            + cb_ref[...])
    o_ref[...] = (jax.nn.silu(conv) * up).astype(o_ref.dtype)


def ffn_in_glu(h, w_gate, w_up, conv_w, conv_b, ctx):
    t, d = h.shape
    fp = w_gate.shape[1]
    tm, tn, _ = _mm_tiles(t, d, fp)
    hb = tm // BF16_ROWS
    last = t // BF16_ROWS - 1
    once = pl.Buffered(1)
    blocks = _nbytes((tm + 2 * BF16_ROWS, d), BF16) + _nbytes((tm, tn), BF16) + 4 * _nbytes((8, tn), F32)
    resident = 2 * _nbytes((d, tn), BF16)
    scratch = _nbytes((tm + 2 * BF16_ROWS, d), BF16) + _nbytes((tm + 2 * BF16_ROWS, tn), F32)
    return pl.pallas_call(
        functools.partial(_ffn_in_kernel, tm=tm, ctx=ctx, total=t),
        out_shape=jax.ShapeDtypeStruct((t, fp), BF16),
        grid=(fp // tn, t // tm),
        in_specs=[
            pl.BlockSpec((tm, d), lambda j, i: (i, 0)),
            pl.BlockSpec((BF16_ROWS, d), lambda j, i: (jnp.maximum(i * hb - 1, 0), 0)),
            pl.BlockSpec((BF16_ROWS, d), lambda j, i: (jnp.minimum((i + 1) * hb, last), 0)),
            pl.BlockSpec((d, tn), lambda j, i: (0, j), pipeline_mode=once),
            pl.BlockSpec((d, tn), lambda j, i: (0, j), pipeline_mode=once),
            pl.BlockSpec((3, tn), lambda j, i: (0, j)),
            pl.BlockSpec((1, tn), lambda j, i: (0, j)),
        ],
        out_specs=pl.BlockSpec((tm, tn), lambda j, i: (i, j)),
        scratch_shapes=[pltpu.VMEM((tm + 2 * BF16_ROWS, d), BF16), pltpu.VMEM((tm + 2 * BF16_ROWS, tn), F32)],
        compiler_params=pltpu.CompilerParams(
            dimension_semantics=("arbitrary", "arbitrary"),
            vmem_limit_bytes=_vmem_limit(blocks, resident + scratch + 5 * _nbytes((tm, tn), F32))),
        name="ffn_in_glu",
    )(h, h, h, w_gate, w_up, conv_w, conv_b.reshape(1, fp))


def _rw_mix_kernel(x_ref, xp_ref, xn_ref, g_ref, sh_ref, sc_ref, mix_ref, *rest, rows, ctx, total):
    outs, buf_ref, inv_ref = rest[:6], rest[6], rest[7]
    row0 = pl.program_id(0) * rows
    pad = V7X_SUBLANES

    def nm(x, first_row):
        t = first_row + lax.broadcasted_iota(jnp.int32, (x.shape[0], 1), 0)
        return _norm_mod_rows(x, g_ref[...], sh_ref[...], sc_ref[...], t < ctx)

    buf_ref[0:pad, :] = nm(xp_ref[...], row0 - pad)
    buf_ref[pad + rows:, :] = nm(xn_ref[...], row0 + rows)

    def emit(sl, value):
        buf_ref[pad:pad + rows, sl] = value

    _norm_mod_stream(x_ref, inv_ref, rows, row0, g_ref, sh_ref, sc_ref, ctx, emit)
    t = row0 + lax.broadcasted_iota(jnp.int32, (rows, 1), 0)
    has_prev = jnp.logical_and(t != 0, t != ctx)
    has_next = jnp.logical_and(t != ctx - 1, t != total - 1)
    for j in range(x_ref.shape[1] // V7X_LANES):
        sl = slice(j * V7X_LANES, (j + 1) * V7X_LANES)
        h = buf_ref[pad:pad + rows, sl]
        before = jnp.where(has_prev, buf_ref[pad - 1:pad - 1 + rows, sl], 0.0)
        after = jnp.where(has_next, buf_ref[pad + 1:pad + 1 + rows, sl], 0.0)
        xx = 0.5 * (before + after) - h
        for n in range(6):
            outs[n][:, sl] = (h + xx * mix_ref[n:n + 1, sl]).astype(outs[n].dtype)


def rw_token_mix(x, g, shift, scale, mix, ctx):
    t, d = x.shape
    rows = _tile(ctx, (128, 64, 32, 16))
    assert t % rows == 0 and ctx % rows == 0 and rows % BF16_ROWS == 0
    rb = rows // V7X_SUBLANES
    last8 = t // V7X_SUBLANES - 1
    blocks = _nbytes((rows + 16, d), F32) + 6 * _nbytes((rows, d), BF16) + 4 * _nbytes((8, d), F32)
    return pl.pallas_call(
        functools.partial(_rw_mix_kernel, rows=rows, ctx=ctx, total=t),
        out_shape=[jax.ShapeDtypeStruct((t, d), BF16)] * 6,
        grid=(t // rows,),
        in_specs=[
            pl.BlockSpec((rows, d), lambda i: (i, 0)),
            pl.BlockSpec((V7X_SUBLANES, d), lambda i: (jnp.maximum(i * rb - 1, 0), 0)),
            pl.BlockSpec((V7X_SUBLANES, d), lambda i: (jnp.minimum((i + 1) * rb, last8), 0)),
            pl.BlockSpec((1, d), lambda i: (0, 0)),
            pl.BlockSpec((2, d), lambda i: (0, 0)),
            pl.BlockSpec((2, d), lambda i: (0, 0)),
            pl.BlockSpec((6, d), lambda i: (0, 0)),
        ],
        out_specs=[pl.BlockSpec((rows, d), lambda i: (i, 0))] * 6,
        scratch_shapes=[pltpu.VMEM((rows + 2 * V7X_SUBLANES, d), F32), pltpu.VMEM((rows, V7X_LANES), F32)],
        compiler_params=pltpu.CompilerParams(
            dimension_semantics=("arbitrary",),
            vmem_limit_bytes=_vmem_limit(blocks, 3 * _nbytes((rows + 16, d), F32))),
        name="rw_token_mix",
    )(x, x, x, g.reshape(1, d), shift, scale, mix)


SCAN_STEPS = 32
PREP_UNROLL = 4


def _mirror_block(i, n_blocks, n_ctx_blocks):
    return jnp.where(i < n_ctx_blocks, n_ctx_blocks - 1 - i, n_blocks + n_ctx_blocks - 1 - i)


def _scan_specs(t, chans, lanes, ctx):
    assert ctx % SCAN_STEPS == 0 and t % SCAN_STEPS == 0
    nb, nbc = t // SCAN_STEPS, ctx // SCAN_STEPS
    here = pl.BlockSpec((SCAN_STEPS, chans, lanes), lambda i: (i, 0, 0))
    mirror = pl.BlockSpec((SCAN_STEPS, chans, lanes), lambda i: (_mirror_block(i, nb, nbc), 0, 0))
    return nb, here, mirror


def _rw_prep_kernel(ra, rb, ka, kb, va, vb, wla, wlb, ala, alb, w0_ref, a0_ref, kk_ref, ka_ref, rk_ref,
                    decay_o, a_o, b_o, kd_o, wr_o, v_o, bonus_o, br_o, kr_o):
    lanes = wla.shape[-1]
    half = lanes // 2
    is_fwd = lax.broadcasted_iota(jnp.int32, (1, lanes), 1) < half

    def body(s, decay_before):
        sb = SCAN_STEPS - 1 - s

        def pick(xa, xb):
            return jnp.where(is_fwd, xa[s], xb[sb])

        def join(xa, xb):
            fwd, bwd = xa[s], xb[sb]
            low = jnp.where(is_fwd, fwd, pltpu.roll(bwd, half, 1))
            high = jnp.where(is_fwd, pltpu.roll(fwd, half, 1), bwd)
            return jnp.concatenate([low, high], axis=0)

        r, k, v = join(ra, rb), join(ka, kb), join(va, vb)
        kk = k * kk_ref[...]
        kk = kk * lax.rsqrt(jnp.maximum(jnp.sum(kk * kk, axis=0, keepdims=True), 1e-24))
        w = jnp.exp(-math.exp(-0.5) * jax.nn.sigmoid(w0_ref[...] + pick(wla, wlb)))
        iclr = jax.nn.sigmoid(a0_ref[...] + pick(ala, alb))
        kd = k * (1.0 + (iclr - 1.0) * ka_ref[...])
        b = kk * iclr
        decay = decay_before * w
        undo = 1.0 / decay
        a_o[s] = -kk * decay_before
        b_o[s] = b * undo
        kd_o[s] = kd * undo
        wr_o[s] = r * decay
        v_o[s] = v
        br_o[s] = jnp.sum(b * r, axis=0, keepdims=True)
        kr_o[s] = jnp.sum(kd * r, axis=0, keepdims=True)
        bonus_o[s] = jnp.sum(r * kd * rk_ref[...], axis=0, keepdims=True) * v
        return decay

    decay_o[...] = lax.fori_loop(0, SCAN_STEPS, body, jnp.ones(decay_o.shape, F32), unroll=PREP_UNROLL)


def _wkv_kernel(*refs, chans):
    prep_in, (y_ref, bonus_ref), scratch = refs[:15], refs[15:17], refs[17:]
    decay_ref, a_ref, b_ref, kd_ref, wr_ref, v_ref, br_ref, kr_ref, s_ref = scratch
    _rw_prep_kernel(*prep_in, decay_ref, a_ref, b_ref, kd_ref, wr_ref, v_ref, bonus_ref, br_ref, kr_ref)

    @pl.when(pl.program_id(0) == 0)
    def _():
        s_ref[...] = jnp.zeros_like(s_ref)

    def reduce_state(i):
        sa = jnp.zeros(s_ref.shape[1:], F32)
        u = jnp.zeros(s_ref.shape[1:], F32)
        for c in range(chans):
            sc = s_ref[c]
            sa = sa + sc * a_ref[i, c:c + 1, :]
            u = u + sc * wr_ref[i, c:c + 1, :]
        return sa, u

    def emit(i, sa, u, vv):
        y_ref[i] = u + sa * br_ref[i] + vv * kr_ref[i]

    def update(i, c, sa, vv):
        return s_ref[c] + sa * b_ref[i, c:c + 1, :] + vv * kd_ref[i, c:c + 1, :]

    def step(i, carry):
        sa, u = carry
        vv = v_ref[i]
        emit(i, sa, u, vv)
        sa_next = jnp.zeros(s_ref.shape[1:], F32)
        u_next = jnp.zeros(s_ref.shape[1:], F32)
        for c in range(chans):
            sn = update(i, c, sa, vv)
            s_ref[c] = sn
            sa_next = sa_next + sn * a_ref[i + 1, c:c + 1, :]
            u_next = u_next + sn * wr_ref[i + 1, c:c + 1, :]
        return sa_next, u_next

    last = SCAN_STEPS - 1
    sa, u = lax.fori_loop(0, last, step, reduce_state(0))
    vv = v_ref[last]
    emit(last, sa, u, vv)
    for c in range(chans):
        s_ref[c] = update(last, c, sa, vv) * decay_ref[c:c + 1, :]


def wkv_scan(r, k, v, wl, al, w0, a0, k_k, k_a, r_k, ctx):
    t, chans, lanes = wl.shape
    nb, here, mirror = _scan_specs(t, chans, lanes, ctx)
    _, here_h, mirror_h = _scan_specs(t, chans // 2, lanes, ctx)
    par = pl.BlockSpec((chans, lanes), lambda i: (0, 0))
    big = jax.ShapeDtypeStruct((t, chans, lanes), F32)
    step_block = pltpu.VMEM((SCAN_STEPS, chans, lanes), F32)
    step_row = pltpu.VMEM((SCAN_STEPS, 1, lanes), F32)
    blocks = 12 * _nbytes((SCAN_STEPS, chans, lanes), F32) + 5 * _nbytes((chans, lanes), F32)
    scratch_bytes = (5 * _nbytes((SCAN_STEPS, chans, lanes), F32) + 2 * _nbytes((SCAN_STEPS, 8, lanes), F32)
                     + _nbytes((chans, lanes), F32) + _nbytes((chans, chans, lanes), F32))
    return pl.pallas_call(
        functools.partial(_wkv_kernel, chans=chans),
        out_shape=[big, big],
        grid=(nb,),
        in_specs=[here_h, mirror_h] * 3 + [here, mirror] * 2 + [par] * 5,
        out_specs=[here, here],
        scratch_shapes=([pltpu.VMEM((chans, lanes), F32)] + [step_block] * 5 + [step_row] * 2
                        + [pltpu.VMEM((chans, chans, lanes), F32)]),
        compiler_params=_params(("arbitrary",), blocks, scratch_bytes + 16 * _nbytes((chans, lanes), F32)),
        name="wkv_scan",
    )(r, r, k, k, v, v, wl, wl, al, al, w0, a0, k_k, k_a, r_k)


def _rw_finish_kernel(ya, yb, ba, bb, gate_ref, lnw_ref, lnb_ref, o_ref):
    lanes = ya.shape[-1]
    half = lanes // 2
    packed = ya.shape[1] // 2
    is_low = lax.broadcasted_iota(jnp.int32, (1, lanes), 1) < half

    def body(s, carry):
        sb = SCAN_STEPS - 1 - s
        y = ya[s] + pltpu.roll(yb[sb], half, 1)
        bonus = ba[s] + pltpu.roll(bb[sb], half, 1)
        mu = jnp.mean(y, axis=0, keepdims=True)
        yc = y - mu
        var = jnp.mean(yc * yc, axis=0, keepdims=True)
        z = yc * lax.rsqrt(var + RW_LN_EPS) * lnw_ref[...] + lnb_ref[...] + bonus
        z_packed = jnp.where(is_low, z[:packed], pltpu.roll(z[packed:], half, 1))
        o_ref[s] = (z_packed * gate_ref[s]).astype(o_ref.dtype)
        return carry

    lax.fori_loop(0, SCAN_STEPS, body, 0, unroll=2 * PREP_UNROLL)


def rw_finish(y, bonus, gate, ln_w, ln_b, ctx):
    t, chans, lanes = y.shape
    nb, here, mirror = _scan_specs(t, chans, lanes, ctx)
    half = pl.BlockSpec((SCAN_STEPS, chans // 2, lanes), lambda i: (i, 0, 0))
    par = pl.BlockSpec((chans, lanes), lambda i: (0, 0))
    blocks = 6 * _nbytes((SCAN_STEPS, chans, lanes), F32) + 2 * _nbytes((chans, lanes), F32)
    return pl.pallas_call(
        _rw_finish_kernel,
        out_shape=jax.ShapeDtypeStruct((t, chans // 2, lanes), BF16),
        grid=(nb,),
        in_specs=[here, mirror, here, mirror, half, par, par],
        out_specs=half,
        compiler_params=_params(("arbitrary",), blocks, 8 * _nbytes((chans, lanes), F32)),
        name="rw_finish",
    )(y, y, bonus, bonus, gate, ln_w, ln_b)


def _rope_tables(ctx, seq):
    pos = jnp.arange(seq)
    row = (pos // GRID_W).astype(F32)
    col = (pos % GRID_W).astype(F32)
    n_freq = HEAD_DIM // 4
    inv_freq = ROPE_THETA ** (-jnp.arange(n_freq, dtype=F32) / n_freq)
    ang = jnp.concatenate([row[:, None] * inv_freq, col[:, None] * inv_freq], axis=-1)
    cos = jnp.concatenate([jnp.ones((ctx, HEAD_DIM // 2), F32), jnp.cos(ang)], axis=0)
    sin = jnp.concatenate([jnp.zeros((ctx, HEAD_DIM // 2), F32), jnp.sin(ang)], axis=0)
    return jnp.concatenate([cos, cos], axis=-1), jnp.concatenate([-sin, sin], axis=-1)


def _channel_major(w):
    heads = w.shape[-1] // RW_HEAD_DIM
    return jnp.swapaxes(w.reshape(w.shape[:-1] + (heads, RW_HEAD_DIM)), -1, -2)


def _channel_packed(w):
    heads = w.shape[-1] // RW_HEAD_DIM
    w4 = w.reshape(w.shape[:-1] + (heads, 2, RW_HEAD_DIM // 2))
    return jnp.moveaxis(w4, -3, -1).swapaxes(-3, -2).reshape(w.shape)


def _per_direction(w2):
    cm = _channel_major(w2)
    zero = jnp.zeros_like(cm[0])
    top = jnp.concatenate([cm[0], zero], axis=-1)
    bottom = jnp.concatenate([zero, cm[1]], axis=-1)
    return jnp.concatenate([top, bottom], axis=0).reshape(2 * w2.shape[1], -1)


def _score_gain(q_g, k_g):
    q_gain = q_g * (HEAD_DIM ** -0.5 * math.log2(math.e))
    bound = HEAD_DIM * jnp.max(jnp.abs(q_gain)) * jnp.max(jnp.abs(k_g))
    return q_gain, bound


def _diff_lambda_init(layer_idx):
    return 0.8 - 0.6 * math.exp(-0.3 * layer_idx)


def _diff_attention(h, cc, ss, wqkv, wo, q_g, k_g, lam_vecs, subln_g, lam_init, x, gate, ctx):
    d = h.shape[1]
    n_heads = d // HEAD_DIM
    q_gain, bound = _score_gain(q_g, k_g)
    gain = jnp.concatenate([jnp.tile(q_gain, n_heads), jnp.tile(k_g, n_heads)])
    qk = matmul_headnorm_rope(h, wqkv[:, :2 * d].astype(BF16), gain, cc, ss, name="da_qk_proj")
    v = matmul(h, wqkv[:, 2 * d:].astype(BF16), BF16, name="da_v_proj")
    o = flash_attention(qk, v, ctx, bound, groups=n_heads // 2, n_sub=2, per_sub_k=True, lam=lam_vecs,
                        subln=subln_g, lam_init=lam_init, name="da_flash")
    return matmul_residual(o, wo.astype(BF16), x, gate, ctx, name="da_out_proj")


def _gqa_attention(h, cc, ss, wqkv, wo, q_g, k_g, x, gate, ctx):
    d = h.shape[1]
    kv_w = d // GA_GROUP
    q_gain, bound = _score_gain(q_g, k_g)
    gain = jnp.concatenate([jnp.tile(q_gain, d // HEAD_DIM), jnp.tile(k_g, kv_w // HEAD_DIM)])
    qk = matmul_headnorm_rope(h, wqkv[:, :d + kv_w].astype(BF16), gain, cc, ss, name="ga_qk_proj")
    v = matmul(h, wqkv[:, d + kv_w:].astype(BF16), BF16, name="ga_v_proj")
    o = flash_attention(qk, v, ctx, bound, groups=kv_w // HEAD_DIM, n_sub=GA_GROUP, per_sub_k=False,
                        name="ga_flash")
    return matmul_residual(o, wo.astype(BF16), x, gate, ctx, name="ga_out_proj")


def _pad_cols(w, n):
    return jnp.pad(w, ((0, 0), (0, n - w.shape[1])))


def _rwkv7(x, g, shift, scale, mix, wrkv, wo, w0, w1, w2, a0, a1, a2, g1, g2, k_k, k_a, r_k, ln_w, ln_b, gate,
           ctx):
    t, d = x.shape
    heads = d // RW_HEAD_DIM
    scan_shape = (t, RW_HEAD_DIM, 2 * heads)
    packed_shape = (t, RW_HEAD_DIM // 2, 2 * heads)
    xr, xw, xk, xv, xa, xg = rw_token_mix(x, g, shift, scale, mix, ctx)

    def shared(xin, w, name):
        return matmul(xin, _channel_packed(w).astype(BF16), F32, name=name).reshape(packed_shape)

    r = shared(xr, wrkv[0], "rw_r_proj")
    k = shared(xk, wrkv[1], "rw_k_proj")
    v = shared(xv, wrkv[2], "rw_v_proj")
    hw = matmul(xw, jnp.concatenate([w1[0], w1[1]], axis=1).astype(BF16), BF16, act="tanh", name="rw_w_lora_in")
    ha = matmul(xa, jnp.concatenate([a1[0], a1[1]], axis=1).astype(BF16), BF16, name="rw_a_lora_in")
    wl = matmul(hw, _per_direction(w2).astype(BF16), F32, name="rw_w_lora_out").reshape(scan_shape)
    al = matmul(ha, _per_direction(a2).astype(BF16), F32, name="rw_a_lora_out").reshape(scan_shape)
    lg = -(-g1.shape[1] // V7X_LANES) * V7X_LANES
    hg = matmul(xg, _pad_cols(g1, lg).astype(BF16), BF16, act="sigmoid", name="rw_g_lora_in")
    g2p = jnp.pad(_channel_packed(g2), ((0, lg - g2.shape[0]), (0, 0))).astype(BF16)
    out_gate = matmul(hg, g2p, F32, name="rw_g_lora_out").reshape(packed_shape)

    def dir_param(p):
        return jnp.concatenate([_channel_major(p[0]), _channel_major(p[1])], axis=-1)

    def shared_param(p):
        return jnp.concatenate([_channel_major(p)] * 2, axis=-1)

    y, bonus = wkv_scan(r, k, v, wl, al, dir_param(w0), dir_param(a0), shared_param(k_k), shared_param(k_a),
                        shared_param(r_k.reshape(d)), ctx)
    z = rw_finish(y, bonus, out_gate, shared_param(ln_w), shared_param(ln_b), ctx).reshape(t, d)
    wo_packed = _channel_packed(wo.T).T
    return matmul_residual(z, wo_packed.astype(BF16), x, gate, ctx, name="rw_out_proj")


def _conv_glu(h, w_in, conv_w, conv_b, w_out, x, gate, ctx):
    f = conv_w.shape[1]
    fp = -(-f // FFN_PAD) * FFN_PAD
    act = ffn_in_glu(h, _pad_cols(w_in[:, :f], fp).astype(BF16), _pad_cols(w_in[:, f:], fp).astype(BF16),
                     _pad_cols(conv_w, fp), jnp.pad(conv_b, (0, fp - f)), ctx)
    w_out_p = jnp.pad(w_out, ((0, fp - f), (0, 0))).astype(BF16)
    return matmul_residual(act, w_out_p, x, gate, ctx, name="ffn_out_proj")


def kernel(x, c, ctx, c_ctx, ada_down, ada_up, ada_b, norm_g, ffn_in, ffn_conv, ffn_conv_b, ffn_out, da_wqkv, da_wo, da_q_g, da_k_g, da_lambda, da_subln_g, rw_mix, rw_wrkv, rw_wo, rw_w0, rw_w1, rw_w2, rw_a0, rw_a1, rw_a2, rw_g1, rw_g2, rw_k_k, rw_k_a, rw_r_k, rw_ln_w, rw_ln_b, ga_wqkv, ga_wo, ga_q_g, ga_k_g):
    assert x.shape[0] == 1
    seq, d = x.shape[1], x.shape[2]
    n_ctx = ctx.shape[1]
    depth = ada_down.shape[0]
    stream = jnp.concatenate([ctx[0], x[0]], axis=0)
    cond = jnp.zeros((V7X_SUBLANES, d), F32).at[0].set(c_ctx).at[1].set(c[0])
    mod = ada_modulation(cond, ada_down, ada_up, ada_b)[:, :2].reshape(depth, 2, N_MOD, d)
    cc, ss = _rope_tables(n_ctx, seq)
    for i in range(depth):
        kind, j = i % N_MIXERS, i // N_MIXERS
        m = [mod[i, :, n, :] for n in range(N_MOD)]
        if kind == 1:
            stream = _rwkv7(stream, norm_g[i, 0], m[0], m[1], rw_mix[j], rw_wrkv[j], rw_wo[j], rw_w0[j], rw_w1[j],
                            rw_w2[j], rw_a0[j], rw_a1[j], rw_a2[j], rw_g1[j], rw_g2[j], rw_k_k[j], rw_k_a[j],
                            rw_r_k[j], rw_ln_w[j], rw_ln_b[j], m[2], n_ctx)
        else:
            h = norm_mod(stream, norm_g[i, 0], m[0], m[1], n_ctx)
            if kind == 0:
                stream = _diff_attention(h, cc, ss, da_wqkv[j], da_wo[j], da_q_g[j], da_k_g[j], da_lambda[j],
                                         da_subln_g[j], _diff_lambda_init(i), stream, m[2], n_ctx)
            else:
                stream = _gqa_attention(h, cc, ss, ga_wqkv[j], ga_wo[j], ga_q_g[j], ga_k_g[j], stream, m[2], n_ctx)
        h2 = norm_mod(stream, norm_g[i, 1], m[3], m[4], n_ctx)
        stream = _conv_glu(h2, ffn_in[i], ffn_conv[i], ffn_conv_b[i], ffn_out[i], stream, m[5], n_ctx)
    return stream[n_ctx:][None]
```
